```python
import math
import jax, jax.numpy as jnp
from jax import lax
import numpy as np

D_MODEL = 2048
BATCH = 1
SEQ = 16384
DEPTH = 2

N_DIFF_HEADS = 8
DIFF_HEAD_DIM = 64
DIFF_V_DIM = 2 * DIFF_HEAD_DIM
ATTN_WIDTH = N_DIFF_HEADS * 2 * DIFF_HEAD_DIM
N_FNET_GROUPS = 8
FNET_GROUP_DIM = 128
FNET_WIDTH = N_FNET_GROUPS * FNET_GROUP_DIM
IN_PROJ_WIDTH = 3 * ATTN_WIDTH + FNET_WIDTH
N_BRANCHES = 2
D_FF = 5632
N_SUBLAYERS = 3
N_MOD = 3
ROPE_THETA = 10000.0
Q_BLOCK = 128
NORM_EPS = 1e-6
SUBLN_EPS = 1e-5
LAMBDA_STD = 0.1
MACARON_WEIGHT = 0.5

kernel_name = "hybrid_diffattn_fnet_macaron_encoder"


def rms_norm(x, g, eps=NORM_EPS):
    xf = x.astype(jnp.float32)
    y = xf * lax.rsqrt(jnp.mean(xf * xf, axis=-1, keepdims=True) + eps)
    return (y * g.astype(jnp.float32)).astype(x.dtype)


def modulate(xn, shift, scale):
    return xn * (1.0 + scale[:, None, :]) + shift[:, None, :]


def rope_tables(seq, dim):
    pos = jnp.arange(seq, dtype=jnp.float32)
    inv_freq = ROPE_THETA ** (-jnp.arange(0, dim, 2, dtype=jnp.float32) / dim)
    ang = pos[:, None] * inv_freq[None, :]
    return jnp.cos(ang), jnp.sin(ang)


def apply_rope(t, cos, sin):
    half = t.shape[-1] // 2
    tf = t.astype(jnp.float32)
    t1, t2 = tf[..., :half], tf[..., half:]
    c = cos[None, :, None, None, :]
    s = sin[None, :, None, None, :]
    out = jnp.concatenate([t1 * c - t2 * s, t2 * c + t1 * s], axis=-1)
    return out.astype(t.dtype)


def swiglu(h, w_in, w_out):
    gu = h @ w_in
    g, u = jnp.split(gu, 2, axis=-1)
    return (jax.nn.silu(g) * u) @ w_out


def diff_attention(q, k, v, lam, lambda_init, subln_g):
    b, s, h, _, dh = q.shape
    n_blocks = s // Q_BLOCK
    scale = dh ** -0.5
    qb = q.reshape(b, n_blocks, Q_BLOCK, h, 2, dh).transpose(1, 0, 2, 3, 4, 5)

    def block(q_blk):
        sc = jnp.einsum('bqhmd,bkhmd->bhmqk', q_blk, k,
                        preferred_element_type=jnp.float32) * scale
        p = jax.nn.softmax(sc, axis=-1)
        a = p[:, :, 0] - lam * p[:, :, 1]
        return jnp.einsum('bhqk,bkhe->bqhe', a.astype(v.dtype), v)

    o = lax.map(block, qb)
    o = o.transpose(1, 0, 2, 3, 4).reshape(b, s, h, 2 * dh)
    o = rms_norm(o, subln_g, SUBLN_EPS) * (1.0 - lambda_init)
    return o.reshape(b, s, h * 2 * dh)


def fourier_mix(u):
    b, s, _ = u.shape
    uf = u.astype(jnp.float32).reshape(b, s, N_FNET_GROUPS, FNET_GROUP_DIM)
    y = jnp.fft.fftn(uf, axes=(1, 3), norm="ortho").real
    return y.reshape(b, s, FNET_WIDTH).astype(u.dtype)


def setup_inputs(seed: int = 0) -> dict:
    key = jax.random.key(seed)
    ks = jax.random.split(key, 20)
    L, D, F = DEPTH, D_MODEL, D_FF
    nrm = lambda k, shape, fan_in, mult=1.0: jax.random.normal(k, shape, jnp.float32) * (mult * fan_in ** -0.5)
    return {
        "x": jax.random.normal(ks[0], (BATCH, SEQ, D), jnp.float32),
        "c": jax.random.normal(ks[1], (BATCH, D), jnp.float32),
        "ada_w": nrm(ks[2], (L, D, N_SUBLAYERS * N_MOD * D), D, 0.5),
        "ada_b": 0.01 * jax.random.normal(ks[3], (L, N_SUBLAYERS * N_MOD * D), jnp.float32),
        "pre_norm_g": 1.0 + 0.02 * jax.random.normal(ks[4], (L, N_SUBLAYERS, D), jnp.float32),
        "post_norm_g": 1.0 + 0.02 * jax.random.normal(ks[5], (L, N_SUBLAYERS, D), jnp.float32),
        "ffn1_w_in": nrm(ks[6], (L, D, 2 * F), D),
        "ffn1_w_out": nrm(ks[7], (L, F, D), F),
        "mix_w_in": nrm(ks[8], (L, D, IN_PROJ_WIDTH), D),
        "lambda_qk": LAMBDA_STD * jax.random.normal(ks[9], (L, 4, DIFF_HEAD_DIM), jnp.float32),
        "subln_g": 1.0 + 0.02 * jax.random.normal(ks[10], (L, DIFF_V_DIM), jnp.float32),
        "attn_proj": nrm(ks[11], (L, ATTN_WIDTH, D), ATTN_WIDTH),
        "fnet_proj": nrm(ks[12], (L, FNET_WIDTH, D), FNET_WIDTH),
        "branch_gate_w": nrm(ks[13], (L, D, N_BRANCHES * D), D),
        "branch_gate_b": 0.01 * jax.random.normal(ks[14], (L, N_BRANCHES * D), jnp.float32),
        "mix_w_out": nrm(ks[15], (L, D, D), D),
        "ffn2_w_in": nrm(ks[16], (L, D, 2 * F), D),
        "ffn2_w_out": nrm(ks[17], (L, F, D), F),
    }


def reference(x, c, ada_w, ada_b, pre_norm_g, post_norm_g, ffn1_w_in, ffn1_w_out, mix_w_in,
              lambda_qk, subln_g, attn_proj, fnet_proj, branch_gate_w, branch_gate_b, mix_w_out,
              ffn2_w_in, ffn2_w_out):
    b, s, d = x.shape
    cos, sin = rope_tables(s, DIFF_HEAD_DIM)
    c_act = jax.nn.silu(c)
    for l in range(DEPTH):
        lambda_init = 0.8 - 0.6 * math.exp(-0.3 * l)
        mod = (c_act @ ada_w[l] + ada_b[l]).reshape(b, N_SUBLAYERS, N_MOD, d)

        h = modulate(rms_norm(x, pre_norm_g[l, 0]), mod[:, 0, 0], mod[:, 0, 1])
        y = swiglu(h, ffn1_w_in[l], ffn1_w_out[l])
        x = x + MACARON_WEIGHT * mod[:, 0, 2][:, None, :] * rms_norm(y, post_norm_g[l, 0])

        h = modulate(rms_norm(x, pre_norm_g[l, 1]), mod[:, 1, 0], mod[:, 1, 1])
        z = h @ mix_w_in[l]
        q, k, v, u = jnp.split(z, [ATTN_WIDTH, 2 * ATTN_WIDTH, 3 * ATTN_WIDTH], axis=-1)
        q = apply_rope(q.reshape(b, s, N_DIFF_HEADS, 2, DIFF_HEAD_DIM), cos, sin)
        k = apply_rope(k.reshape(b, s, N_DIFF_HEADS, 2, DIFF_HEAD_DIM), cos, sin)
        v = v.reshape(b, s, N_DIFF_HEADS, DIFF_V_DIM)
        lq = lambda_qk[l].astype(jnp.float32)
        lam = jnp.exp(jnp.sum(lq[0] * lq[1])) - jnp.exp(jnp.sum(lq[2] * lq[3])) + lambda_init
        y_attn = diff_attention(q, k, v, lam, lambda_init, subln_g[l]) @ attn_proj[l]
        y_fnet = fourier_mix(u) @ fnet_proj[l]
        gates = jax.nn.sigmoid(h @ branch_gate_w[l] + branch_gate_b[l])
        g_attn, g_fnet = jnp.split(gates, 2, axis=-1)
        y = (g_attn * y_attn + g_fnet * y_fnet) @ mix_w_out[l]
        x = x + mod[:, 1, 2][:, None, :] * rms_norm(y, post_norm_g[l, 1])

        h = modulate(rms_norm(x, pre_norm_g[l, 2]), mod[:, 2, 0], mod[:, 2, 1])
        y = swiglu(h, ffn2_w_in[l], ffn2_w_out[l])
        x = x + MACARON_WEIGHT * mod[:, 2, 2][:, None, :] * rms_norm(y, post_norm_g[l, 2])
    return x
```

```python
import functools
import math

import numpy as np
import jax
import jax.numpy as jnp
from jax import lax
from jax.experimental import pallas as pl
from jax.experimental.pallas import tpu as pltpu

F32 = jnp.float32
BF16 = jnp.bfloat16

N_HEADS = 8
HEAD_DIM = 64
V_DIM = 2 * HEAD_DIM
ATTN_WIDTH = N_HEADS * V_DIM
N_GROUPS = 8
GROUP_DIM = 128
FNET_WIDTH = N_GROUPS * GROUP_DIM
ROPE_THETA = 10000.0
NORM_EPS = 1e-6
SUBLN_EPS = 1e-5
MACARON_WEIGHT = 0.5

LANES = 128
SUBLANES = 8
VMEM_LIMIT_BYTES = 56 * 1024 * 1024

ADA_TN = 1024
ROW_TILE = 512
FFN_TF = 512
MIX_TC = 512
ATTN_TQ = 256
ATTN_TK = ROW_TILE
FFT_N = 128
FFT_NB = 4
FFT_KB = 4


def _cparams(sem):
    return pltpu.CompilerParams(dimension_semantics=sem, vmem_limit_bytes=VMEM_LIMIT_BYTES)


def _rms(x, g, eps):
    ms = jnp.mean(x * x, axis=-1, keepdims=True)
    return x * lax.rsqrt(ms + eps) * g


def _ada_kernel(c_ref, w_ref, b_ref, o_ref):
    d, tn = w_ref.shape

    def body(k, acc):
        r = pl.multiple_of(k * SUBLANES, SUBLANES)
        cb = c_ref[pl.ds(r, SUBLANES), :]
        cb = cb * jax.nn.sigmoid(cb)
        return acc + w_ref[pl.ds(r, SUBLANES), :] * pltpu.repeat(cb, tn // LANES, axis=1)

    acc = lax.fori_loop(0, d // SUBLANES, body, jnp.zeros((SUBLANES, tn), F32), unroll=8)
    o_ref[...] = jnp.sum(acc, axis=0, keepdims=True) + b_ref[...]


def _ada_mod(c, ada_w, ada_b):
    n_layers, d, n = ada_w.shape
    c_b = jnp.broadcast_to(c.reshape(d, 1), (d, LANES))
    out = pl.pallas_call(
        _ada_kernel,
        grid=(n_layers, n // ADA_TN),
        in_specs=[
            pl.BlockSpec((d, LANES), lambda l, j: (0, 0)),
            pl.BlockSpec((None, d, ADA_TN), lambda l, j: (l, 0, j)),
            pl.BlockSpec((None, 1, ADA_TN), lambda l, j: (l, 0, j)),
        ],
        out_specs=pl.BlockSpec((None, 1, ADA_TN), lambda l, j: (l, 0, j)),
        out_shape=jax.ShapeDtypeStruct((n_layers, 1, n), F32),
        compiler_params=_cparams(("parallel", "parallel")),
        name="ada_mod",
    )(c_b, ada_w, ada_b.reshape(n_layers, 1, n))
    return out


def _ffn_kernel(x_ref, pg_ref, sh_ref, sc_ref, gt_ref, qg_ref, wg_ref, wu_ref, wo_ref,
                o_ref, h_scr, acc_scr):
    j = pl.program_id(1)

    @pl.when(j == 0)
    def _():
        xn = _rms(x_ref[...], pg_ref[...], NORM_EPS)
        h_scr[...] = (xn * (1.0 + sc_ref[...]) + sh_ref[...]).astype(BF16)
        acc_scr[...] = jnp.zeros_like(acc_scr)

    h = h_scr[...]
    g = jnp.dot(h, wg_ref[...], preferred_element_type=F32)
    u = jnp.dot(h, wu_ref[...], preferred_element_type=F32)
    a = (g * jax.nn.sigmoid(g) * u).astype(BF16)
    acc_scr[...] += jnp.dot(a, wo_ref[...], preferred_element_type=F32)

    @pl.when(j == pl.num_programs(1) - 1)
    def _():
        yn = _rms(acc_scr[...], qg_ref[...], NORM_EPS)
        o_ref[...] = x_ref[...] + (MACARON_WEIGHT * gt_ref[...]) * yn


def _ffn(x, pre_g, shift, scale, gate, post_g, w_in, w_out):
    s, d = x.shape
    f = w_out.shape[0]
    nf = f // FFN_TF
    vec = pl.BlockSpec((1, d), lambda i, j: (0, 0))
    return pl.pallas_call(
        _ffn_kernel,
        grid=(s // ROW_TILE, nf),
        in_specs=[
            pl.BlockSpec((ROW_TILE, d), lambda i, j: (i, 0)),
            vec, vec, vec, vec, vec,
            pl.BlockSpec((d, FFN_TF), lambda i, j: (0, j)),
            pl.BlockSpec((d, FFN_TF), lambda i, j: (0, nf + j)),
            pl.BlockSpec((FFN_TF, d), lambda i, j: (j, 0)),
        ],
        out_specs=pl.BlockSpec((ROW_TILE, d), lambda i, j: (i, 0)),
        out_shape=jax.ShapeDtypeStruct((s, d), F32),
        scratch_shapes=[pltpu.VMEM((ROW_TILE, d), BF16), pltpu.VMEM((ROW_TILE, d), F32)],
        compiler_params=_cparams(("parallel", "arbitrary")),
        name="ffn",
    )(x, pre_g, shift, scale, gate, post_g, w_in, w_in, w_out)


def _mixin_kernel(x_ref, pg_ref, sh_ref, sc_ref, w_ref, cos_ref, sinn_ref, sinp_ref,
                  q_ref, k_ref, vt_ref, u_ref, *, q_scale):
    xn = _rms(x_ref[...], pg_ref[...], NORM_EPS)
    h = (xn * (1.0 + sc_ref[...]) + sh_ref[...]).astype(BF16)
    cos_t, sin_n, sin_p = cos_ref[...], sinn_ref[...], sinp_ref[...]

    def rope(zs):
        return (zs * cos_t + pltpu.roll(zs, LANES - HEAD_DIM // 2, 1) * sin_n
                + pltpu.roll(zs, HEAD_DIM // 2, 1) * sin_p)

    zq = jnp.dot(h, w_ref[:, 0:ATTN_WIDTH], preferred_element_type=F32)
    for hd in range(N_HEADS):
        q_ref[hd] = (rope(zq[:, hd * V_DIM:(hd + 1) * V_DIM]) * q_scale).astype(BF16)
    zk = jnp.dot(h, w_ref[:, ATTN_WIDTH:2 * ATTN_WIDTH], preferred_element_type=F32)
    for hd in range(N_HEADS):
        k_ref[hd] = rope(zk[:, hd * V_DIM:(hd + 1) * V_DIM]).astype(BF16)
    zv = jnp.dot(h, w_ref[:, 2 * ATTN_WIDTH:3 * ATTN_WIDTH], preferred_element_type=F32)
    zvt = zv.T
    for hd in range(N_HEADS):
        vt_ref[hd] = zvt[hd * V_DIM:(hd + 1) * V_DIM, :].astype(BF16)
    u_ref[...] = jnp.dot(h, w_ref[:, 3 * ATTN_WIDTH:], preferred_element_type=F32)


def _mix_in(x, pre_g, shift, scale, w, cos_t, sin_n, sin_p, q_scale):
    s, d = x.shape
    n_tiles = s // ROW_TILE
    vec = pl.BlockSpec((1, d), lambda i: (0, 0))
    tab = pl.BlockSpec((ROW_TILE, LANES), lambda i: (i, 0))
    return pl.pallas_call(
        functools.partial(_mixin_kernel, q_scale=q_scale),
        grid=(n_tiles,),
        in_specs=[
            pl.BlockSpec((ROW_TILE, d), lambda i: (i, 0)),
            vec, vec, vec,
            pl.BlockSpec(w.shape, lambda i: (0, 0), pipeline_mode=pl.Buffered(1)),
            tab, tab, tab,
        ],
        out_specs=[
            pl.BlockSpec((N_HEADS, ROW_TILE, V_DIM), lambda i: (0, i, 0)),
            pl.BlockSpec((N_HEADS, ROW_TILE, V_DIM), lambda i: (0, i, 0)),
            pl.BlockSpec((N_HEADS, None, V_DIM, ROW_TILE), lambda i: (0, i, 0, 0)),
            pl.BlockSpec((ROW_TILE, FNET_WIDTH), lambda i: (i, 0)),
        ],
        out_shape=[
            jax.ShapeDtypeStruct((N_HEADS, s, V_DIM), BF16),
            jax.ShapeDtypeStruct((N_HEADS, s, V_DIM), BF16),
            jax.ShapeDtypeStruct((N_HEADS, n_tiles, V_DIM, ROW_TILE), BF16),
            jax.ShapeDtypeStruct((s, FNET_WIDTH), F32),
        ],
        compiler_params=_cparams(("parallel",)),
        name="mix_in",
    )(x, pre_g, shift, scale, w, cos_t, sin_n, sin_p)


def _attn_kernel(lq_ref, g_ref, q_ref, k_ref, vt_ref, o_ref, acc_scr, m_scr, l_scr,
                 *, lambda_init):
    tq = q_ref.shape[0]
    n_kv, _, tk = vt_ref.shape
    q = q_ref[...].astype(F32)
    lane = lax.broadcasted_iota(jnp.int32, q.shape, 1)
    qz = jnp.concatenate([jnp.where(lane < HEAD_DIM, q, 0.0),
                          jnp.where(lane >= HEAD_DIM, q, 0.0)], axis=0).astype(BF16)
    m_scr[...] = jnp.full_like(m_scr, -jnp.inf)
    l_scr[...] = jnp.zeros_like(l_scr)
    acc_scr[...] = jnp.zeros_like(acc_scr)

    def body(j, carry):
        r = pl.multiple_of(j * tk, tk)
        s = lax.dot_general(k_ref[pl.ds(r, tk), :], qz, (((1,), (1,)), ((), ())),
                            preferred_element_type=F32)
        m_prev = m_scr[...]
        m_new = jnp.maximum(m_prev, jnp.max(s, axis=0, keepdims=True))
        alpha = jnp.exp2(m_prev - m_new)
        p = jnp.exp2(s - m_new)
        l_scr[...] = alpha * l_scr[...] + jnp.sum(p, axis=0, keepdims=True)
        acc_scr[...] = acc_scr[...] * alpha + jnp.dot(
            vt_ref[j], p.astype(BF16), preferred_element_type=F32)
        m_scr[...] = m_new
        return carry

    lax.fori_loop(0, n_kv, body, 0)

    lq = lq_ref[...]
    lam = (jnp.exp(jnp.sum(lq[0:1] * lq[1:2], axis=-1, keepdims=True))
           - jnp.exp(jnp.sum(lq[2:3] * lq[3:4], axis=-1, keepdims=True)) + lambda_init)
    on = acc_scr[...] / l_scr[...]
    o = (on[:, :tq] - lam * on[:, tq:]).T
    o_ref[...] = (_rms(o, g_ref[...], SUBLN_EPS) * (1.0 - lambda_init)).astype(BF16)


def _attention(q, k, vt, lambda_qk, subln_g, lambda_init):
    n_heads, s, _ = q.shape
    n_kv = vt.shape[1]
    return pl.pallas_call(
        functools.partial(_attn_kernel, lambda_init=lambda_init),
        grid=(n_heads, s // ATTN_TQ),
        in_specs=[
            pl.BlockSpec(lambda_qk.shape, lambda h, i: (0, 0)),
            pl.BlockSpec((1, V_DIM), lambda h, i: (0, 0)),
            pl.BlockSpec((None, ATTN_TQ, V_DIM), lambda h, i: (h, i, 0)),
            pl.BlockSpec((None, s, V_DIM), lambda h, i: (h, 0, 0)),
            pl.BlockSpec((None, n_kv, V_DIM, ATTN_TK), lambda h, i: (h, 0, 0, 0)),
        ],
        out_specs=pl.BlockSpec((ATTN_TQ, V_DIM), lambda h, i: (i, h)),
        out_shape=jax.ShapeDtypeStruct((s, n_heads * V_DIM), BF16),
        scratch_shapes=[pltpu.VMEM((V_DIM, 2 * ATTN_TQ), F32),
                        pltpu.VMEM((1, 2 * ATTN_TQ), F32),
                        pltpu.VMEM((1, 2 * ATTN_TQ), F32)],
        compiler_params=_cparams(("parallel", "parallel")),
        name="diff_attn",
    )(lambda_qk, subln_g, q, k, vt)


def _dft_tables():
    n = FFT_N
    jk = np.outer(np.arange(n), np.arange(n)) % n
    ang = 2.0 * np.pi * jk / n
    c = np.cos(ang) / math.sqrt(n)
    s = np.sin(ang) / math.sqrt(n)
    stage1 = np.concatenate([c, -s], axis=0)
    stage2 = np.block([[c, s], [-s, c]])
    chan = np.concatenate([c, s], axis=0)
    tw_ang = 2.0 * np.pi * np.outer(np.arange(n), np.arange(n)) / (n * n)
    return (jnp.asarray(stage1, F32), jnp.asarray(stage2, F32), jnp.asarray(chan, F32),
            jnp.asarray(np.cos(tw_ang), F32), jnp.asarray(np.sin(tw_ang), F32))


def _fft1_kernel(x_ref, f_ref, twc_ref, tws_ref, o_ref):
    n = FFT_N
    t = jnp.dot(f_ref[...], x_ref[...], preferred_element_type=F32,
                precision=lax.Precision.HIGHEST)
    width = x_ref.shape[1] // FFT_NB
    for b in range(FFT_NB):
        tr = t[:n, b * width:(b + 1) * width]
        ti = t[n:, b * width:(b + 1) * width]
        c = pltpu.repeat(twc_ref[b], width // LANES, axis=1)
        s = pltpu.repeat(tws_ref[b], width // LANES, axis=1)
        o_ref[0, :, b * width:(b + 1) * width] = tr * c + ti * s
        o_ref[1, :, b * width:(b + 1) * width] = ti * c - tr * s


def _fft2_kernel(t_ref, f_ref, ch_ref, o_ref):
    n = FFT_N
    width = t_ref.shape[3]
    for b in range(FFT_KB):
        tt = jnp.concatenate([t_ref[0, b], t_ref[1, b]], axis=0)
        z = jnp.dot(f_ref[...], tt, preferred_element_type=F32,
                    precision=lax.Precision.HIGHEST)
        for g in range(N_GROUPS):
            zz = jnp.concatenate([z[:n, g * GROUP_DIM:(g + 1) * GROUP_DIM],
                                  z[n:, g * GROUP_DIM:(g + 1) * GROUP_DIM]], axis=1)
            y = jnp.dot(zz, ch_ref[...], preferred_element_type=F32,
                        precision=lax.Precision.HIGHEST)
            o_ref[:, b * width + g * GROUP_DIM:b * width + (g + 1) * GROUP_DIM] = y.astype(o_ref.dtype)


def _fourier_mix(u, tables):
    s, width = u.shape
    n = FFT_N
    stage1, stage2, chan, twc, tws = tables
    twc_b = jnp.broadcast_to(twc[:, :, None], (n, n, LANES))
    tws_b = jnp.broadcast_to(tws[:, :, None], (n, n, LANES))
    t = pl.pallas_call(
        _fft1_kernel,
        grid=(n // FFT_NB,),
        in_specs=[
            pl.BlockSpec((n, FFT_NB * width), lambda j: (0, j)),
            pl.BlockSpec((2 * n, n), lambda j: (0, 0)),
            pl.BlockSpec((FFT_NB, n, LANES), lambda j: (j, 0, 0)),
            pl.BlockSpec((FFT_NB, n, LANES), lambda j: (j, 0, 0)),
        ],
        out_specs=pl.BlockSpec((2, n, FFT_NB * width), lambda j: (0, 0, j)),
        out_shape=jax.ShapeDtypeStruct((2, n, n * width), F32),
        compiler_params=_cparams(("parallel",)),
        name="fft_stage1",
    )(u.reshape(n, n * width), stage1, twc_b, tws_b)
    y = pl.pallas_call(
        _fft2_kernel,
        grid=(n // FFT_KB,),
        in_specs=[
            pl.BlockSpec((2, FFT_KB, n, width), lambda i: (0, i, 0, 0)),
            pl.BlockSpec((2 * n, 2 * n), lambda i: (0, 0)),
            pl.BlockSpec((2 * n, n), lambda i: (0, 0)),
        ],
        out_specs=pl.BlockSpec((n, FFT_KB * width), lambda i: (0, i)),
        out_shape=jax.ShapeDtypeStruct((n, n * width), BF16),
        compiler_params=_cparams(("parallel",)),
        name="fft_stage2",
    )(t.reshape(2, n, n, width), stage2, chan)
    return y.reshape(s, width)


def _mixout_kernel(x_ref, pg_ref, sh_ref, sc_ref, gt_ref, qg_ref, ao_ref, fy_ref,
                   gwa_ref, gwf_ref, gba_ref, gbf_ref, ap_ref, fp_ref, wo_ref,
                   o_ref, h_scr, acc_scr):
    j = pl.program_id(1)

    @pl.when(j == 0)
    def _():
        xn = _rms(x_ref[...], pg_ref[...], NORM_EPS)
        h_scr[...] = (xn * (1.0 + sc_ref[...]) + sh_ref[...]).astype(BF16)
        acc_scr[...] = jnp.zeros_like(acc_scr)

    h = h_scr[...]
    ga = jax.nn.sigmoid(jnp.dot(h, gwa_ref[...], preferred_element_type=F32) + gba_ref[...])
    gf = jax.nn.sigmoid(jnp.dot(h, gwf_ref[...], preferred_element_type=F32) + gbf_ref[...])
    ya = jnp.dot(ao_ref[...], ap_ref[...], preferred_element_type=F32)
    yf = jnp.dot(fy_ref[...], fp_ref[...], preferred_element_type=F32)
    y = (ga * ya + gf * yf).astype(BF16)
    acc_scr[...] += jnp.dot(y, wo_ref[...], preferred_element_type=F32)

    @pl.when(j == pl.num_programs(1) - 1)
    def _():
        yn = _rms(acc_scr[...], qg_ref[...], NORM_EPS)
        o_ref[...] = x_ref[...] + gt_ref[...] * yn


def _mix_out(x, pre_g, shift, scale, gate, post_g, ao, fy, gate_w, gate_b, attn_proj,
             fnet_proj, w_out):
    s, d = x.shape
    nc = d // MIX_TC
    vec = pl.BlockSpec((1, d), lambda i, j: (0, 0))
    return pl.pallas_call(
        _mixout_kernel,
        grid=(s // ROW_TILE, nc),
        in_specs=[
            pl.BlockSpec((ROW_TILE, d), lambda i, j: (i, 0)),
            vec, vec, vec, vec, vec,
            pl.BlockSpec((ROW_TILE, ATTN_WIDTH), lambda i, j: (i, 0)),
            pl.BlockSpec((ROW_TILE, FNET_WIDTH), lambda i, j: (i, 0)),
            pl.BlockSpec((d, MIX_TC), lambda i, j: (0, j)),
            pl.BlockSpec((d, MIX_TC), lambda i, j: (0, nc + j)),
            pl.BlockSpec((1, MIX_TC), lambda i, j: (0, j)),
            pl.BlockSpec((1, MIX_TC), lambda i, j: (0, nc + j)),
            pl.BlockSpec((ATTN_WIDTH, MIX_TC), lambda i, j: (0, j)),
            pl.BlockSpec((FNET_WIDTH, MIX_TC), lambda i, j: (0, j)),
            pl.BlockSpec((MIX_TC, d), lambda i, j: (j, 0)),
        ],
        out_specs=pl.BlockSpec((ROW_TILE, d), lambda i, j: (i, 0)),
        out_shape=jax.ShapeDtypeStruct((s, d), F32),
        scratch_shapes=[pltpu.VMEM((ROW_TILE, d), BF16), pltpu.VMEM((ROW_TILE, d), F32)],
        compiler_params=_cparams(("parallel", "arbitrary")),
        name="mix_out",
    )(x, pre_g, shift, scale, gate, post_g, ao, fy, gate_w, gate_w, gate_b, gate_b,
      attn_proj, fnet_proj, w_out)


def _rope_tables(seq):
    half = HEAD_DIM // 2
    pos = jnp.arange(seq, dtype=F32)
    inv_freq = ROPE_THETA ** (-jnp.arange(0, HEAD_DIM, 2, dtype=F32) / HEAD_DIM)
    ang = pos[:, None] * inv_freq[None, :]
    cos, sin = jnp.cos(ang), jnp.sin(ang)
    zero = jnp.zeros_like(sin)
    cos_t = jnp.concatenate([cos] * (LANES // half), axis=1)
    sin_n = jnp.concatenate([-sin, zero] * (LANES // HEAD_DIM), axis=1)
    sin_p = jnp.concatenate([zero, sin] * (LANES // HEAD_DIM), axis=1)
    return cos_t, sin_n, sin_p


def kernel(x, c, ada_w, ada_b, pre_norm_g, post_norm_g, ffn1_w_in, ffn1_w_out, mix_w_in,
           lambda_qk, subln_g, attn_proj, fnet_proj, branch_gate_w, branch_gate_b, mix_w_out,
           ffn2_w_in, ffn2_w_out):
    b, s, d = x.shape
    assert b == 1 and s == FFT_N * FFT_N and s % ROW_TILE == 0
    n_layers = ada_w.shape[0]
    cos_t, sin_n, sin_p = _rope_tables(s)
    dft = _dft_tables()
    mod = _ada_mod(c, ada_w, ada_b).reshape(n_layers, 3, 3, 1, d)
    q_scale = HEAD_DIM ** -0.5 * math.log2(math.e)
    xs = x.reshape(s, d)
    for l in range(n_layers):
        lambda_init = 0.8 - 0.6 * math.exp(-0.3 * l)
        pre = pre_norm_g[l].reshape(3, 1, d)
        post = post_norm_g[l].reshape(3, 1, d)
        xs = _ffn(xs, pre[0], mod[l, 0, 0], mod[l, 0, 1], mod[l, 0, 2], post[0],
                  ffn1_w_in[l].astype(BF16), ffn1_w_out[l].astype(BF16))
        q, k, vt, u = _mix_in(xs, pre[1], mod[l, 1, 0], mod[l, 1, 1], mix_w_in[l].astype(BF16),
                              cos_t, sin_n, sin_p, q_scale)
        ao = _attention(q, k, vt, lambda_qk[l], subln_g[l].reshape(1, V_DIM), lambda_init)
        fy = _fourier_mix(u, dft)
        xs = _mix_out(xs, pre[1], mod[l, 1, 0], mod[l, 1, 1], mod[l, 1, 2], post[1], ao, fy,
                      branch_gate_w[l].astype(BF16), branch_gate_b[l].reshape(1, 2 * d),
                      attn_proj[l].astype(BF16), fnet_proj[l].astype(BF16),
                      mix_w_out[l].astype(BF16))
        xs = _ffn(xs, pre[2], mod[l, 2, 0], mod[l, 2, 1], mod[l, 2, 2], post[2],
                  ffn2_w_in[l].astype(BF16), ffn2_w_out[l].astype(BF16))
    return xs.reshape(b, s, d)
```

```python
import functools
import math

import numpy as np
import jax
import jax.numpy as jnp
from jax import lax
from jax.experimental import pallas as pl
from jax.experimental.pallas import tpu as pltpu

F32 = jnp.float32
BF16 = jnp.bfloat16

N_HEADS = 8
HEAD_DIM = 64
V_DIM = 2 * HEAD_DIM
ATTN_WIDTH = N_HEADS * V_DIM
N_GROUPS = 8
GROUP_DIM = 128
FNET_WIDTH = N_GROUPS * GROUP_DIM
ROPE_THETA = 10000.0
NORM_EPS = 1e-6
SUBLN_EPS = 1e-5
MACARON_WEIGHT = 0.5

LANES = 128
SUBLANES = 8
MXU_DIM = 256
VMEM_LIMIT_BYTES = 56 * 1024 * 1024

ADA_TN = 1024
ROW_TILE = 512
FFN_TF = 512
MIX_TC = 512
ATTN_TQ = 256
ATTN_TK = ROW_TILE
ATTN_SUB = 2
FFT_N = 128
FFT_NB = 4
FFT_KB = 4


def _cparams(sem):
    return pltpu.CompilerParams(dimension_semantics=sem, vmem_limit_bytes=VMEM_LIMIT_BYTES)


def _rms(x, g, eps):
    ms = jnp.mean(x * x, axis=-1, keepdims=True)
    return x * lax.rsqrt(ms + eps) * g


def _ada_kernel(c_ref, w_ref, b_ref, o_ref):
    d, tn = w_ref.shape

    def body(k, acc):
        r = pl.multiple_of(k * SUBLANES, SUBLANES)
        cb = c_ref[pl.ds(r, SUBLANES), :]
        cb = cb * jax.nn.sigmoid(cb)
        return acc + w_ref[pl.ds(r, SUBLANES), :] * pltpu.repeat(cb, tn // LANES, axis=1)

    acc = lax.fori_loop(0, d // SUBLANES, body, jnp.zeros((SUBLANES, tn), F32), unroll=8)
    o_ref[...] = jnp.sum(acc, axis=0, keepdims=True) + b_ref[...]


def _ada_mod(c, ada_w, ada_b):
    n_layers, d, n = ada_w.shape
    c_b = jnp.broadcast_to(c.reshape(d, 1), (d, LANES))
    out = pl.pallas_call(
        _ada_kernel,
        grid=(n_layers, n // ADA_TN),
        in_specs=[
            pl.BlockSpec((d, LANES), lambda l, j: (0, 0)),
            pl.BlockSpec((None, d, ADA_TN), lambda l, j: (l, 0, j)),
            pl.BlockSpec((None, 1, ADA_TN), lambda l, j: (l, 0, j)),
        ],
        out_specs=pl.BlockSpec((None, 1, ADA_TN), lambda l, j: (l, 0, j)),
        out_shape=jax.ShapeDtypeStruct((n_layers, 1, n), F32),
        compiler_params=_cparams(("parallel", "parallel")),
        name="ada_mod",
    )(c_b, ada_w, ada_b.reshape(n_layers, 1, n))
    return out


def _ffn_kernel(x_ref, pg_ref, sh_ref, sc_ref, gt_ref, qg_ref, wg_ref, wu_ref, wo_ref,
                o_ref, h_scr, acc_scr):
    j = pl.program_id(1)

    @pl.when(j == 0)
    def _():
        xn = _rms(x_ref[...], pg_ref[...], NORM_EPS)
        h_scr[...] = (xn * (1.0 + sc_ref[...]) + sh_ref[...]).astype(BF16)
        acc_scr[...] = jnp.zeros_like(acc_scr)

    h = h_scr[...]
    g = jnp.dot(h, wg_ref[...], preferred_element_type=F32)
    u = jnp.dot(h, wu_ref[...], preferred_element_type=F32)
    a = (g * jax.nn.sigmoid(g) * u).astype(BF16)
    acc_scr[...] += jnp.dot(a, wo_ref[...], preferred_element_type=F32)

    @pl.when(j == pl.num_programs(1) - 1)
    def _():
        yn = _rms(acc_scr[...], qg_ref[...], NORM_EPS)
        o_ref[...] = x_ref[...] + (MACARON_WEIGHT * gt_ref[...]) * yn


def _ffn(x, pre_g, shift, scale, gate, post_g, w_in, w_out):
    s, d = x.shape
    f = w_out.shape[0]
    nf = f // FFN_TF
    vec = pl.BlockSpec((1, d), lambda i, j: (0, 0))
    return pl.pallas_call(
        _ffn_kernel,
        grid=(s // ROW_TILE, nf),
        in_specs=[
            pl.BlockSpec((ROW_TILE, d), lambda i, j: (i, 0)),
            vec, vec, vec, vec, vec,
            pl.BlockSpec((d, FFN_TF), lambda i, j: (0, j)),
            pl.BlockSpec((d, FFN_TF), lambda i, j: (0, nf + j)),
            pl.BlockSpec((FFN_TF, d), lambda i, j: (j, 0)),
        ],
        out_specs=pl.BlockSpec((ROW_TILE, d), lambda i, j: (i, 0)),
        out_shape=jax.ShapeDtypeStruct((s, d), F32),
        scratch_shapes=[pltpu.VMEM((ROW_TILE, d), BF16), pltpu.VMEM((ROW_TILE, d), F32)],
        compiler_params=_cparams(("parallel", "arbitrary")),
        name="ffn",
    )(x, pre_g, shift, scale, gate, post_g, w_in, w_in, w_out)


def _mixin_kernel(x_ref, pg_ref, sh_ref, sc_ref, w_ref, cos_ref, sinn_ref, sinp_ref,
                  q_ref, k_ref, vt_ref, u_ref, *, q_scale):
    xn = _rms(x_ref[...], pg_ref[...], NORM_EPS)
    h = (xn * (1.0 + sc_ref[...]) + sh_ref[...]).astype(BF16)
    cos_t, sin_n, sin_p = cos_ref[...], sinn_ref[...], sinp_ref[...]

    def rope(zs):
        return (zs * cos_t + pltpu.roll(zs, LANES - HEAD_DIM // 2, 1) * sin_n
                + pltpu.roll(zs, HEAD_DIM // 2, 1) * sin_p)

    zq = jnp.dot(h, w_ref[:, 0:ATTN_WIDTH], preferred_element_type=F32)
    for hd in range(N_HEADS):
        q_ref[hd] = (rope(zq[:, hd * V_DIM:(hd + 1) * V_DIM]) * q_scale).astype(BF16)
    zk = jnp.dot(h, w_ref[:, ATTN_WIDTH:2 * ATTN_WIDTH], preferred_element_type=F32)
    for hd in range(N_HEADS):
        k_ref[hd] = rope(zk[:, hd * V_DIM:(hd + 1) * V_DIM]).astype(BF16)
    zv = jnp.dot(h, w_ref[:, 2 * ATTN_WIDTH:3 * ATTN_WIDTH], preferred_element_type=F32)
    zvt = zv.T
    for hd in range(N_HEADS):
        vt_ref[hd] = zvt[hd * V_DIM:(hd + 1) * V_DIM, :].astype(BF16)
    u_ref[...] = jnp.dot(h, w_ref[:, 3 * ATTN_WIDTH:], preferred_element_type=F32)


def _mix_in(x, pre_g, shift, scale, w, cos_t, sin_n, sin_p, q_scale):
    s, d = x.shape
    n_tiles = s // ROW_TILE
    vec = pl.BlockSpec((1, d), lambda i: (0, 0))
    tab = pl.BlockSpec((ROW_TILE, LANES), lambda i: (i, 0))
    return pl.pallas_call(
        functools.partial(_mixin_kernel, q_scale=q_scale),
        grid=(n_tiles,),
        in_specs=[
            pl.BlockSpec((ROW_TILE, d), lambda i: (i, 0)),
            vec, vec, vec,
            pl.BlockSpec(w.shape, lambda i: (0, 0), pipeline_mode=pl.Buffered(1)),
            tab, tab, tab,
        ],
        out_specs=[
            pl.BlockSpec((N_HEADS, ROW_TILE, V_DIM), lambda i: (0, i, 0)),
            pl.BlockSpec((N_HEADS, ROW_TILE, V_DIM), lambda i: (0, i, 0)),
            pl.BlockSpec((N_HEADS, None, V_DIM, ROW_TILE), lambda i: (0, i, 0, 0)),
            pl.BlockSpec((ROW_TILE, FNET_WIDTH), lambda i: (i, 0)),
        ],
        out_shape=[
            jax.ShapeDtypeStruct((N_HEADS, s, V_DIM), BF16),
            jax.ShapeDtypeStruct((N_HEADS, s, V_DIM), BF16),
            jax.ShapeDtypeStruct((N_HEADS, n_tiles, V_DIM, ROW_TILE), BF16),
            jax.ShapeDtypeStruct((s, FNET_WIDTH), F32),
        ],
        compiler_params=_cparams(("parallel",)),
        name="mix_in",
    )(x, pre_g, shift, scale, w, cos_t, sin_n, sin_p)


def _attn_kernel(lq_ref, g_ref, q_ref, k_ref, vt_ref, o_ref,
                 qz_scr, s0_scr, s1_scr, c0_scr, c1_scr, acc_scr, m_scr, l_scr, *, lambda_init):
    tq = q_ref.shape[0]
    n_kv, _, tk = vt_ref.shape
    q = q_ref[...].astype(F32)
    lane = lax.broadcasted_iota(jnp.int32, q.shape, 1)
    qz_scr[...] = jnp.concatenate([jnp.where(lane < HEAD_DIM, q, 0.0),
                                   jnp.where(lane >= HEAD_DIM, q, 0.0)], axis=0).astype(BF16)
    m_scr[...] = jnp.full_like(m_scr, -jnp.inf)
    l_scr[...] = jnp.zeros_like(l_scr)
    acc_scr[...] = jnp.zeros_like(acc_scr)

    n_chunks = n_kv // ATTN_SUB
    ck = ATTN_SUB * tk

    def scores(j, s_scr, c_scr):
        r = pl.multiple_of(j * ck, ck)
        s = lax.dot_general(k_ref[pl.ds(r, ck), :], qz_scr[...], (((1,), (1,)), ((), ())),
                            preferred_element_type=F32)
        s_scr[...] = s
        c_scr[...] = jnp.max(s, axis=0, keepdims=True)

    def consume(j, s_scr, c_scr):
        m_prev = m_scr[...]
        m_new = jnp.maximum(m_prev, c_scr[...])
        alpha = jnp.exp2(m_prev - m_new)
        for nh in range(2 * tq // MXU_DIM):
            cols = slice(nh * MXU_DIM, (nh + 1) * MXU_DIM)
            m_cols = m_new[:, cols]
            pv = None
            lpart = jnp.zeros((SUBLANES, MXU_DIM), F32)
            for kb in range(ck // MXU_DIM):
                p = jnp.exp2(s_scr[kb * MXU_DIM:(kb + 1) * MXU_DIM, cols] - m_cols)
                lpart = lpart + p.reshape(MXU_DIM // SUBLANES, SUBLANES, MXU_DIM).sum(axis=0)
                blk, off = divmod(kb * MXU_DIM, tk)
                d = jnp.dot(vt_ref[j * ATTN_SUB + blk, :, off:off + MXU_DIM], p.astype(BF16),
                            preferred_element_type=F32)
                pv = d if pv is None else pv + d
            acc_scr[:, cols] = acc_scr[:, cols] * alpha[:, cols] + pv
            l_scr[:, cols] = alpha[:, cols] * l_scr[:, cols] + jnp.sum(lpart, axis=0, keepdims=True)
        m_scr[...] = m_new

    scores(0, s0_scr, c0_scr)

    def body(i, carry):
        j = 2 * i
        scores(j + 1, s1_scr, c1_scr)
        consume(j, s0_scr, c0_scr)
        scores(jnp.minimum(j + 2, n_chunks - 1), s0_scr, c0_scr)
        consume(j + 1, s1_scr, c1_scr)
        return carry

    lax.fori_loop(0, n_chunks // 2, body, 0)

    lq = lq_ref[...]
    lam = (jnp.exp(jnp.sum(lq[0:1] * lq[1:2], axis=-1, keepdims=True))
           - jnp.exp(jnp.sum(lq[2:3] * lq[3:4], axis=-1, keepdims=True)) + lambda_init)
    on = acc_scr[...] / l_scr[...]
    o = (on[:, :tq] - lam * on[:, tq:]).T
    o_ref[...] = (_rms(o, g_ref[...], SUBLN_EPS) * (1.0 - lambda_init)).astype(BF16)


def _attention(q, k, vt, lambda_qk, subln_g, lambda_init):
    n_heads, s, _ = q.shape
    n_kv = vt.shape[1]
    return pl.pallas_call(
        functools.partial(_attn_kernel, lambda_init=lambda_init),
        grid=(n_heads, s // ATTN_TQ),
        in_specs=[
            pl.BlockSpec(lambda_qk.shape, lambda h, i: (0, 0)),
            pl.BlockSpec((1, V_DIM), lambda h, i: (0, 0)),
            pl.BlockSpec((None, ATTN_TQ, V_DIM), lambda h, i: (h, i, 0)),
            pl.BlockSpec((None, s, V_DIM), lambda h, i: (h, 0, 0)),
            pl.BlockSpec((None, n_kv, V_DIM, ATTN_TK), lambda h, i: (h, 0, 0, 0)),
        ],
        out_specs=pl.BlockSpec((ATTN_TQ, V_DIM), lambda h, i: (i, h)),
        out_shape=jax.ShapeDtypeStruct((s, n_heads * V_DIM), BF16),
        scratch_shapes=[pltpu.VMEM((2 * ATTN_TQ, V_DIM), BF16),
                        pltpu.VMEM((ATTN_SUB * ATTN_TK, 2 * ATTN_TQ), F32),
                        pltpu.VMEM((ATTN_SUB * ATTN_TK, 2 * ATTN_TQ), F32),
                        pltpu.VMEM((1, 2 * ATTN_TQ), F32),
                        pltpu.VMEM((1, 2 * ATTN_TQ), F32),
                        pltpu.VMEM((V_DIM, 2 * ATTN_TQ), F32),
                        pltpu.VMEM((1, 2 * ATTN_TQ), F32),
                        pltpu.VMEM((1, 2 * ATTN_TQ), F32)],
        compiler_params=_cparams(("parallel", "parallel")),
        name="diff_attn",
    )(lambda_qk, subln_g, q, k, vt)


def _dft_tables():
    n = FFT_N
    jk = np.outer(np.arange(n), np.arange(n)) % n
    ang = 2.0 * np.pi * jk / n
    c = np.cos(ang) / math.sqrt(n)
    s = np.sin(ang) / math.sqrt(n)
    stage1 = np.concatenate([c, -s], axis=0)
    stage2 = np.block([[c, s], [-s, c]])
    chan = np.concatenate([c, s], axis=0)
    tw_ang = 2.0 * np.pi * np.outer(np.arange(n), np.arange(n)) / (n * n)
    return (jnp.asarray(stage1, F32), jnp.asarray(stage2, F32), jnp.asarray(chan, F32),
            jnp.asarray(np.cos(tw_ang), F32), jnp.asarray(np.sin(tw_ang), F32))


def _fft1_kernel(x_ref, f_ref, twc_ref, tws_ref, o_ref):
    n = FFT_N
    t = jnp.dot(f_ref[...], x_ref[...], preferred_element_type=F32,
                precision=lax.Precision.HIGHEST)
    width = x_ref.shape[1] // FFT_NB
    for b in range(FFT_NB):
        tr = t[:n, b * width:(b + 1) * width]
        ti = t[n:, b * width:(b + 1) * width]
        c = pltpu.repeat(twc_ref[b], width // LANES, axis=1)
        s = pltpu.repeat(tws_ref[b], width // LANES, axis=1)
        o_ref[0, :, b * width:(b + 1) * width] = tr * c + ti * s
        o_ref[1, :, b * width:(b + 1) * width] = ti * c - tr * s


def _fft2_kernel(t_ref, f_ref, ch_ref, o_ref):
    n = FFT_N
    width = t_ref.shape[3]
    for b in range(FFT_KB):
        tt = jnp.concatenate([t_ref[0, b], t_ref[1, b]], axis=0)
        z = jnp.dot(f_ref[...], tt, preferred_element_type=F32,
                    precision=lax.Precision.HIGHEST)
        for g in range(N_GROUPS):
            zz = jnp.concatenate([z[:n, g * GROUP_DIM:(g + 1) * GROUP_DIM],
                                  z[n:, g * GROUP_DIM:(g + 1) * GROUP_DIM]], axis=1)
            y = jnp.dot(zz, ch_ref[...], preferred_element_type=F32,
                        precision=lax.Precision.HIGHEST)
            o_ref[:, b * width + g * GROUP_DIM:b * width + (g + 1) * GROUP_DIM] = y.astype(o_ref.dtype)


def _fourier_mix(u, tables):
    s, width = u.shape
    n = FFT_N
    stage1, stage2, chan, twc, tws = tables
    twc_b = jnp.broadcast_to(twc[:, :, None], (n, n, LANES))
    tws_b = jnp.broadcast_to(tws[:, :, None], (n, n, LANES))
    t = pl.pallas_call(
        _fft1_kernel,
        grid=(n // FFT_NB,),
        in_specs=[
            pl.BlockSpec((n, FFT_NB * width), lambda j: (0, j)),
            pl.BlockSpec((2 * n, n), lambda j: (0, 0)),
            pl.BlockSpec((FFT_NB, n, LANES), lambda j: (j, 0, 0)),
            pl.BlockSpec((FFT_NB, n, LANES), lambda j: (j, 0, 0)),
        ],
        out_specs=pl.BlockSpec((2, n, FFT_NB * width), lambda j: (0, 0, j)),
        out_shape=jax.ShapeDtypeStruct((2, n, n * width), F32),
        compiler_params=_cparams(("parallel",)),
        name="fft_stage1",
    )(u.reshape(n, n * width), stage1, twc_b, tws_b)
    y = pl.pallas_call(
        _fft2_kernel,
        grid=(n // FFT_KB,),
        in_specs=[
            pl.BlockSpec((2, FFT_KB, n, width), lambda i: (0, i, 0, 0)),
            pl.BlockSpec((2 * n, 2 * n), lambda i: (0, 0)),
            pl.BlockSpec((2 * n, n), lambda i: (0, 0)),
        ],
        out_specs=pl.BlockSpec((n, FFT_KB * width), lambda i: (0, i)),
        out_shape=jax.ShapeDtypeStruct((n, n * width), BF16),
        compiler_params=_cparams(("parallel",)),
        name="fft_stage2",
    )(t.reshape(2, n, n, width), stage2, chan)
    return y.reshape(s, width)


def _mixout_kernel(x_ref, pg_ref, sh_ref, sc_ref, gt_ref, qg_ref, ao_ref, fy_ref,
                   gwa_ref, gwf_ref, gba_ref, gbf_ref, ap_ref, fp_ref, wo_ref,
                   o_ref, h_scr, acc_scr):
    j = pl.program_id(1)

    @pl.when(j == 0)
    def _():
        xn = _rms(x_ref[...], pg_ref[...], NORM_EPS)
        h_scr[...] = (xn * (1.0 + sc_ref[...]) + sh_ref[...]).astype(BF16)
        acc_scr[...] = jnp.zeros_like(acc_scr)

    h = h_scr[...]
    ga = jax.nn.sigmoid(jnp.dot(h, gwa_ref[...], preferred_element_type=F32) + gba_ref[...])
    gf = jax.nn.sigmoid(jnp.dot(h, gwf_ref[...], preferred_element_type=F32) + gbf_ref[...])
    ya = jnp.dot(ao_ref[...], ap_ref[...], preferred_element_type=F32)
    yf = jnp.dot(fy_ref[...], fp_ref[...], preferred_element_type=F32)
    y = (ga * ya + gf * yf).astype(BF16)
    acc_scr[...] += jnp.dot(y, wo_ref[...], preferred_element_type=F32)

    @pl.when(j == pl.num_programs(1) - 1)
    def _():
        yn = _rms(acc_scr[...], qg_ref[...], NORM_EPS)
        o_ref[...] = x_ref[...] + gt_ref[...] * yn


def _mix_out(x, pre_g, shift, scale, gate, post_g, ao, fy, gate_w, gate_b, attn_proj,
             fnet_proj, w_out):
    s, d = x.shape
    nc = d // MIX_TC
    vec = pl.BlockSpec((1, d), lambda i, j: (0, 0))
    return pl.pallas_call(
        _mixout_kernel,
        grid=(s // ROW_TILE, nc),
        in_specs=[
            pl.BlockSpec((ROW_TILE, d), lambda i, j: (i, 0)),
            vec, vec, vec, vec, vec,
            pl.BlockSpec((ROW_TILE, ATTN_WIDTH), lambda i, j: (i, 0)),
            pl.BlockSpec((ROW_TILE, FNET_WIDTH), lambda i, j: (i, 0)),
            pl.BlockSpec((d, MIX_TC), lambda i, j: (0, j)),
            pl.BlockSpec((d, MIX_TC), lambda i, j: (0, nc + j)),
            pl.BlockSpec((1, MIX_TC), lambda i, j: (0, j)),
            pl.BlockSpec((1, MIX_TC), lambda i, j: (0, nc + j)),
            pl.BlockSpec((ATTN_WIDTH, MIX_TC), lambda i, j: (0, j)),
            pl.BlockSpec((FNET_WIDTH, MIX_TC), lambda i, j: (0, j)),
            pl.BlockSpec((MIX_TC, d), lambda i, j: (j, 0)),
        ],
        out_specs=pl.BlockSpec((ROW_TILE, d), lambda i, j: (i, 0)),
        out_shape=jax.ShapeDtypeStruct((s, d), F32),
        scratch_shapes=[pltpu.VMEM((ROW_TILE, d), BF16), pltpu.VMEM((ROW_TILE, d), F32)],
        compiler_params=_cparams(("parallel", "arbitrary")),
        name="mix_out",
    )(x, pre_g, shift, scale, gate, post_g, ao, fy, gate_w, gate_w, gate_b, gate_b,
      attn_proj, fnet_proj, w_out)


def _rope_tables(seq):
    half = HEAD_DIM // 2
    pos = jnp.arange(seq, dtype=F32)
    inv_freq = ROPE_THETA ** (-jnp.arange(0, HEAD_DIM, 2, dtype=F32) / HEAD_DIM)
    ang = pos[:, None] * inv_freq[None, :]
    cos, sin = jnp.cos(ang), jnp.sin(ang)
    zero = jnp.zeros_like(sin)
    cos_t = jnp.concatenate([cos] * (LANES // half), axis=1)
    sin_n = jnp.concatenate([-sin, zero] * (LANES // HEAD_DIM), axis=1)
    sin_p = jnp.concatenate([zero, sin] * (LANES // HEAD_DIM), axis=1)
    return cos_t, sin_n, sin_p


def kernel(x, c, ada_w, ada_b, pre_norm_g, post_norm_g, ffn1_w_in, ffn1_w_out, mix_w_in,
           lambda_qk, subln_g, attn_proj, fnet_proj, branch_gate_w, branch_gate_b, mix_w_out,
           ffn2_w_in, ffn2_w_out):
    b, s, d = x.shape
    assert b == 1 and s == FFT_N * FFT_N and s % ROW_TILE == 0
    n_layers = ada_w.shape[0]
    cos_t, sin_n, sin_p = _rope_tables(s)
    dft = _dft_tables()
    mod = _ada_mod(c, ada_w, ada_b).reshape(n_layers, 3, 3, 1, d)
    q_scale = HEAD_DIM ** -0.5 * math.log2(math.e)
    xs = x.reshape(s, d)
    for l in range(n_layers):
        lambda_init = 0.8 - 0.6 * math.exp(-0.3 * l)
        pre = pre_norm_g[l].reshape(3, 1, d)
        post = post_norm_g[l].reshape(3, 1, d)
        xs = _ffn(xs, pre[0], mod[l, 0, 0], mod[l, 0, 1], mod[l, 0, 2], post[0],
                  ffn1_w_in[l].astype(BF16), ffn1_w_out[l].astype(BF16))
        q, k, vt, u = _mix_in(xs, pre[1], mod[l, 1, 0], mod[l, 1, 1], mix_w_in[l].astype(BF16),
                              cos_t, sin_n, sin_p, q_scale)
        ao = _attention(q, k, vt, lambda_qk[l], subln_g[l].reshape(1, V_DIM), lambda_init)
        fy = _fourier_mix(u, dft)
        xs = _mix_out(xs, pre[1], mod[l, 1, 0], mod[l, 1, 1], mod[l, 1, 2], post[1], ao, fy,
                      branch_gate_w[l].astype(BF16), branch_gate_b[l].reshape(1, 2 * d),
                      attn_proj[l].astype(BF16), fnet_proj[l].astype(BF16),
                      mix_w_out[l].astype(BF16))
        xs = _ffn(xs, pre[2], mod[l, 2, 0], mod[l, 2, 1], mod[l, 2, 2], post[2],
                  ffn2_w_in[l].astype(BF16), ffn2_w_out[l].astype(BF16))
    return xs.reshape(b, s, d)
```

```python
import functools
import math

import numpy as np
import jax
import jax.numpy as jnp
from jax import lax
from jax.experimental import pallas as pl
from jax.experimental.pallas import tpu as pltpu

F32 = jnp.float32
BF16 = jnp.bfloat16

N_HEADS = 8
HEAD_DIM = 64
V_DIM = 2 * HEAD_DIM
VT_ROWS = V_DIM + 16
K_LANES = 2 * V_DIM
ATTN_WIDTH = N_HEADS * V_DIM
N_GROUPS = 8
GROUP_DIM = 128
FNET_WIDTH = N_GROUPS * GROUP_DIM
ROPE_THETA = 10000.0
NORM_EPS = 1e-6
SUBLN_EPS = 1e-5
MACARON_WEIGHT = 0.5

LANES = 128
SUBLANES = 8
MXU_DIM = 256
VMEM_LIMIT_BYTES = 56 * 1024 * 1024

ADA_TN = 1024
ROW_TILE = 512
FFN_TF = 512
MIX_TC = 512
ATTN_TQ = 256
ATTN_TK = ROW_TILE
ATTN_SUB = 2
ATTN_TS = 16
SHIFT_LIMIT = 60.0
BOUND_MARGIN = 1.0 + 2.0 ** -7
FFT_N = 128
FFT_NB = 4
FFT_KB = 4


def _cparams(sem):
    return pltpu.CompilerParams(dimension_semantics=sem, vmem_limit_bytes=VMEM_LIMIT_BYTES)


def _rms(x, g, eps):
    ms = jnp.mean(x * x, axis=-1, keepdims=True)
    return x * lax.rsqrt(ms + eps) * g


def _ada_kernel(c_ref, w_ref, b_ref, o_ref):
    d, tn = w_ref.shape

    def body(k, acc):
        r = pl.multiple_of(k * SUBLANES, SUBLANES)
        cb = c_ref[pl.ds(r, SUBLANES), :]
        cb = cb * jax.nn.sigmoid(cb)
        return acc + w_ref[pl.ds(r, SUBLANES), :] * jnp.concatenate([cb] * (tn // LANES), axis=1)

    acc = lax.fori_loop(0, d // SUBLANES, body, jnp.zeros((SUBLANES, tn), F32), unroll=8)
    o_ref[...] = jnp.sum(acc, axis=0, keepdims=True) + b_ref[...]


def _ada_mod(c, ada_w, ada_b):
    n_layers, d, n = ada_w.shape
    c_b = jnp.broadcast_to(c.reshape(d, 1), (d, LANES))
    out = pl.pallas_call(
        _ada_kernel,
        grid=(n_layers, n // ADA_TN),
        in_specs=[
            pl.BlockSpec((d, LANES), lambda l, j: (0, 0)),
            pl.BlockSpec((None, d, ADA_TN), lambda l, j: (l, 0, j)),
            pl.BlockSpec((None, 1, ADA_TN), lambda l, j: (l, 0, j)),
        ],
        out_specs=pl.BlockSpec((None, 1, ADA_TN), lambda l, j: (l, 0, j)),
        out_shape=jax.ShapeDtypeStruct((n_layers, 1, n), F32),
        compiler_params=_cparams(("parallel", "parallel")),
        name="ada_mod",
    )(c_b, ada_w, ada_b.reshape(n_layers, 1, n))
    return out


def _ffn_kernel(x_ref, pg_ref, sh_ref, sc_ref, gt_ref, qg_ref, wg_ref, wu_ref, wo_ref,
                o_ref, h_scr, acc_scr):
    j = pl.program_id(1)

    @pl.when(j == 0)
    def _():
        xn = _rms(x_ref[...], pg_ref[...], NORM_EPS)
        h_scr[...] = (xn * (1.0 + sc_ref[...]) + sh_ref[...]).astype(BF16)
        acc_scr[...] = jnp.zeros_like(acc_scr)

    h = h_scr[...]
    g = jnp.dot(h, wg_ref[...], preferred_element_type=F32)
    u = jnp.dot(h, wu_ref[...], preferred_element_type=F32)
    a = (g * jax.nn.sigmoid(g) * u).astype(BF16)
    acc_scr[...] += jnp.dot(a, wo_ref[...], preferred_element_type=F32)

    @pl.when(j == pl.num_programs(1) - 1)
    def _():
        yn = _rms(acc_scr[...], qg_ref[...], NORM_EPS)
        o_ref[...] = x_ref[...] + (MACARON_WEIGHT * gt_ref[...]) * yn


def _ffn(x, pre_g, shift, scale, gate, post_g, w_in, w_out):
    s, d = x.shape
    f = w_out.shape[0]
    nf = f // FFN_TF
    vec = pl.BlockSpec((1, d), lambda i, j: (0, 0))
    return pl.pallas_call(
        _ffn_kernel,
        grid=(s // ROW_TILE, nf),
        in_specs=[
            pl.BlockSpec((ROW_TILE, d), lambda i, j: (i, 0)),
            vec, vec, vec, vec, vec,
            pl.BlockSpec((d, FFN_TF), lambda i, j: (0, j)),
            pl.BlockSpec((d, FFN_TF), lambda i, j: (0, nf + j)),
            pl.BlockSpec((FFN_TF, d), lambda i, j: (j, 0)),
        ],
        out_specs=pl.BlockSpec((ROW_TILE, d), lambda i, j: (i, 0)),
        out_shape=jax.ShapeDtypeStruct((s, d), F32),
        scratch_shapes=[pltpu.VMEM((ROW_TILE, d), BF16), pltpu.VMEM((ROW_TILE, d), F32)],
        compiler_params=_cparams(("parallel", "arbitrary")),
        name="ffn",
    )(x, pre_g, shift, scale, gate, post_g, w_in, w_in, w_out)


def _mixin_kernel(x_ref, pg_ref, sh_ref, sc_ref, w_ref, cos_ref, sinn_ref, sinp_ref,
                  q_ref, k_ref, kn_ref, vt_ref, u_ref, *, q_scale):
    xn = _rms(x_ref[...], pg_ref[...], NORM_EPS)
    h = (xn * (1.0 + sc_ref[...]) + sh_ref[...]).astype(BF16)
    cos_t, sin_n, sin_p = cos_ref[...], sinn_ref[...], sinp_ref[...]

    def rope(zs):
        return (zs * cos_t + pltpu.roll(zs, LANES - HEAD_DIM // 2, 1) * sin_n
                + pltpu.roll(zs, HEAD_DIM // 2, 1) * sin_p)

    zq = jnp.dot(h, w_ref[:, 0:ATTN_WIDTH], preferred_element_type=F32)
    for hd in range(N_HEADS):
        q_ref[hd] = (rope(zq[:, hd * V_DIM:(hd + 1) * V_DIM]) * q_scale).astype(BF16)
    zk = jnp.dot(h, w_ref[:, ATTN_WIDTH:2 * ATTN_WIDTH], preferred_element_type=F32)
    lane = lax.broadcasted_iota(jnp.int32, (zk.shape[0], LANES), 1)
    shift_lanes = jnp.where(lane == 0, 1.0, 0.0).astype(BF16)
    for hd in range(N_HEADS):
        kr = rope(zk[:, hd * V_DIM:(hd + 1) * V_DIM]).astype(BF16)
        k_ref[hd, :, :V_DIM] = kr
        k_ref[hd, :, V_DIM:] = shift_lanes
        k2 = kr.astype(F32) * kr.astype(F32)
        n1 = jnp.max(jnp.sum(jnp.where(lane < HEAD_DIM, k2, 0.0), axis=1, keepdims=True),
                     axis=0, keepdims=True)
        n2 = jnp.max(jnp.sum(jnp.where(lane < HEAD_DIM, 0.0, k2), axis=1, keepdims=True),
                     axis=0, keepdims=True)
        kn_ref[hd:hd + 1, :] = jnp.where(lane[:1] < HEAD_DIM, n1, n2)
    zv = jnp.dot(h, w_ref[:, 2 * ATTN_WIDTH:3 * ATTN_WIDTH], preferred_element_type=F32)
    zvt = zv.T
    for hd in range(N_HEADS):
        vt_ref[hd, :V_DIM, :] = zvt[hd * V_DIM:(hd + 1) * V_DIM, :].astype(BF16)
        vt_ref[hd, V_DIM:, :] = jnp.ones((VT_ROWS - V_DIM, zvt.shape[1]), BF16)
    u_ref[...] = jnp.dot(h, w_ref[:, 3 * ATTN_WIDTH:], preferred_element_type=F32)


def _mix_in(x, pre_g, shift, scale, w, cos_t, sin_n, sin_p, q_scale):
    s, d = x.shape
    n_tiles = s // ROW_TILE
    vec = pl.BlockSpec((1, d), lambda i: (0, 0))
    tab = pl.BlockSpec((ROW_TILE, LANES), lambda i: (i, 0))
    return pl.pallas_call(
        functools.partial(_mixin_kernel, q_scale=q_scale),
        grid=(n_tiles,),
        in_specs=[
            pl.BlockSpec((ROW_TILE, d), lambda i: (i, 0)),
            vec, vec, vec,
            pl.BlockSpec(w.shape, lambda i: (0, 0), pipeline_mode=pl.Buffered(1)),
            tab, tab, tab,
        ],
        out_specs=[
            pl.BlockSpec((N_HEADS, ROW_TILE, V_DIM), lambda i: (0, i, 0)),
            pl.BlockSpec((N_HEADS, ROW_TILE, K_LANES), lambda i: (0, i, 0)),
            pl.BlockSpec((None, N_HEADS, LANES), lambda i: (i, 0, 0)),
            pl.BlockSpec((N_HEADS, None, VT_ROWS, ROW_TILE), lambda i: (0, i, 0, 0)),
            pl.BlockSpec((ROW_TILE, FNET_WIDTH), lambda i: (i, 0)),
        ],
        out_shape=[
            jax.ShapeDtypeStruct((N_HEADS, s, V_DIM), BF16),
            jax.ShapeDtypeStruct((N_HEADS, s, K_LANES), BF16),
            jax.ShapeDtypeStruct((n_tiles, N_HEADS, LANES), F32),
            jax.ShapeDtypeStruct((N_HEADS, n_tiles, VT_ROWS, ROW_TILE), BF16),
            jax.ShapeDtypeStruct((s, FNET_WIDTH), F32),
        ],
        compiler_params=_cparams(("parallel",)),
        name="mix_in",
    )(x, pre_g, shift, scale, w, cos_t, sin_n, sin_p)


def _attn_kernel(lq_ref, g_ref, kn_ref, q_ref, k_ref, vt_ref, o_ref,
                 qz_scr, s0_scr, s1_scr, c0_scr, c1_scr, acc_scr, m_scr, *, lambda_init):
    tq = q_ref.shape[0]
    n_kv, _, tk = vt_ref.shape
    n_col = 2 * tq // MXU_DIM
    q = q_ref[...].astype(F32)
    lane = lax.broadcasted_iota(jnp.int32, q.shape, 1)
    qz = jnp.concatenate([jnp.where(lane < HEAD_DIM, q, 0.0),
                          jnp.where(lane < HEAD_DIM, 0.0, q)], axis=0)
    kn = jnp.max(kn_ref[...], axis=0, keepdims=True)
    row = lax.broadcasted_iota(jnp.int32, (2 * tq, 1), 0)
    kn_rows = jnp.where(row < tq, kn[:, :1], kn[:, HEAD_DIM:HEAD_DIM + 1])
    bound = jnp.sqrt(jnp.sum(qz * qz, axis=1, keepdims=True) * kn_rows) * BOUND_MARGIN
    fast = jnp.max(bound) <= SHIFT_LIMIT
    shift = jnp.where(fast, bound, 0.0)
    ext = jnp.where(lax.broadcasted_iota(jnp.int32, (2 * tq, LANES), 1) == 0, -shift, 0.0)
    qz_scr[...] = jnp.concatenate([qz, ext], axis=1).T.astype(BF16)
    acc_scr[...] = jnp.zeros_like(acc_scr)

    @pl.when(fast)
    def _():
        span = ATTN_TS * MXU_DIM
        last = n_kv * tk - MXU_DIM

        def shifted_scores(r):
            return jnp.dot(k_ref[pl.ds(pl.multiple_of(r, MXU_DIM), MXU_DIM), :], qz_scr[...],
                           preferred_element_type=F32)

        s0_scr[:MXU_DIM, :] = shifted_scores(0)

        def body(t, carry):
            r0 = t * span
            pv = [None] * n_col
            s_next = None
            for kb in range(ATTN_TS):
                s = s0_scr[:MXU_DIM, :] if kb == 0 else s_next
                s_next = shifted_scores(jnp.minimum(r0 + (kb + 1) * MXU_DIM, last))
                blk, off = divmod(kb * MXU_DIM, tk)
                for nh in range(n_col):
                    p = jnp.exp2(s[:, nh * MXU_DIM:(nh + 1) * MXU_DIM]).astype(BF16)
                    d = jnp.dot(vt_ref[t * (span // tk) + blk, :, off:off + MXU_DIM], p,
                                preferred_element_type=F32)
                    pv[nh] = d if pv[nh] is None else pv[nh] + d
            s0_scr[:MXU_DIM, :] = s_next
            for nh in range(n_col):
                cols = slice(nh * MXU_DIM, (nh + 1) * MXU_DIM)
                acc_scr[:, cols] = acc_scr[:, cols] + pv[nh]
            return carry

        lax.fori_loop(0, n_kv * tk // span, body, 0)

    @pl.when(jnp.logical_not(fast))
    def _():
        n_chunks = n_kv // ATTN_SUB
        ck = ATTN_SUB * tk
        m_scr[...] = jnp.full_like(m_scr, -jnp.inf)

        def scores(j, s_scr, c_scr):
            r = pl.multiple_of(j * ck, ck)
            s = jnp.dot(k_ref[pl.ds(r, ck), :], qz_scr[...],
                        preferred_element_type=F32)
            s_scr[...] = s
            c_scr[...] = jnp.max(s, axis=0, keepdims=True)

        def consume(j, s_scr, c_scr):
            m_prev = m_scr[...]
            m_new = jnp.maximum(m_prev, c_scr[...])
            alpha = jnp.exp2(m_prev - m_new)
            for nh in range(n_col):
                cols = slice(nh * MXU_DIM, (nh + 1) * MXU_DIM)
                pv = None
                for kb in range(ck // MXU_DIM):
                    p = jnp.exp2(s_scr[kb * MXU_DIM:(kb + 1) * MXU_DIM, cols] - m_new[:, cols])
                    blk, off = divmod(kb * MXU_DIM, tk)
                    d = jnp.dot(vt_ref[j * ATTN_SUB + blk, :, off:off + MXU_DIM], p.astype(BF16),
                                preferred_element_type=F32)
                    pv = d if pv is None else pv + d
                acc_scr[:, cols] = acc_scr[:, cols] * alpha[:, cols] + pv
            m_scr[...] = m_new

        scores(0, s0_scr, c0_scr)

        def body(i, carry):
            j = 2 * i
            scores(j + 1, s1_scr, c1_scr)
            consume(j, s0_scr, c0_scr)
            scores(jnp.minimum(j + 2, n_chunks - 1), s0_scr, c0_scr)
            consume(j + 1, s1_scr, c1_scr)
            return carry

        lax.fori_loop(0, n_chunks // 2, body, 0)

    lq = lq_ref[...]
    lam = (jnp.exp(jnp.sum(lq[0:1] * lq[1:2], axis=-1, keepdims=True))
           - jnp.exp(jnp.sum(lq[2:3] * lq[3:4], axis=-1, keepdims=True)) + lambda_init)
    on = acc_scr[:V_DIM, :] / acc_scr[V_DIM:V_DIM + 1, :]
    o = (on[:, :tq] - lam * on[:, tq:]).T
    o_ref[...] = (_rms(o, g_ref[...], SUBLN_EPS) * (1.0 - lambda_init)).astype(BF16)


def _attention(q, k, kn, vt, lambda_qk, subln_g, lambda_init):
    n_heads, s, _ = q.shape
    n_kv = vt.shape[1]
    return pl.pallas_call(
        functools.partial(_attn_kernel, lambda_init=lambda_init),
        grid=(n_heads, s // ATTN_TQ),
        in_specs=[
            pl.BlockSpec(lambda_qk.shape, lambda h, i: (0, 0)),
            pl.BlockSpec((1, V_DIM), lambda h, i: (0, 0)),
            pl.BlockSpec((kn.shape[0], LANES), lambda h, i: (0, h)),
            pl.BlockSpec((None, ATTN_TQ, V_DIM), lambda h, i: (h, i, 0)),
            pl.BlockSpec((None, s, K_LANES), lambda h, i: (h, 0, 0)),
            pl.BlockSpec((None, n_kv, VT_ROWS, ATTN_TK), lambda h, i: (h, 0, 0, 0)),
        ],
        out_specs=pl.BlockSpec((ATTN_TQ, V_DIM), lambda h, i: (i, h)),
        out_shape=jax.ShapeDtypeStruct((s, n_heads * V_DIM), BF16),
        scratch_shapes=[pltpu.VMEM((K_LANES, 2 * ATTN_TQ), BF16),
                        pltpu.VMEM((ATTN_SUB * ATTN_TK, 2 * ATTN_TQ), F32),
                        pltpu.VMEM((ATTN_SUB * ATTN_TK, 2 * ATTN_TQ), F32),
                        pltpu.VMEM((1, 2 * ATTN_TQ), F32),
                        pltpu.VMEM((1, 2 * ATTN_TQ), F32),
                        pltpu.VMEM((VT_ROWS, 2 * ATTN_TQ), F32),
                        pltpu.VMEM((1, 2 * ATTN_TQ), F32)],
        compiler_params=_cparams(("parallel", "parallel")),
        name="diff_attn",
    )(lambda_qk, subln_g, kn.reshape(kn.shape[0], n_heads * LANES), q, k, vt)


def _dft_tables():
    n = FFT_N
    jk = np.outer(np.arange(n), np.arange(n)) % n
    ang = 2.0 * np.pi * jk / n
    c = np.cos(ang) / math.sqrt(n)
    s = np.sin(ang) / math.sqrt(n)
    stage1 = np.concatenate([c, -s], axis=0)
    stage2 = np.block([[c, s], [-s, c]])
    chan = np.concatenate([c, s], axis=0)
    tw_ang = 2.0 * np.pi * np.outer(np.arange(n), np.arange(n)) / (n * n)
    return (jnp.asarray(stage1, F32), jnp.asarray(stage2, F32), jnp.asarray(chan, F32),
            jnp.asarray(np.cos(tw_ang), F32), jnp.asarray(np.sin(tw_ang), F32))


def _fft1_kernel(x_ref, f_ref, twc_ref, tws_ref, o_ref):
    n = FFT_N
    t = jnp.dot(f_ref[...], x_ref[...], preferred_element_type=F32,
                precision=lax.Precision.HIGHEST)
    width = x_ref.shape[1] // FFT_NB
    for b in range(FFT_NB):
        tr = t[:n, b * width:(b + 1) * width]
        ti = t[n:, b * width:(b + 1) * width]
        c = jnp.concatenate([twc_ref[b]] * (width // LANES), axis=1)
        s = jnp.concatenate([tws_ref[b]] * (width // LANES), axis=1)
        o_ref[0, :, b * width:(b + 1) * width] = tr * c + ti * s
        o_ref[1, :, b * width:(b + 1) * width] = ti * c - tr * s


def _fft2_kernel(t_ref, f_ref, ch_ref, o_ref):
    n = FFT_N
    width = t_ref.shape[3]
    for b in range(FFT_KB):
        tt = jnp.concatenate([t_ref[0, b], t_ref[1, b]], axis=0)
        z = jnp.dot(f_ref[...], tt, preferred_element_type=F32,
                    precision=lax.Precision.HIGHEST)
        for g in range(N_GROUPS):
            zz = jnp.concatenate([z[:n, g * GROUP_DIM:(g + 1) * GROUP_DIM],
                                  z[n:, g * GROUP_DIM:(g + 1) * GROUP_DIM]], axis=1)
            y = jnp.dot(zz, ch_ref[...], preferred_element_type=F32,
                        precision=lax.Precision.HIGHEST)
            o_ref[:, b * width + g * GROUP_DIM:b * width + (g + 1) * GROUP_DIM] = y.astype(o_ref.dtype)


def _fourier_mix(u, tables):
    s, width = u.shape
    n = FFT_N
    stage1, stage2, chan, twc, tws = tables
    twc_b = jnp.broadcast_to(twc[:, :, None], (n, n, LANES))
    tws_b = jnp.broadcast_to(tws[:, :, None], (n, n, LANES))
    t = pl.pallas_call(
        _fft1_kernel,
        grid=(n // FFT_NB,),
        in_specs=[
            pl.BlockSpec((n, FFT_NB * width), lambda j: (0, j)),
            pl.BlockSpec((2 * n, n), lambda j: (0, 0)),
            pl.BlockSpec((FFT_NB, n, LANES), lambda j: (j, 0, 0)),
            pl.BlockSpec((FFT_NB, n, LANES), lambda j: (j, 0, 0)),
        ],
        out_specs=pl.BlockSpec((2, n, FFT_NB * width), lambda j: (0, 0, j)),
        out_shape=jax.ShapeDtypeStruct((2, n, n * width), F32),
        compiler_params=_cparams(("parallel",)),
        name="fft_stage1",
    )(u.reshape(n, n * width), stage1, twc_b, tws_b)
    y = pl.pallas_call(
        _fft2_kernel,
        grid=(n // FFT_KB,),
        in_specs=[
            pl.BlockSpec((2, FFT_KB, n, width), lambda i: (0, i, 0, 0)),
            pl.BlockSpec((2 * n, 2 * n), lambda i: (0, 0)),
            pl.BlockSpec((2 * n, n), lambda i: (0, 0)),
        ],
        out_specs=pl.BlockSpec((n, FFT_KB * width), lambda i: (0, i)),
        out_shape=jax.ShapeDtypeStruct((n, n * width), BF16),
        compiler_params=_cparams(("parallel",)),
        name="fft_stage2",
    )(t.reshape(2, n, n, width), stage2, chan)
    return y.reshape(s, width)


def _mixout_kernel(x_ref, pg_ref, sh_ref, sc_ref, gt_ref, qg_ref, ao_ref, fy_ref,
                   gwa_ref, gwf_ref, gba_ref, gbf_ref, ap_ref, fp_ref, wo_ref,
                   o_ref, h_scr, acc_scr):
    j = pl.program_id(1)

    @pl.when(j == 0)
    def _():
        xn = _rms(x_ref[...], pg_ref[...], NORM_EPS)
        h_scr[...] = (xn * (1.0 + sc_ref[...]) + sh_ref[...]).astype(BF16)
        acc_scr[...] = jnp.zeros_like(acc_scr)

    h = h_scr[...]
    ga = jax.nn.sigmoid(jnp.dot(h, gwa_ref[...], preferred_element_type=F32) + gba_ref[...])
    gf = jax.nn.sigmoid(jnp.dot(h, gwf_ref[...], preferred_element_type=F32) + gbf_ref[...])
    ya = jnp.dot(ao_ref[...], ap_ref[...], preferred_element_type=F32)
    yf = jnp.dot(fy_ref[...], fp_ref[...], preferred_element_type=F32)
    y = (ga * ya + gf * yf).astype(BF16)
    acc_scr[...] += jnp.dot(y, wo_ref[...], preferred_element_type=F32)

    @pl.when(j == pl.num_programs(1) - 1)
    def _():
        yn = _rms(acc_scr[...], qg_ref[...], NORM_EPS)
        o_ref[...] = x_ref[...] + gt_ref[...] * yn


def _mix_out(x, pre_g, shift, scale, gate, post_g, ao, fy, gate_w, gate_b, attn_proj,
             fnet_proj, w_out):
    s, d = x.shape
    nc = d // MIX_TC
    vec = pl.BlockSpec((1, d), lambda i, j: (0, 0))
    return pl.pallas_call(
        _mixout_kernel,
        grid=(s // ROW_TILE, nc),
        in_specs=[
            pl.BlockSpec((ROW_TILE, d), lambda i, j: (i, 0)),
            vec, vec, vec, vec, vec,
            pl.BlockSpec((ROW_TILE, ATTN_WIDTH), lambda i, j: (i, 0)),
            pl.BlockSpec((ROW_TILE, FNET_WIDTH), lambda i, j: (i, 0)),
            pl.BlockSpec((d, MIX_TC), lambda i, j: (0, j)),
            pl.BlockSpec((d, MIX_TC), lambda i, j: (0, nc + j)),
            pl.BlockSpec((1, MIX_TC), lambda i, j: (0, j)),
            pl.BlockSpec((1, MIX_TC), lambda i, j: (0, nc + j)),
            pl.BlockSpec((ATTN_WIDTH, MIX_TC), lambda i, j: (0, j)),
            pl.BlockSpec((FNET_WIDTH, MIX_TC), lambda i, j: (0, j)),
            pl.BlockSpec((MIX_TC, d), lambda i, j: (j, 0)),
        ],
        out_specs=pl.BlockSpec((ROW_TILE, d), lambda i, j: (i, 0)),
        out_shape=jax.ShapeDtypeStruct((s, d), F32),
        scratch_shapes=[pltpu.VMEM((ROW_TILE, d), BF16), pltpu.VMEM((ROW_TILE, d), F32)],
        compiler_params=_cparams(("parallel", "arbitrary")),
        name="mix_out",
    )(x, pre_g, shift, scale, gate, post_g, ao, fy, gate_w, gate_w, gate_b, gate_b,
      attn_proj, fnet_proj, w_out)


def _rope_tables(seq):
    half = HEAD_DIM // 2
    pos = jnp.arange(seq, dtype=F32)
    inv_freq = ROPE_THETA ** (-jnp.arange(0, HEAD_DIM, 2, dtype=F32) / HEAD_DIM)
    ang = pos[:, None] * inv_freq[None, :]
    cos, sin = jnp.cos(ang), jnp.sin(ang)
    zero = jnp.zeros_like(sin)
    cos_t = jnp.concatenate([cos] * (LANES // half), axis=1)
    sin_n = jnp.concatenate([-sin, zero] * (LANES // HEAD_DIM), axis=1)
    sin_p = jnp.concatenate([zero, sin] * (LANES // HEAD_DIM), axis=1)
    return cos_t, sin_n, sin_p


def kernel(x, c, ada_w, ada_b, pre_norm_g, post_norm_g, ffn1_w_in, ffn1_w_out, mix_w_in,
           lambda_qk, subln_g, attn_proj, fnet_proj, branch_gate_w, branch_gate_b, mix_w_out,
           ffn2_w_in, ffn2_w_out):
    b, s, d = x.shape
    assert b == 1 and s == FFT_N * FFT_N and s % ROW_TILE == 0
    n_layers = ada_w.shape[0]
    cos_t, sin_n, sin_p = _rope_tables(s)
    dft = _dft_tables()
    mod = _ada_mod(c, ada_w, ada_b).reshape(n_layers, 3, 3, 1, d)
    q_scale = HEAD_DIM ** -0.5 * math.log2(math.e)
    xs = x.reshape(s, d)
    for l in range(n_layers):
        lambda_init = 0.8 - 0.6 * math.exp(-0.3 * l)
        pre = pre_norm_g[l].reshape(3, 1, d)
        post = post_norm_g[l].reshape(3, 1, d)
        xs = _ffn(xs, pre[0], mod[l, 0, 0], mod[l, 0, 1], mod[l, 0, 2], post[0],
                  ffn1_w_in[l].astype(BF16), ffn1_w_out[l].astype(BF16))
        q, k, kn, vt, u = _mix_in(xs, pre[1], mod[l, 1, 0], mod[l, 1, 1], mix_w_in[l].astype(BF16),
                              cos_t, sin_n, sin_p, q_scale)
        ao = _attention(q, k, kn, vt, lambda_qk[l], subln_g[l].reshape(1, V_DIM), lambda_init)
        fy = _fourier_mix(u, dft)
        xs = _mix_out(xs, pre[1], mod[l, 1, 0], mod[l, 1, 1], mod[l, 1, 2], post[1], ao, fy,
                      branch_gate_w[l].astype(BF16), branch_gate_b[l].reshape(1, 2 * d),
                      attn_proj[l].astype(BF16), fnet_proj[l].astype(BF16),
                      mix_w_out[l].astype(BF16))
        xs = _ffn(xs, pre[2], mod[l, 2, 0], mod[l, 2, 1], mod[l, 2, 2], post[2],
                  ffn2_w_in[l].astype(BF16), ffn2_w_out[l].astype(BF16))
    return xs.reshape(b, s, d)
```

```python
import functools
import math

import numpy as np
import jax
import jax.numpy as jnp
from jax import lax
from jax.experimental import pallas as pl
from jax.experimental.pallas import tpu as pltpu

F32 = jnp.float32
BF16 = jnp.bfloat16

N_HEADS = 8
HEAD_DIM = 64
V_DIM = 2 * HEAD_DIM
VT_ROWS = V_DIM + 16
K_LANES = 2 * V_DIM
ATTN_WIDTH = N_HEADS * V_DIM
N_GROUPS = 8
GROUP_DIM = 128
FNET_WIDTH = N_GROUPS * GROUP_DIM
ROPE_THETA = 10000.0
NORM_EPS = 1e-6
SUBLN_EPS = 1e-5
MACARON_WEIGHT = 0.5

LANES = 128
SUBLANES = 8
MXU_DIM = 256
VMEM_LIMIT_BYTES = 56 * 1024 * 1024

ADA_TN = 1024
ROW_TILE = 512
FFN_TF = 512
MIX_TC = 512
ATTN_TQ = 256
ATTN_TK = ROW_TILE
ATTN_SUB = 2
ATTN_QK = 512
ATTN_TS = 8
SHIFT_LIMIT = 60.0
BOUND_MARGIN = 1.0 + 2.0 ** -7
FFT_N = 128
FFT_NB = 4
FFT_KB = 4


def _cparams(sem):
    return pltpu.CompilerParams(dimension_semantics=sem, vmem_limit_bytes=VMEM_LIMIT_BYTES)


def _rms(x, g, eps):
    ms = jnp.mean(x * x, axis=-1, keepdims=True)
    return x * lax.rsqrt(ms + eps) * g


def _ada_kernel(c_ref, w_ref, b_ref, o_ref):
    d, tn = w_ref.shape

    def body(k, acc):
        r = pl.multiple_of(k * SUBLANES, SUBLANES)
        cb = c_ref[pl.ds(r, SUBLANES), :]
        cb = cb * jax.nn.sigmoid(cb)
        return acc + w_ref[pl.ds(r, SUBLANES), :] * jnp.concatenate([cb] * (tn // LANES), axis=1)

    acc = lax.fori_loop(0, d // SUBLANES, body, jnp.zeros((SUBLANES, tn), F32), unroll=8)
    o_ref[...] = jnp.sum(acc, axis=0, keepdims=True) + b_ref[...]


def _ada_mod(c, ada_w, ada_b):
    n_layers, d, n = ada_w.shape
    c_b = jnp.broadcast_to(c.reshape(d, 1), (d, LANES))
    out = pl.pallas_call(
        _ada_kernel,
        grid=(n_layers, n // ADA_TN),
        in_specs=[
            pl.BlockSpec((d, LANES), lambda l, j: (0, 0)),
            pl.BlockSpec((None, d, ADA_TN), lambda l, j: (l, 0, j)),
            pl.BlockSpec((None, 1, ADA_TN), lambda l, j: (l, 0, j)),
        ],
        out_specs=pl.BlockSpec((None, 1, ADA_TN), lambda l, j: (l, 0, j)),
        out_shape=jax.ShapeDtypeStruct((n_layers, 1, n), F32),
        compiler_params=_cparams(("parallel", "parallel")),
        name="ada_mod",
    )(c_b, ada_w, ada_b.reshape(n_layers, 1, n))
    return out


def _ffn_kernel(x_ref, pg_ref, sh_ref, sc_ref, gt_ref, qg_ref, wg_ref, wu_ref, wo_ref,
                o_ref, h_scr, acc_scr):
    j = pl.program_id(1)

    @pl.when(j == 0)
    def _():
        xn = _rms(x_ref[...], pg_ref[...], NORM_EPS)
        h_scr[...] = (xn * (1.0 + sc_ref[...]) + sh_ref[...]).astype(BF16)
        acc_scr[...] = jnp.zeros_like(acc_scr)

    h = h_scr[...]
    g = jnp.dot(h, wg_ref[...], preferred_element_type=F32)
    u = jnp.dot(h, wu_ref[...], preferred_element_type=F32)
    a = (g * jax.nn.sigmoid(g) * u).astype(BF16)
    acc_scr[...] += jnp.dot(a, wo_ref[...], preferred_element_type=F32)

    @pl.when(j == pl.num_programs(1) - 1)
    def _():
        yn = _rms(acc_scr[...], qg_ref[...], NORM_EPS)
        o_ref[...] = x_ref[...] + (MACARON_WEIGHT * gt_ref[...]) * yn


def _ffn(x, pre_g, shift, scale, gate, post_g, w_in, w_out, layer):
    s, d = x.shape
    f = w_out.shape[1]
    nf = f // FFN_TF
    vec = pl.BlockSpec((1, d), lambda i, j: (0, 0))
    return pl.pallas_call(
        _ffn_kernel,
        grid=(s // ROW_TILE, nf),
        in_specs=[
            pl.BlockSpec((ROW_TILE, d), lambda i, j: (i, 0)),
            vec, vec, vec, vec, vec,
            pl.BlockSpec((None, d, FFN_TF), lambda i, j: (layer, 0, j)),
            pl.BlockSpec((None, d, FFN_TF), lambda i, j: (layer, 0, nf + j)),
            pl.BlockSpec((None, FFN_TF, d), lambda i, j: (layer, j, 0)),
        ],
        out_specs=pl.BlockSpec((ROW_TILE, d), lambda i, j: (i, 0)),
        out_shape=jax.ShapeDtypeStruct((s, d), F32),
        scratch_shapes=[pltpu.VMEM((ROW_TILE, d), BF16), pltpu.VMEM((ROW_TILE, d), F32)],
        compiler_params=_cparams(("parallel", "arbitrary")),
        name="ffn",
    )(x, pre_g, shift, scale, gate, post_g, w_in, w_in, w_out)


def _mixin_kernel(x_ref, pg_ref, sh_ref, sc_ref, w_ref, cos_ref, sinn_ref, sinp_ref,
                  q_ref, k_ref, kn_ref, vt_ref, u_ref, *, q_scale):
    xn = _rms(x_ref[...], pg_ref[...], NORM_EPS)
    h = (xn * (1.0 + sc_ref[...]) + sh_ref[...]).astype(BF16)
    cos_t, sin_n, sin_p = cos_ref[...], sinn_ref[...], sinp_ref[...]

    def rope(zs):
        return (zs * cos_t + pltpu.roll(zs, LANES - HEAD_DIM // 2, 1) * sin_n
                + pltpu.roll(zs, HEAD_DIM // 2, 1) * sin_p)

    zq = jnp.dot(h, w_ref[:, 0:ATTN_WIDTH], preferred_element_type=F32)
    for hd in range(N_HEADS):
        q_ref[hd] = (rope(zq[:, hd * V_DIM:(hd + 1) * V_DIM]) * q_scale).astype(BF16)
    zk = jnp.dot(h, w_ref[:, ATTN_WIDTH:2 * ATTN_WIDTH], preferred_element_type=F32)
    lane = lax.broadcasted_iota(jnp.int32, (zk.shape[0], LANES), 1)
    shift_lanes = jnp.where(lane == 0, 1.0, 0.0).astype(BF16)
    for hd in range(N_HEADS):
        kr = rope(zk[:, hd * V_DIM:(hd + 1) * V_DIM]).astype(BF16)
        k_ref[hd, :, :V_DIM] = kr
        k_ref[hd, :, V_DIM:] = shift_lanes
        k2 = kr.astype(F32) * kr.astype(F32)
        n1 = jnp.max(jnp.sum(jnp.where(lane < HEAD_DIM, k2, 0.0), axis=1, keepdims=True),
                     axis=0, keepdims=True)
        n2 = jnp.max(jnp.sum(jnp.where(lane < HEAD_DIM, 0.0, k2), axis=1, keepdims=True),
                     axis=0, keepdims=True)
        kn_ref[hd:hd + 1, :] = jnp.where(lane[:1] < HEAD_DIM, n1, n2)
    zv = jnp.dot(h, w_ref[:, 2 * ATTN_WIDTH:3 * ATTN_WIDTH], preferred_element_type=F32)
    zvt = zv.T
    for hd in range(N_HEADS):
        vt_ref[hd, :V_DIM, :] = zvt[hd * V_DIM:(hd + 1) * V_DIM, :].astype(BF16)
        vt_ref[hd, V_DIM:, :] = jnp.ones((VT_ROWS - V_DIM, zvt.shape[1]), BF16)
    u_ref[...] = jnp.dot(h, w_ref[:, 3 * ATTN_WIDTH:], preferred_element_type=F32)


def _mix_in(x, pre_g, shift, scale, w, layer, cos_t, sin_n, sin_p, q_scale):
    s, d = x.shape
    n_tiles = s // ROW_TILE
    vec = pl.BlockSpec((1, d), lambda i: (0, 0))
    tab = pl.BlockSpec((ROW_TILE, LANES), lambda i: (i, 0))
    return pl.pallas_call(
        functools.partial(_mixin_kernel, q_scale=q_scale),
        grid=(n_tiles,),
        in_specs=[
            pl.BlockSpec((ROW_TILE, d), lambda i: (i, 0)),
            vec, vec, vec,
            pl.BlockSpec((None,) + w.shape[1:], lambda i: (layer, 0, 0), pipeline_mode=pl.Buffered(1)),
            tab, tab, tab,
        ],
        out_specs=[
            pl.BlockSpec((N_HEADS, ROW_TILE, V_DIM), lambda i: (0, i, 0)),
            pl.BlockSpec((N_HEADS, ROW_TILE, K_LANES), lambda i: (0, i, 0)),
            pl.BlockSpec((None, N_HEADS, LANES), lambda i: (i, 0, 0)),
            pl.BlockSpec((N_HEADS, None, VT_ROWS, ROW_TILE), lambda i: (0, i, 0, 0)),
            pl.BlockSpec((ROW_TILE, FNET_WIDTH), lambda i: (i, 0)),
        ],
        out_shape=[
            jax.ShapeDtypeStruct((N_HEADS, s, V_DIM), BF16),
            jax.ShapeDtypeStruct((N_HEADS, s, K_LANES), BF16),
            jax.ShapeDtypeStruct((n_tiles, N_HEADS, LANES), F32),
            jax.ShapeDtypeStruct((N_HEADS, n_tiles, VT_ROWS, ROW_TILE), BF16),
            jax.ShapeDtypeStruct((s, FNET_WIDTH), F32),
        ],
        compiler_params=_cparams(("parallel",)),
        name="mix_in",
    )(x, pre_g, shift, scale, w, cos_t, sin_n, sin_p)


def _attn_kernel(lq_ref, g_ref, kn_ref, q_ref, k_ref, vt_ref, o_ref,
                 qz_scr, s0_scr, s1_scr, c0_scr, c1_scr, acc_scr, m_scr, *, lambda_init):
    tq = q_ref.shape[0]
    n_kv, _, tk = vt_ref.shape
    n_col = 2 * tq // MXU_DIM
    q = q_ref[...].astype(F32)
    lane = lax.broadcasted_iota(jnp.int32, q.shape, 1)
    qz = jnp.concatenate([jnp.where(lane < HEAD_DIM, q, 0.0),
                          jnp.where(lane < HEAD_DIM, 0.0, q)], axis=0)
    kn = jnp.max(kn_ref[...], axis=0, keepdims=True)
    row = lax.broadcasted_iota(jnp.int32, (2 * tq, 1), 0)
    kn_rows = jnp.where(row < tq, kn[:, :1], kn[:, HEAD_DIM:HEAD_DIM + 1])
    bound = jnp.sqrt(jnp.sum(qz * qz, axis=1, keepdims=True) * kn_rows) * BOUND_MARGIN
    fast = jnp.max(bound) <= SHIFT_LIMIT
    shift = jnp.where(fast, bound, 0.0)
    ext = jnp.where(lax.broadcasted_iota(jnp.int32, (2 * tq, LANES), 1) == 0, -shift, 0.0)
    qz_scr[...] = jnp.concatenate([qz, ext], axis=1).T.astype(BF16)
    acc_scr[...] = jnp.zeros_like(acc_scr)

    @pl.when(fast)
    def _():
        span = ATTN_TS * ATTN_QK
        last = n_kv * tk - ATTN_QK

        def shifted_scores(r):
            return jnp.dot(k_ref[pl.ds(pl.multiple_of(r, ATTN_QK), ATTN_QK), :], qz_scr[...],
                           preferred_element_type=F32)

        s0_scr[:ATTN_QK, :] = shifted_scores(0)

        def body(t, carry):
            r0 = t * span
            pv = [None] * n_col
            s_next = None
            for kb in range(ATTN_TS):
                s = s0_scr[:ATTN_QK, :] if kb == 0 else s_next
                s_next = shifted_scores(jnp.minimum(r0 + (kb + 1) * ATTN_QK, last))
                for sub in range(ATTN_QK // MXU_DIM):
                    blk, off = divmod(kb * ATTN_QK + sub * MXU_DIM, tk)
                    rows = slice(sub * MXU_DIM, (sub + 1) * MXU_DIM)
                    for nh in range(n_col):
                        p = jnp.exp2(s[rows, nh * MXU_DIM:(nh + 1) * MXU_DIM]).astype(BF16)
                        d = jnp.dot(vt_ref[t * (span // tk) + blk, :, off:off + MXU_DIM], p,
                                    preferred_element_type=F32)
                        pv[nh] = d if pv[nh] is None else pv[nh] + d
            s0_scr[:ATTN_QK, :] = s_next
            for nh in range(n_col):
                cols = slice(nh * MXU_DIM, (nh + 1) * MXU_DIM)
                acc_scr[:, cols] = acc_scr[:, cols] + pv[nh]
            return carry

        lax.fori_loop(0, n_kv * tk // span, body, 0)

    @pl.when(jnp.logical_not(fast))
    def _():
        n_chunks = n_kv // ATTN_SUB
        ck = ATTN_SUB * tk
        m_scr[...] = jnp.full_like(m_scr, -jnp.inf)

        def scores(j, s_scr, c_scr):
            r = pl.multiple_of(j * ck, ck)
            s = jnp.dot(k_ref[pl.ds(r, ck), :], qz_scr[...],
                        preferred_element_type=F32)
            s_scr[...] = s
            c_scr[...] = jnp.max(s, axis=0, keepdims=True)

        def consume(j, s_scr, c_scr):
            m_prev = m_scr[...]
            m_new = jnp.maximum(m_prev, c_scr[...])
            alpha = jnp.exp2(m_prev - m_new)
            for nh in range(n_col):
                cols = slice(nh * MXU_DIM, (nh + 1) * MXU_DIM)
                pv = None
                for kb in range(ck // MXU_DIM):
                    p = jnp.exp2(s_scr[kb * MXU_DIM:(kb + 1) * MXU_DIM, cols] - m_new[:, cols])
                    blk, off = divmod(kb * MXU_DIM, tk)
                    d = jnp.dot(vt_ref[j * ATTN_SUB + blk, :, off:off + MXU_DIM], p.astype(BF16),
                                preferred_element_type=F32)
                    pv = d if pv is None else pv + d
                acc_scr[:, cols] = acc_scr[:, cols] * alpha[:, cols] + pv
            m_scr[...] = m_new

        scores(0, s0_scr, c0_scr)

        def body(i, carry):
            j = 2 * i
            scores(j + 1, s1_scr, c1_scr)
            consume(j, s0_scr, c0_scr)
            scores(jnp.minimum(j + 2, n_chunks - 1), s0_scr, c0_scr)
            consume(j + 1, s1_scr, c1_scr)
            return carry

        lax.fori_loop(0, n_chunks // 2, body, 0)

    lq = lq_ref[...]
    lam = (jnp.exp(jnp.sum(lq[0:1] * lq[1:2], axis=-1, keepdims=True))
           - jnp.exp(jnp.sum(lq[2:3] * lq[3:4], axis=-1, keepdims=True)) + lambda_init)
    on = acc_scr[:V_DIM, :] / acc_scr[V_DIM:V_DIM + 1, :]
    o = (on[:, :tq] - lam * on[:, tq:]).T
    o_ref[...] = (_rms(o, g_ref[...], SUBLN_EPS) * (1.0 - lambda_init)).astype(BF16)


def _attention(q, k, kn, vt, lambda_qk, subln_g, lambda_init):
    n_heads, s, _ = q.shape
    n_kv = vt.shape[1]
    return pl.pallas_call(
        functools.partial(_attn_kernel, lambda_init=lambda_init),
        grid=(n_heads, s // ATTN_TQ),
        in_specs=[
            pl.BlockSpec(lambda_qk.shape, lambda h, i: (0, 0)),
            pl.BlockSpec((1, V_DIM), lambda h, i: (0, 0)),
            pl.BlockSpec((kn.shape[0], LANES), lambda h, i: (0, h)),
            pl.BlockSpec((None, ATTN_TQ, V_DIM), lambda h, i: (h, i, 0)),
            pl.BlockSpec((None, s, K_LANES), lambda h, i: (h, 0, 0)),
            pl.BlockSpec((None, n_kv, VT_ROWS, ATTN_TK), lambda h, i: (h, 0, 0, 0)),
        ],
        out_specs=pl.BlockSpec((ATTN_TQ, V_DIM), lambda h, i: (i, h)),
        out_shape=jax.ShapeDtypeStruct((s, n_heads * V_DIM), BF16),
        scratch_shapes=[pltpu.VMEM((K_LANES, 2 * ATTN_TQ), BF16),
                        pltpu.VMEM((ATTN_SUB * ATTN_TK, 2 * ATTN_TQ), F32),
                        pltpu.VMEM((ATTN_SUB * ATTN_TK, 2 * ATTN_TQ), F32),
                        pltpu.VMEM((1, 2 * ATTN_TQ), F32),
                        pltpu.VMEM((1, 2 * ATTN_TQ), F32),
                        pltpu.VMEM((VT_ROWS, 2 * ATTN_TQ), F32),
                        pltpu.VMEM((1, 2 * ATTN_TQ), F32)],
        compiler_params=_cparams(("parallel", "parallel")),
        name="diff_attn",
    )(lambda_qk, subln_g, kn.reshape(kn.shape[0], n_heads * LANES), q, k, vt)


def _dft_tables():
    n = FFT_N
    jk = np.outer(np.arange(n), np.arange(n)) % n
    ang = 2.0 * np.pi * jk / n
    c = np.cos(ang) / math.sqrt(n)
    s = np.sin(ang) / math.sqrt(n)
    stage1 = np.concatenate([c, -s], axis=0)
    stage2 = np.block([[c, s], [-s, c]])
    chan = np.concatenate([c, s], axis=0)
    tw_ang = 2.0 * np.pi * np.outer(np.arange(n), np.arange(n)) / (n * n)
    return (jnp.asarray(stage1, F32), jnp.asarray(stage2, F32), jnp.asarray(chan, F32),
            jnp.asarray(np.cos(tw_ang), F32), jnp.asarray(np.sin(tw_ang), F32))


def _fft1_kernel(x_ref, f_ref, twc_ref, tws_ref, o_ref):
    n = FFT_N
    t = jnp.dot(f_ref[...], x_ref[...], preferred_element_type=F32,
                precision=lax.Precision.HIGHEST)
    width = x_ref.shape[1] // FFT_NB
    for b in range(FFT_NB):
        tr = t[:n, b * width:(b + 1) * width]
        ti = t[n:, b * width:(b + 1) * width]
        c = jnp.concatenate([twc_ref[b]] * (width // LANES), axis=1)
        s = jnp.concatenate([tws_ref[b]] * (width // LANES), axis=1)
        o_ref[0, :, b * width:(b + 1) * width] = tr * c + ti * s
        o_ref[1, :, b * width:(b + 1) * width] = ti * c - tr * s


def _fft2_kernel(t_ref, f_ref, ch_ref, o_ref):
    n = FFT_N
    width = t_ref.shape[3]
    for b in range(FFT_KB):
        tt = jnp.concatenate([t_ref[0, b], t_ref[1, b]], axis=0)
        z = jnp.dot(f_ref[...], tt, preferred_element_type=F32,
                    precision=lax.Precision.HIGHEST)
        for g in range(N_GROUPS):
            zz = jnp.concatenate([z[:n, g * GROUP_DIM:(g + 1) * GROUP_DIM],
                                  z[n:, g * GROUP_DIM:(g + 1) * GROUP_DIM]], axis=1)
            y = jnp.dot(zz, ch_ref[...], preferred_element_type=F32,
                        precision=lax.Precision.HIGHEST)
            o_ref[:, b * width + g * GROUP_DIM:b * width + (g + 1) * GROUP_DIM] = y.astype(o_ref.dtype)


def _fourier_mix(u, tables):
    s, width = u.shape
    n = FFT_N
    stage1, stage2, chan, twc, tws = tables
    twc_b = jnp.broadcast_to(twc[:, :, None], (n, n, LANES))
    tws_b = jnp.broadcast_to(tws[:, :, None], (n, n, LANES))
    t = pl.pallas_call(
        _fft1_kernel,
        grid=(n // FFT_NB,),
        in_specs=[
            pl.BlockSpec((n, FFT_NB * width), lambda j: (0, j)),
            pl.BlockSpec((2 * n, n), lambda j: (0, 0)),
            pl.BlockSpec((FFT_NB, n, LANES), lambda j: (j, 0, 0)),
            pl.BlockSpec((FFT_NB, n, LANES), lambda j: (j, 0, 0)),
        ],
        out_specs=pl.BlockSpec((2, n, FFT_NB * width), lambda j: (0, 0, j)),
        out_shape=jax.ShapeDtypeStruct((2, n, n * width), F32),
        compiler_params=_cparams(("parallel",)),
        name="fft_stage1",
    )(u.reshape(n, n * width), stage1, twc_b, tws_b)
    y = pl.pallas_call(
        _fft2_kernel,
        grid=(n // FFT_KB,),
        in_specs=[
            pl.BlockSpec((2, FFT_KB, n, width), lambda i: (0, i, 0, 0)),
            pl.BlockSpec((2 * n, 2 * n), lambda i: (0, 0)),
            pl.BlockSpec((2 * n, n), lambda i: (0, 0)),
        ],
        out_specs=pl.BlockSpec((n, FFT_KB * width), lambda i: (0, i)),
        out_shape=jax.ShapeDtypeStruct((n, n * width), BF16),
        compiler_params=_cparams(("parallel",)),
        name="fft_stage2",
    )(t.reshape(2, n, n, width), stage2, chan)
    return y.reshape(s, width)


def _mixout_kernel(x_ref, pg_ref, sh_ref, sc_ref, gt_ref, qg_ref, ao_ref, fy_ref,
                   gwa_ref, gwf_ref, gba_ref, gbf_ref, ap_ref, fp_ref, wo_ref,
                   o_ref, h_scr, acc_scr):
    j = pl.program_id(1)

    @pl.when(j == 0)
    def _():
        xn = _rms(x_ref[...], pg_ref[...], NORM_EPS)
        h_scr[...] = (xn * (1.0 + sc_ref[...]) + sh_ref[...]).astype(BF16)
        acc_scr[...] = jnp.zeros_like(acc_scr)

    h = h_scr[...]
    ga = jax.nn.sigmoid(jnp.dot(h, gwa_ref[...], preferred_element_type=F32) + gba_ref[...])
    gf = jax.nn.sigmoid(jnp.dot(h, gwf_ref[...], preferred_element_type=F32) + gbf_ref[...])
    ya = jnp.dot(ao_ref[...], ap_ref[...], preferred_element_type=F32)
    yf = jnp.dot(fy_ref[...], fp_ref[...], preferred_element_type=F32)
    y = (ga * ya + gf * yf).astype(BF16)
    acc_scr[...] += jnp.dot(y, wo_ref[...], preferred_element_type=F32)

    @pl.when(j == pl.num_programs(1) - 1)
    def _():
        yn = _rms(acc_scr[...], qg_ref[...], NORM_EPS)
        o_ref[...] = x_ref[...] + gt_ref[...] * yn


def _mix_out(x, pre_g, shift, scale, gate, post_g, ao, fy, gate_w, gate_b, attn_proj,
             fnet_proj, w_out, layer):
    s, d = x.shape
    nc = d // MIX_TC
    vec = pl.BlockSpec((1, d), lambda i, j: (0, 0))
    return pl.pallas_call(
        _mixout_kernel,
        grid=(s // ROW_TILE, nc),
        in_specs=[
            pl.BlockSpec((ROW_TILE, d), lambda i, j: (i, 0)),
            vec, vec, vec, vec, vec,
            pl.BlockSpec((ROW_TILE, ATTN_WIDTH), lambda i, j: (i, 0)),
            pl.BlockSpec((ROW_TILE, FNET_WIDTH), lambda i, j: (i, 0)),
            pl.BlockSpec((None, d, MIX_TC), lambda i, j: (layer, 0, j)),
            pl.BlockSpec((None, d, MIX_TC), lambda i, j: (layer, 0, nc + j)),
            pl.BlockSpec((1, MIX_TC), lambda i, j: (0, j)),
            pl.BlockSpec((1, MIX_TC), lambda i, j: (0, nc + j)),
            pl.BlockSpec((None, ATTN_WIDTH, MIX_TC), lambda i, j: (layer, 0, j)),
            pl.BlockSpec((None, FNET_WIDTH, MIX_TC), lambda i, j: (layer, 0, j)),
            pl.BlockSpec((None, MIX_TC, d), lambda i, j: (layer, j, 0)),
        ],
        out_specs=pl.BlockSpec((ROW_TILE, d), lambda i, j: (i, 0)),
        out_shape=jax.ShapeDtypeStruct((s, d), F32),
        scratch_shapes=[pltpu.VMEM((ROW_TILE, d), BF16), pltpu.VMEM((ROW_TILE, d), F32)],
        compiler_params=_cparams(("parallel", "arbitrary")),
        name="mix_out",
    )(x, pre_g, shift, scale, gate, post_g, ao, fy, gate_w, gate_w, gate_b, gate_b,
      attn_proj, fnet_proj, w_out)


def _rope_tables(seq):
    half = HEAD_DIM // 2
    pos = jnp.arange(seq, dtype=F32)
    inv_freq = ROPE_THETA ** (-jnp.arange(0, HEAD_DIM, 2, dtype=F32) / HEAD_DIM)
    ang = pos[:, None] * inv_freq[None, :]
    cos, sin = jnp.cos(ang), jnp.sin(ang)
    zero = jnp.zeros_like(sin)
    cos_t = jnp.concatenate([cos] * (LANES // half), axis=1)
    sin_n = jnp.concatenate([-sin, zero] * (LANES // HEAD_DIM), axis=1)
    sin_p = jnp.concatenate([zero, sin] * (LANES // HEAD_DIM), axis=1)
    return cos_t, sin_n, sin_p


def kernel(x, c, ada_w, ada_b, pre_norm_g, post_norm_g, ffn1_w_in, ffn1_w_out, mix_w_in,
           lambda_qk, subln_g, attn_proj, fnet_proj, branch_gate_w, branch_gate_b, mix_w_out,
           ffn2_w_in, ffn2_w_out):
    b, s, d = x.shape
    assert b == 1 and s == FFT_N * FFT_N and s % ROW_TILE == 0
    n_layers = ada_w.shape[0]
    cos_t, sin_n, sin_p = _rope_tables(s)
    dft = _dft_tables()
    mod = _ada_mod(c, ada_w, ada_b).reshape(n_layers, 3, 3, 1, d)
    q_scale = HEAD_DIM ** -0.5 * math.log2(math.e)
    xs = x.reshape(s, d)
    ffn1_in, ffn1_out, ffn2_in, ffn2_out, mix_in_w, gate_w, attn_w, fnet_w, mix_out_w = (
        w.astype(BF16) for w in (ffn1_w_in, ffn1_w_out, ffn2_w_in, ffn2_w_out, mix_w_in,
                                 branch_gate_w, attn_proj, fnet_proj, mix_w_out))
    for l in range(n_layers):
        lambda_init = 0.8 - 0.6 * math.exp(-0.3 * l)
        pre = pre_norm_g[l].reshape(3, 1, d)
        post = post_norm_g[l].reshape(3, 1, d)
        xs = _ffn(xs, pre[0], mod[l, 0, 0], mod[l, 0, 1], mod[l, 0, 2], post[0],
                  ffn1_in, ffn1_out, l)
        q, k, kn, vt, u = _mix_in(xs, pre[1], mod[l, 1, 0], mod[l, 1, 1], mix_in_w, l,
                              cos_t, sin_n, sin_p, q_scale)
        ao = _attention(q, k, kn, vt, lambda_qk[l], subln_g[l].reshape(1, V_DIM), lambda_init)
        fy = _fourier_mix(u, dft)
        xs = _mix_out(xs, pre[1], mod[l, 1, 0], mod[l, 1, 1], mod[l, 1, 2], post[1], ao, fy,
                      gate_w, branch_gate_b[l].reshape(1, 2 * d), attn_w, fnet_w, mix_out_w, l)
        xs = _ffn(xs, pre[2], mod[l, 2, 0], mod[l, 2, 1], mod[l, 2, 2], post[2],
                  ffn2_in, ffn2_out, l)
    return xs.reshape(b, s, d)
```

```python
import functools
import math

import numpy as np
import jax
import jax.numpy as jnp
from jax import lax
from jax.experimental import pallas as pl
from jax.experimental.pallas import tpu as pltpu

F32 = jnp.float32
BF16 = jnp.bfloat16

N_HEADS = 8
HEAD_DIM = 64
V_DIM = 2 * HEAD_DIM
VT_ROWS = V_DIM + 16
K_LANES = 2 * V_DIM
ATTN_WIDTH = N_HEADS * V_DIM
N_GROUPS = 8
GROUP_DIM = 128
FNET_WIDTH = N_GROUPS * GROUP_DIM
ROPE_THETA = 10000.0
NORM_EPS = 1e-6
SUBLN_EPS = 1e-5
MACARON_WEIGHT = 0.5

LANES = 128
SUBLANES = 8
MXU_DIM = 256
VMEM_LIMIT_BYTES = 56 * 1024 * 1024

ADA_TN = 1024
ROW_TILE = 512
FFN_TF = 512
MIX_TC = 512
ATTN_TQ = 256
ATTN_TK = ROW_TILE
ATTN_SUB = 2
ATTN_QK = 512
ATTN_TS = 16
SHIFT_LIMIT = 60.0
BOUND_MARGIN = 1.0 + 2.0 ** -7
FFT_N = 128
FFT_NB = 4
FFT_KB = 4


def _cparams(sem):
    return pltpu.CompilerParams(dimension_semantics=sem, vmem_limit_bytes=VMEM_LIMIT_BYTES)


def _rms(x, g, eps):
    ms = jnp.mean(x * x, axis=-1, keepdims=True)
    return x * lax.rsqrt(ms + eps) * g


def _ada_kernel(c_ref, w_ref, b_ref, o_ref):
    d, tn = w_ref.shape

    def body(k, acc):
        r = pl.multiple_of(k * SUBLANES, SUBLANES)
        cb = c_ref[pl.ds(r, SUBLANES), :]
        cb = cb * jax.nn.sigmoid(cb)
        return acc + w_ref[pl.ds(r, SUBLANES), :] * jnp.concatenate([cb] * (tn // LANES), axis=1)

    acc = lax.fori_loop(0, d // SUBLANES, body, jnp.zeros((SUBLANES, tn), F32), unroll=8)
    o_ref[...] = jnp.sum(acc, axis=0, keepdims=True) + b_ref[...]


def _ada_mod(c, ada_w, ada_b):
    n_layers, d, n = ada_w.shape
    c_b = jnp.broadcast_to(c.reshape(d, 1), (d, LANES))
    out = pl.pallas_call(
        _ada_kernel,
        grid=(n_layers, n // ADA_TN),
        in_specs=[
            pl.BlockSpec((d, LANES), lambda l, j: (0, 0)),
            pl.BlockSpec((None, d, ADA_TN), lambda l, j: (l, 0, j)),
            pl.BlockSpec((None, 1, ADA_TN), lambda l, j: (l, 0, j)),
        ],
        out_specs=pl.BlockSpec((None, 1, ADA_TN), lambda l, j: (l, 0, j)),
        out_shape=jax.ShapeDtypeStruct((n_layers, 1, n), F32),
        compiler_params=_cparams(("parallel", "parallel")),
        name="ada_mod",
    )(c_b, ada_w, ada_b.reshape(n_layers, 1, n))
    return out


def _ffn_kernel(x_ref, pg_ref, sh_ref, sc_ref, gt_ref, qg_ref, wg_ref, wu_ref, wo_ref,
                o_ref, h_scr, acc_scr):
    j = pl.program_id(1)

    @pl.when(j == 0)
    def _():
        xn = _rms(x_ref[...], pg_ref[...], NORM_EPS)
        h_scr[...] = (xn * (1.0 + sc_ref[...]) + sh_ref[...]).astype(BF16)
        acc_scr[...] = jnp.zeros_like(acc_scr)

    h = h_scr[...]
    g = jnp.dot(h, wg_ref[...], preferred_element_type=F32)
    u = jnp.dot(h, wu_ref[...], preferred_element_type=F32)
    a = (g * jax.nn.sigmoid(g) * u).astype(BF16)
    acc_scr[...] += jnp.dot(a, wo_ref[...], preferred_element_type=F32)

    @pl.when(j == pl.num_programs(1) - 1)
    def _():
        yn = _rms(acc_scr[...], qg_ref[...], NORM_EPS)
        o_ref[...] = x_ref[...] + (MACARON_WEIGHT * gt_ref[...]) * yn


def _ffn(x, pre_g, shift, scale, gate, post_g, w_in, w_out, layer):
    s, d = x.shape
    f = w_out.shape[1]
    nf = f // FFN_TF
    vec = pl.BlockSpec((1, d), lambda i, j: (0, 0))
    return pl.pallas_call(
        _ffn_kernel,
        grid=(s // ROW_TILE, nf),
        in_specs=[
            pl.BlockSpec((ROW_TILE, d), lambda i, j: (i, 0)),
            vec, vec, vec, vec, vec,
            pl.BlockSpec((None, d, FFN_TF), lambda i, j: (layer, 0, j)),
            pl.BlockSpec((None, d, FFN_TF), lambda i, j: (layer, 0, nf + j)),
            pl.BlockSpec((None, FFN_TF, d), lambda i, j: (layer, j, 0)),
        ],
        out_specs=pl.BlockSpec((ROW_TILE, d), lambda i, j: (i, 0)),
        out_shape=jax.ShapeDtypeStruct((s, d), F32),
        scratch_shapes=[pltpu.VMEM((ROW_TILE, d), BF16), pltpu.VMEM((ROW_TILE, d), F32)],
        compiler_params=_cparams(("parallel", "arbitrary")),
        name="ffn",
    )(x, pre_g, shift, scale, gate, post_g, w_in, w_in, w_out)


def _mixin_kernel(x_ref, pg_ref, sh_ref, sc_ref, w_ref, cos_ref, sinn_ref, sinp_ref,
                  q_ref, k_ref, kn_ref, vt_ref, u_ref, *, q_scale):
    xn = _rms(x_ref[...], pg_ref[...], NORM_EPS)
    h = (xn * (1.0 + sc_ref[...]) + sh_ref[...]).astype(BF16)
    cos_t, sin_n, sin_p = cos_ref[...], sinn_ref[...], sinp_ref[...]

    def rope(zs):
        return (zs * cos_t + pltpu.roll(zs, LANES - HEAD_DIM // 2, 1) * sin_n
                + pltpu.roll(zs, HEAD_DIM // 2, 1) * sin_p)

    zq = jnp.dot(h, w_ref[:, 0:ATTN_WIDTH], preferred_element_type=F32)
    for hd in range(N_HEADS):
        q_ref[hd] = (rope(zq[:, hd * V_DIM:(hd + 1) * V_DIM]) * q_scale).astype(BF16)
    zk = jnp.dot(h, w_ref[:, ATTN_WIDTH:2 * ATTN_WIDTH], preferred_element_type=F32)
    lane = lax.broadcasted_iota(jnp.int32, (zk.shape[0], LANES), 1)
    shift_lanes = jnp.where(lane == 0, 1.0, 0.0).astype(BF16)
    for hd in range(N_HEADS):
        kr = rope(zk[:, hd * V_DIM:(hd + 1) * V_DIM]).astype(BF16)
        k_ref[hd, :, :V_DIM] = kr
        k_ref[hd, :, V_DIM:] = shift_lanes
        k2 = kr.astype(F32) * kr.astype(F32)
        n1 = jnp.max(jnp.sum(jnp.where(lane < HEAD_DIM, k2, 0.0), axis=1, keepdims=True),
                     axis=0, keepdims=True)
        n2 = jnp.max(jnp.sum(jnp.where(lane < HEAD_DIM, 0.0, k2), axis=1, keepdims=True),
                     axis=0, keepdims=True)
        kn_ref[hd:hd + 1, :] = jnp.where(lane[:1] < HEAD_DIM, n1, n2)
    zv = jnp.dot(h, w_ref[:, 2 * ATTN_WIDTH:3 * ATTN_WIDTH], preferred_element_type=F32)
    zvt = zv.T
    for hd in range(N_HEADS):
        vt_ref[hd, :V_DIM, :] = zvt[hd * V_DIM:(hd + 1) * V_DIM, :].astype(BF16)
        vt_ref[hd, V_DIM:, :] = jnp.ones((VT_ROWS - V_DIM, zvt.shape[1]), BF16)
    u_ref[...] = jnp.dot(h, w_ref[:, 3 * ATTN_WIDTH:], preferred_element_type=F32)


def _mix_in(x, pre_g, shift, scale, w, layer, cos_t, sin_n, sin_p, q_scale):
    s, d = x.shape
    n_tiles = s // ROW_TILE
    vec = pl.BlockSpec((1, d), lambda i: (0, 0))
    tab = pl.BlockSpec((ROW_TILE, LANES), lambda i: (i, 0))
    return pl.pallas_call(
        functools.partial(_mixin_kernel, q_scale=q_scale),
        grid=(n_tiles,),
        in_specs=[
            pl.BlockSpec((ROW_TILE, d), lambda i: (i, 0)),
            vec, vec, vec,
            pl.BlockSpec((None,) + w.shape[1:], lambda i: (layer, 0, 0), pipeline_mode=pl.Buffered(1)),
            tab, tab, tab,
        ],
        out_specs=[
            pl.BlockSpec((N_HEADS, ROW_TILE, V_DIM), lambda i: (0, i, 0)),
            pl.BlockSpec((N_HEADS, ROW_TILE, K_LANES), lambda i: (0, i, 0)),
            pl.BlockSpec((None, N_HEADS, LANES), lambda i: (i, 0, 0)),
            pl.BlockSpec((N_HEADS, None, VT_ROWS, ROW_TILE), lambda i: (0, i, 0, 0)),
            pl.BlockSpec((ROW_TILE, FNET_WIDTH), lambda i: (i, 0)),
        ],
        out_shape=[
            jax.ShapeDtypeStruct((N_HEADS, s, V_DIM), BF16),
            jax.ShapeDtypeStruct((N_HEADS, s, K_LANES), BF16),
            jax.ShapeDtypeStruct((n_tiles, N_HEADS, LANES), F32),
            jax.ShapeDtypeStruct((N_HEADS, n_tiles, VT_ROWS, ROW_TILE), BF16),
            jax.ShapeDtypeStruct((s, FNET_WIDTH), F32),
        ],
        compiler_params=_cparams(("parallel",)),
        name="mix_in",
    )(x, pre_g, shift, scale, w, cos_t, sin_n, sin_p)


def _attn_kernel(lq_ref, g_ref, kn_ref, q_ref, k_ref, vt_ref, o_ref,
                 qz_scr, s0_scr, s1_scr, c0_scr, c1_scr, acc_scr, m_scr, *, lambda_init):
    tq = q_ref.shape[0]
    n_kv, _, tk = vt_ref.shape
    n_col = 2 * tq // MXU_DIM
    q = q_ref[...].astype(F32)
    lane = lax.broadcasted_iota(jnp.int32, q.shape, 1)
    qz = jnp.concatenate([jnp.where(lane < HEAD_DIM, q, 0.0),
                          jnp.where(lane < HEAD_DIM, 0.0, q)], axis=0)
    kn = jnp.max(kn_ref[...], axis=0, keepdims=True)
    row = lax.broadcasted_iota(jnp.int32, (2 * tq, 1), 0)
    kn_rows = jnp.where(row < tq, kn[:, :1], kn[:, HEAD_DIM:HEAD_DIM + 1])
    bound = jnp.sqrt(jnp.sum(qz * qz, axis=1, keepdims=True) * kn_rows) * BOUND_MARGIN
    fast = jnp.max(bound) <= SHIFT_LIMIT
    shift = jnp.where(fast, bound, 0.0)
    ext = jnp.where(lax.broadcasted_iota(jnp.int32, (2 * tq, LANES), 1) == 0, -shift, 0.0)
    qz_scr[...] = jnp.concatenate([qz, ext], axis=1).T.astype(BF16)
    acc_scr[...] = jnp.zeros_like(acc_scr)

    @pl.when(fast)
    def _():
        span = ATTN_TS * ATTN_QK
        last = n_kv * tk - ATTN_QK

        def shifted_scores(r):
            return jnp.dot(k_ref[pl.ds(pl.multiple_of(r, ATTN_QK), ATTN_QK), :], qz_scr[...],
                           preferred_element_type=F32)

        s0_scr[:ATTN_QK, :] = shifted_scores(0)

        def body(t, carry):
            r0 = t * span
            pv = [None] * n_col
            s_next = None
            for kb in range(ATTN_TS):
                s = s0_scr[:ATTN_QK, :] if kb == 0 else s_next
                s_next = shifted_scores(jnp.minimum(r0 + (kb + 1) * ATTN_QK, last))
                for sub in range(ATTN_QK // MXU_DIM):
                    blk, off = divmod(kb * ATTN_QK + sub * MXU_DIM, tk)
                    rows = slice(sub * MXU_DIM, (sub + 1) * MXU_DIM)
                    for nh in range(n_col):
                        p = jnp.exp2(s[rows, nh * MXU_DIM:(nh + 1) * MXU_DIM]).astype(BF16)
                        d = jnp.dot(vt_ref[t * (span // tk) + blk, :, off:off + MXU_DIM], p,
                                    preferred_element_type=F32)
                        pv[nh] = d if pv[nh] is None else pv[nh] + d
            s0_scr[:ATTN_QK, :] = s_next
            for nh in range(n_col):
                cols = slice(nh * MXU_DIM, (nh + 1) * MXU_DIM)
                acc_scr[:, cols] = acc_scr[:, cols] + pv[nh]
            return carry

        lax.fori_loop(0, n_kv * tk // span, body, 0)

    @pl.when(jnp.logical_not(fast))
    def _():
        n_chunks = n_kv // ATTN_SUB
        ck = ATTN_SUB * tk
        m_scr[...] = jnp.full_like(m_scr, -jnp.inf)

        def scores(j, s_scr, c_scr):
            r = pl.multiple_of(j * ck, ck)
            s = jnp.dot(k_ref[pl.ds(r, ck), :], qz_scr[...],
                        preferred_element_type=F32)
            s_scr[...] = s
            c_scr[...] = jnp.max(s, axis=0, keepdims=True)

        def consume(j, s_scr, c_scr):
            m_prev = m_scr[...]
            m_new = jnp.maximum(m_prev, c_scr[...])
            alpha = jnp.exp2(m_prev - m_new)
            for nh in range(n_col):
                cols = slice(nh * MXU_DIM, (nh + 1) * MXU_DIM)
                pv = None
                for kb in range(ck // MXU_DIM):
                    p = jnp.exp2(s_scr[kb * MXU_DIM:(kb + 1) * MXU_DIM, cols] - m_new[:, cols])
                    blk, off = divmod(kb * MXU_DIM, tk)
                    d = jnp.dot(vt_ref[j * ATTN_SUB + blk, :, off:off + MXU_DIM], p.astype(BF16),
                                preferred_element_type=F32)
                    pv = d if pv is None else pv + d
                acc_scr[:, cols] = acc_scr[:, cols] * alpha[:, cols] + pv
            m_scr[...] = m_new

        scores(0, s0_scr, c0_scr)

        def body(i, carry):
            j = 2 * i
            scores(j + 1, s1_scr, c1_scr)
            consume(j, s0_scr, c0_scr)
            scores(jnp.minimum(j + 2, n_chunks - 1), s0_scr, c0_scr)
            consume(j + 1, s1_scr, c1_scr)
            return carry

        lax.fori_loop(0, n_chunks // 2, body, 0)

    lq = lq_ref[...]
    lam = (jnp.exp(jnp.sum(lq[0:1] * lq[1:2], axis=-1, keepdims=True))
           - jnp.exp(jnp.sum(lq[2:3] * lq[3:4], axis=-1, keepdims=True)) + lambda_init)
    on = acc_scr[:V_DIM, :] / acc_scr[V_DIM:V_DIM + 1, :]
    o = (on[:, :tq] - lam * on[:, tq:]).T
    o_ref[...] = (_rms(o, g_ref[...], SUBLN_EPS) * (1.0 - lambda_init)).astype(BF16)


def _attention(q, k, kn, vt, lambda_qk, subln_g, lambda_init):
    n_heads, s, _ = q.shape
    n_kv = vt.shape[1]
    return pl.pallas_call(
        functools.partial(_attn_kernel, lambda_init=lambda_init),
        grid=(n_heads, s // ATTN_TQ),
        in_specs=[
            pl.BlockSpec(lambda_qk.shape, lambda h, i: (0, 0)),
            pl.BlockSpec((1, V_DIM), lambda h, i: (0, 0)),
            pl.BlockSpec((kn.shape[0], LANES), lambda h, i: (0, h)),
            pl.BlockSpec((None, ATTN_TQ, V_DIM), lambda h, i: (h, i, 0)),
            pl.BlockSpec((None, s, K_LANES), lambda h, i: (h, 0, 0)),
            pl.BlockSpec((None, n_kv, VT_ROWS, ATTN_TK), lambda h, i: (h, 0, 0, 0)),
        ],
        out_specs=pl.BlockSpec((ATTN_TQ, V_DIM), lambda h, i: (i, h)),
        out_shape=jax.ShapeDtypeStruct((s, n_heads * V_DIM), BF16),
        scratch_shapes=[pltpu.VMEM((K_LANES, 2 * ATTN_TQ), BF16),
                        pltpu.VMEM((ATTN_SUB * ATTN_TK, 2 * ATTN_TQ), F32),
                        pltpu.VMEM((ATTN_SUB * ATTN_TK, 2 * ATTN_TQ), F32),
                        pltpu.VMEM((1, 2 * ATTN_TQ), F32),
                        pltpu.VMEM((1, 2 * ATTN_TQ), F32),
                        pltpu.VMEM((VT_ROWS, 2 * ATTN_TQ), F32),
                        pltpu.VMEM((1, 2 * ATTN_TQ), F32)],
        compiler_params=_cparams(("parallel", "parallel")),
        name="diff_attn",
    )(lambda_qk, subln_g, kn.reshape(kn.shape[0], n_heads * LANES), q, k, vt)


def _dft_tables():
    n = FFT_N
    jk = np.outer(np.arange(n), np.arange(n)) % n
    ang = 2.0 * np.pi * jk / n
    c = np.cos(ang) / math.sqrt(n)
    s = np.sin(ang) / math.sqrt(n)
    stage1 = np.concatenate([c, -s], axis=0)
    stage2 = np.block([[c, s], [-s, c]])
    chan = np.concatenate([c, s], axis=0)
    tw_ang = 2.0 * np.pi * np.outer(np.arange(n), np.arange(n)) / (n * n)
    return (_split(jnp.asarray(stage1, F32)), _split(jnp.asarray(stage2, F32)),
            _split(jnp.asarray(chan, F32)),
            jnp.asarray(np.cos(tw_ang), F32), jnp.asarray(np.sin(tw_ang), F32))


def _split(a):
    hi = a.astype(BF16)
    return hi, (a - hi.astype(F32)).astype(BF16)


def _dot3(a, b):
    (a_hi, a_lo), (b_hi, b_lo) = a, b
    return (jnp.dot(a_hi, b_hi, preferred_element_type=F32)
            + jnp.dot(a_hi, b_lo, preferred_element_type=F32)
            + jnp.dot(a_lo, b_hi, preferred_element_type=F32))


def _fft1_kernel(x_ref, fh_ref, fl_ref, twc_ref, tws_ref, o_ref):
    n = FFT_N
    t = _dot3((fh_ref[...], fl_ref[...]), _split(x_ref[...]))
    width = x_ref.shape[1] // FFT_NB
    for b in range(FFT_NB):
        tr = t[:n, b * width:(b + 1) * width]
        ti = t[n:, b * width:(b + 1) * width]
        c = jnp.concatenate([twc_ref[b]] * (width // LANES), axis=1)
        s = jnp.concatenate([tws_ref[b]] * (width // LANES), axis=1)
        o_ref[0, :, b * width:(b + 1) * width] = tr * c + ti * s
        o_ref[1, :, b * width:(b + 1) * width] = ti * c - tr * s


def _fft2_kernel(t_ref, fh_ref, fl_ref, chh_ref, chl_ref, o_ref):
    n = FFT_N
    width = t_ref.shape[3]
    f = (fh_ref[...], fl_ref[...])
    ch = (chh_ref[...], chl_ref[...])
    for b in range(FFT_KB):
        tt = jnp.concatenate([t_ref[0, b], t_ref[1, b]], axis=0)
        z = _dot3(f, _split(tt))
        for g in range(N_GROUPS):
            zz = jnp.concatenate([z[:n, g * GROUP_DIM:(g + 1) * GROUP_DIM],
                                  z[n:, g * GROUP_DIM:(g + 1) * GROUP_DIM]], axis=1)
            y = _dot3(_split(zz), ch)
            o_ref[:, b * width + g * GROUP_DIM:b * width + (g + 1) * GROUP_DIM] = y.astype(o_ref.dtype)


def _fourier_mix(u, tables):
    s, width = u.shape
    n = FFT_N
    stage1, stage2, chan, twc, tws = tables
    twc_b = jnp.broadcast_to(twc[:, :, None], (n, n, LANES))
    tws_b = jnp.broadcast_to(tws[:, :, None], (n, n, LANES))
    t = pl.pallas_call(
        _fft1_kernel,
        grid=(n // FFT_NB,),
        in_specs=[
            pl.BlockSpec((n, FFT_NB * width), lambda j: (0, j)),
            pl.BlockSpec((2 * n, n), lambda j: (0, 0)),
            pl.BlockSpec((2 * n, n), lambda j: (0, 0)),
            pl.BlockSpec((FFT_NB, n, LANES), lambda j: (j, 0, 0)),
            pl.BlockSpec((FFT_NB, n, LANES), lambda j: (j, 0, 0)),
        ],
        out_specs=pl.BlockSpec((2, n, FFT_NB * width), lambda j: (0, 0, j)),
        out_shape=jax.ShapeDtypeStruct((2, n, n * width), F32),
        compiler_params=_cparams(("parallel",)),
        name="fft_stage1",
    )(u.reshape(n, n * width), *stage1, twc_b, tws_b)
    y = pl.pallas_call(
        _fft2_kernel,
        grid=(n // FFT_KB,),
        in_specs=[
            pl.BlockSpec((2, FFT_KB, n, width), lambda i: (0, i, 0, 0)),
            pl.BlockSpec((2 * n, 2 * n), lambda i: (0, 0)),
            pl.BlockSpec((2 * n, 2 * n), lambda i: (0, 0)),
            pl.BlockSpec((2 * n, n), lambda i: (0, 0)),
            pl.BlockSpec((2 * n, n), lambda i: (0, 0)),
        ],
        out_specs=pl.BlockSpec((n, FFT_KB * width), lambda i: (0, i)),
        out_shape=jax.ShapeDtypeStruct((n, n * width), BF16),
        compiler_params=_cparams(("parallel",)),
        name="fft_stage2",
    )(t.reshape(2, n, n, width), *stage2, *chan)
    return y.reshape(s, width)


def _mixout_kernel(x_ref, pg_ref, sh_ref, sc_ref, gt_ref, qg_ref, ao_ref, fy_ref,
                   gwa_ref, gwf_ref, gba_ref, gbf_ref, ap_ref, fp_ref, wo_ref,
                   o_ref, h_scr, acc_scr):
    j = pl.program_id(1)

    @pl.when(j == 0)
    def _():
        xn = _rms(x_ref[...], pg_ref[...], NORM_EPS)
        h_scr[...] = (xn * (1.0 + sc_ref[...]) + sh_ref[...]).astype(BF16)
        acc_scr[...] = jnp.zeros_like(acc_scr)

    h = h_scr[...]
    ga = jax.nn.sigmoid(jnp.dot(h, gwa_ref[...], preferred_element_type=F32) + gba_ref[...])
    gf = jax.nn.sigmoid(jnp.dot(h, gwf_ref[...], preferred_element_type=F32) + gbf_ref[...])
    ya = jnp.dot(ao_ref[...], ap_ref[...], preferred_element_type=F32)
    yf = jnp.dot(fy_ref[...], fp_ref[...], preferred_element_type=F32)
    y = (ga * ya + gf * yf).astype(BF16)
    acc_scr[...] += jnp.dot(y, wo_ref[...], preferred_element_type=F32)

    @pl.when(j == pl.num_programs(1) - 1)
    def _():
        yn = _rms(acc_scr[...], qg_ref[...], NORM_EPS)
        o_ref[...] = x_ref[...] + gt_ref[...] * yn


def _mix_out(x, pre_g, shift, scale, gate, post_g, ao, fy, gate_w, gate_b, attn_proj,
             fnet_proj, w_out, layer):
    s, d = x.shape
    nc = d // MIX_TC
    vec = pl.BlockSpec((1, d), lambda i, j: (0, 0))
    return pl.pallas_call(
        _mixout_kernel,
        grid=(s // ROW_TILE, nc),
        in_specs=[
            pl.BlockSpec((ROW_TILE, d), lambda i, j: (i, 0)),
            vec, vec, vec, vec, vec,
            pl.BlockSpec((ROW_TILE, ATTN_WIDTH), lambda i, j: (i, 0)),
            pl.BlockSpec((ROW_TILE, FNET_WIDTH), lambda i, j: (i, 0)),
            pl.BlockSpec((None, d, MIX_TC), lambda i, j: (layer, 0, j)),
            pl.BlockSpec((None, d, MIX_TC), lambda i, j: (layer, 0, nc + j)),
            pl.BlockSpec((1, MIX_TC), lambda i, j: (0, j)),
            pl.BlockSpec((1, MIX_TC), lambda i, j: (0, nc + j)),
            pl.BlockSpec((None, ATTN_WIDTH, MIX_TC), lambda i, j: (layer, 0, j)),
            pl.BlockSpec((None, FNET_WIDTH, MIX_TC), lambda i, j: (layer, 0, j)),
            pl.BlockSpec((None, MIX_TC, d), lambda i, j: (layer, j, 0)),
        ],
        out_specs=pl.BlockSpec((ROW_TILE, d), lambda i, j: (i, 0)),
        out_shape=jax.ShapeDtypeStruct((s, d), F32),
        scratch_shapes=[pltpu.VMEM((ROW_TILE, d), BF16), pltpu.VMEM((ROW_TILE, d), F32)],
        compiler_params=_cparams(("parallel", "arbitrary")),
        name="mix_out",
    )(x, pre_g, shift, scale, gate, post_g, ao, fy, gate_w, gate_w, gate_b, gate_b,
      attn_proj, fnet_proj, w_out)


def _rope_tables(seq):
    half = HEAD_DIM // 2
    pos = jnp.arange(seq, dtype=F32)
    inv_freq = ROPE_THETA ** (-jnp.arange(0, HEAD_DIM, 2, dtype=F32) / HEAD_DIM)
    ang = pos[:, None] * inv_freq[None, :]
    cos, sin = jnp.cos(ang), jnp.sin(ang)
    zero = jnp.zeros_like(sin)
    cos_t = jnp.concatenate([cos] * (LANES // half), axis=1)
    sin_n = jnp.concatenate([-sin, zero] * (LANES // HEAD_DIM), axis=1)
    sin_p = jnp.concatenate([zero, sin] * (LANES // HEAD_DIM), axis=1)
    return cos_t, sin_n, sin_p


def kernel(x, c, ada_w, ada_b, pre_norm_g, post_norm_g, ffn1_w_in, ffn1_w_out, mix_w_in,
           lambda_qk, subln_g, attn_proj, fnet_proj, branch_gate_w, branch_gate_b, mix_w_out,
           ffn2_w_in, ffn2_w_out):
    b, s, d = x.shape
    assert b == 1 and s == FFT_N * FFT_N and s % ROW_TILE == 0
    n_layers = ada_w.shape[0]
    cos_t, sin_n, sin_p = _rope_tables(s)
    dft = _dft_tables()
    mod = _ada_mod(c, ada_w, ada_b).reshape(n_layers, 3, 3, 1, d)
    q_scale = HEAD_DIM ** -0.5 * math.log2(math.e)
    xs = x.reshape(s, d)
    ffn1_in, ffn1_out, ffn2_in, ffn2_out, mix_in_w, gate_w, attn_w, fnet_w, mix_out_w = (
        w.astype(BF16) for w in (ffn1_w_in, ffn1_w_out, ffn2_w_in, ffn2_w_out, mix_w_in,
                                 branch_gate_w, attn_proj, fnet_proj, mix_w_out))
    for l in range(n_layers):
        lambda_init = 0.8 - 0.6 * math.exp(-0.3 * l)
        pre = pre_norm_g[l].reshape(3, 1, d)
        post = post_norm_g[l].reshape(3, 1, d)
        xs = _ffn(xs, pre[0], mod[l, 0, 0], mod[l, 0, 1], mod[l, 0, 2], post[0],
                  ffn1_in, ffn1_out, l)
        q, k, kn, vt, u = _mix_in(xs, pre[1], mod[l, 1, 0], mod[l, 1, 1], mix_in_w, l,
                              cos_t, sin_n, sin_p, q_scale)
        ao = _attention(q, k, kn, vt, lambda_qk[l], subln_g[l].reshape(1, V_DIM), lambda_init)
        fy = _fourier_mix(u, dft)
        xs = _mix_out(xs, pre[1], mod[l, 1, 0], mod[l, 1, 1], mod[l, 1, 2], post[1], ao, fy,
                      gate_w, branch_gate_b[l].reshape(1, 2 * d), attn_w, fnet_w, mix_out_w, l)
        xs = _ffn(xs, pre[2], mod[l, 2, 0], mod[l, 2, 1], mod[l, 2, 2], post[2],
                  ffn2_in, ffn2_out, l)
    return xs.reshape(b, s, d)
```

```python
import functools
import math

import numpy as np
import jax
import jax.numpy as jnp
from jax import lax
from jax.experimental import pallas as pl
from jax.experimental.pallas import tpu as pltpu

F32 = jnp.float32
BF16 = jnp.bfloat16

N_HEADS = 8
HEAD_DIM = 64
V_DIM = 2 * HEAD_DIM
VT_ROWS = V_DIM + 16
K_LANES = 2 * V_DIM
ATTN_WIDTH = N_HEADS * V_DIM
N_GROUPS = 8
GROUP_DIM = 128
FNET_WIDTH = N_GROUPS * GROUP_DIM
ROPE_THETA = 10000.0
NORM_EPS = 1e-6
SUBLN_EPS = 1e-5
MACARON_WEIGHT = 0.5

LANES = 128
SUBLANES = 8
MXU_DIM = 256
VMEM_LIMIT_BYTES = 56 * 1024 * 1024

ADA_TN = 1024
ROW_TILE = 512
FFN_TF = 512
MIX_TC = 512
ATTN_TQ = 256
ATTN_TK = ROW_TILE
ATTN_SUB = 2
ATTN_QK = 512
ATTN_TS = 16
SHIFT_LIMIT = 60.0
BOUND_MARGIN = 1.0 + 2.0 ** -7
FFT_N = 128
FFT_NB = 4
FFT_KB = 4


def _cparams(sem):
    return pltpu.CompilerParams(dimension_semantics=sem, vmem_limit_bytes=VMEM_LIMIT_BYTES)


def _rms(x, g, eps):
    ms = jnp.mean(x * x, axis=-1, keepdims=True)
    return x * lax.rsqrt(ms + eps) * g


def _ada_kernel(c_ref, w_ref, b_ref, o_ref):
    d, tn = w_ref.shape

    def body(k, acc):
        r = pl.multiple_of(k * SUBLANES, SUBLANES)
        cb = c_ref[pl.ds(r, SUBLANES), :]
        cb = cb * jax.nn.sigmoid(cb)
        return acc + w_ref[pl.ds(r, SUBLANES), :] * jnp.concatenate([cb] * (tn // LANES), axis=1)

    acc = lax.fori_loop(0, d // SUBLANES, body, jnp.zeros((SUBLANES, tn), F32), unroll=8)
    o_ref[...] = jnp.sum(acc, axis=0, keepdims=True) + b_ref[...]


def _ada_mod(c, ada_w, ada_b):
    n_layers, d, n = ada_w.shape
    c_b = jnp.broadcast_to(c.reshape(d, 1), (d, LANES))
    out = pl.pallas_call(
        _ada_kernel,
        grid=(n_layers, n // ADA_TN),
        in_specs=[
            pl.BlockSpec((d, LANES), lambda l, j: (0, 0)),
            pl.BlockSpec((None, d, ADA_TN), lambda l, j: (l, 0, j)),
            pl.BlockSpec((None, 1, ADA_TN), lambda l, j: (l, 0, j)),
        ],
        out_specs=pl.BlockSpec((None, 1, ADA_TN), lambda l, j: (l, 0, j)),
        out_shape=jax.ShapeDtypeStruct((n_layers, 1, n), F32),
        compiler_params=_cparams(("parallel", "parallel")),
        name="ada_mod",
    )(c_b, ada_w, ada_b.reshape(n_layers, 1, n))
    return out


def _ffn_kernel(x_ref, pg_ref, sh_ref, sc_ref, gt_ref, qg_ref, wg_ref, wu_ref, wo_ref,
                o_ref, h_scr, acc_scr):
    j = pl.program_id(1)

    @pl.when(j == 0)
    def _():
        xn = _rms(x_ref[...], pg_ref[...], NORM_EPS)
        h_scr[...] = (xn * (1.0 + sc_ref[...]) + sh_ref[...]).astype(BF16)
        acc_scr[...] = jnp.zeros_like(acc_scr)

    h = h_scr[...]
    g = jnp.dot(h, wg_ref[...], preferred_element_type=F32)
    u = jnp.dot(h, wu_ref[...], preferred_element_type=F32)
    a = (g * jax.nn.sigmoid(g) * u).astype(BF16)
    acc_scr[...] += jnp.dot(a, wo_ref[...], preferred_element_type=F32)

    @pl.when(j == pl.num_programs(1) - 1)
    def _():
        yn = _rms(acc_scr[...], qg_ref[...], NORM_EPS)
        o_ref[...] = x_ref[...] + (MACARON_WEIGHT * gt_ref[...]) * yn


def _ffn(x, pre_g, shift, scale, gate, post_g, w_in, w_out, layer):
    s, d = x.shape
    f = w_out.shape[1]
    nf = f // FFN_TF
    vec = pl.BlockSpec((1, d), lambda i, j: (0, 0))
    return pl.pallas_call(
        _ffn_kernel,
        grid=(s // ROW_TILE, nf),
        in_specs=[
            pl.BlockSpec((ROW_TILE, d), lambda i, j: (i, 0)),
            vec, vec, vec, vec, vec,
            pl.BlockSpec((None, d, FFN_TF), lambda i, j: (layer, 0, j)),
            pl.BlockSpec((None, d, FFN_TF), lambda i, j: (layer, 0, nf + j)),
            pl.BlockSpec((None, FFN_TF, d), lambda i, j: (layer, j, 0)),
        ],
        out_specs=pl.BlockSpec((ROW_TILE, d), lambda i, j: (i, 0)),
        out_shape=jax.ShapeDtypeStruct((s, d), F32),
        scratch_shapes=[pltpu.VMEM((ROW_TILE, d), BF16), pltpu.VMEM((ROW_TILE, d), F32)],
        compiler_params=_cparams(("parallel", "arbitrary")),
        name="ffn",
    )(x, pre_g, shift, scale, gate, post_g, w_in, w_in, w_out)


def _mixin_kernel(x_ref, pg_ref, sh_ref, sc_ref, w_ref, cos_ref, sinn_ref, sinp_ref,
                  qt_ref, qn_ref, k_ref, kn_ref, vt_ref, u_ref, *, q_scale):
    xn = _rms(x_ref[...], pg_ref[...], NORM_EPS)
    h = (xn * (1.0 + sc_ref[...]) + sh_ref[...]).astype(BF16)
    cos_t, sin_n, sin_p = cos_ref[...], sinn_ref[...], sinp_ref[...]

    def rope(zs):
        return (zs * cos_t + pltpu.roll(zs, LANES - HEAD_DIM // 2, 1) * sin_n
                + pltpu.roll(zs, HEAD_DIM // 2, 1) * sin_p)

    lane = lax.broadcasted_iota(jnp.int32, (x_ref.shape[0], LANES), 1)

    def max_sq_norms(t):
        t2 = t.astype(F32) * t.astype(F32)
        n1 = jnp.max(jnp.sum(jnp.where(lane < HEAD_DIM, t2, 0.0), axis=1, keepdims=True),
                     axis=0, keepdims=True)
        n2 = jnp.max(jnp.sum(jnp.where(lane < HEAD_DIM, 0.0, t2), axis=1, keepdims=True),
                     axis=0, keepdims=True)
        return jnp.where(lane[:1] < HEAD_DIM, n1, n2)

    zq = jnp.dot(h, w_ref[:, 0:ATTN_WIDTH], preferred_element_type=F32)
    for hd in range(N_HEADS):
        qr = (rope(zq[:, hd * V_DIM:(hd + 1) * V_DIM]) * q_scale).astype(BF16)
        qt_ref[hd] = qr.astype(F32).T.astype(BF16)
        qn_ref[hd:hd + 1, :] = max_sq_norms(qr)
    zk = jnp.dot(h, w_ref[:, ATTN_WIDTH:2 * ATTN_WIDTH], preferred_element_type=F32)
    shift_lanes = jnp.where(lane == 0, 1.0, 0.0).astype(BF16)
    for hd in range(N_HEADS):
        kr = rope(zk[:, hd * V_DIM:(hd + 1) * V_DIM]).astype(BF16)
        k_ref[hd, :, :V_DIM] = kr
        k_ref[hd, :, V_DIM:] = shift_lanes
        kn_ref[hd:hd + 1, :] = max_sq_norms(kr)
    zv = jnp.dot(h, w_ref[:, 2 * ATTN_WIDTH:3 * ATTN_WIDTH], preferred_element_type=F32)
    zvt = zv.T
    for hd in range(N_HEADS):
        vt_ref[hd, :V_DIM, :] = zvt[hd * V_DIM:(hd + 1) * V_DIM, :].astype(BF16)
        vt_ref[hd, V_DIM:, :] = jnp.ones((VT_ROWS - V_DIM, zvt.shape[1]), BF16)
    u_ref[...] = jnp.dot(h, w_ref[:, 3 * ATTN_WIDTH:], preferred_element_type=F32)


def _mix_in(x, pre_g, shift, scale, w, layer, cos_t, sin_n, sin_p, q_scale):
    s, d = x.shape
    n_tiles = s // ROW_TILE
    vec = pl.BlockSpec((1, d), lambda i: (0, 0))
    tab = pl.BlockSpec((ROW_TILE, LANES), lambda i: (i, 0))
    return pl.pallas_call(
        functools.partial(_mixin_kernel, q_scale=q_scale),
        grid=(n_tiles,),
        in_specs=[
            pl.BlockSpec((ROW_TILE, d), lambda i: (i, 0)),
            vec, vec, vec,
            pl.BlockSpec((None,) + w.shape[1:], lambda i: (layer, 0, 0), pipeline_mode=pl.Buffered(1)),
            tab, tab, tab,
        ],
        out_specs=[
            pl.BlockSpec((N_HEADS, None, V_DIM, ROW_TILE), lambda i: (0, i, 0, 0)),
            pl.BlockSpec((None, N_HEADS, LANES), lambda i: (i, 0, 0)),
            pl.BlockSpec((N_HEADS, ROW_TILE, K_LANES), lambda i: (0, i, 0)),
            pl.BlockSpec((None, N_HEADS, LANES), lambda i: (i, 0, 0)),
            pl.BlockSpec((N_HEADS, None, VT_ROWS, ROW_TILE), lambda i: (0, i, 0, 0)),
            pl.BlockSpec((ROW_TILE, FNET_WIDTH), lambda i: (i, 0)),
        ],
        out_shape=[
            jax.ShapeDtypeStruct((N_HEADS, n_tiles, V_DIM, ROW_TILE), BF16),
            jax.ShapeDtypeStruct((n_tiles, N_HEADS, LANES), F32),
            jax.ShapeDtypeStruct((N_HEADS, s, K_LANES), BF16),
            jax.ShapeDtypeStruct((n_tiles, N_HEADS, LANES), F32),
            jax.ShapeDtypeStruct((N_HEADS, n_tiles, VT_ROWS, ROW_TILE), BF16),
            jax.ShapeDtypeStruct((s, FNET_WIDTH), F32),
        ],
        compiler_params=_cparams(("parallel",)),
        name="mix_in",
    )(x, pre_g, shift, scale, w, cos_t, sin_n, sin_p)


def _attn_kernel(qb_ref, kb_ref, lq_ref, g_ref, qt_ref, k_ref, vt_ref, o_ref,
                 qz_scr, s0_scr, s1_scr, c0_scr, c1_scr, acc_scr, m_scr, *, lambda_init):
    tq = qt_ref.shape[1]
    n_kv, _, tk = vt_ref.shape
    n_col = 2 * tq // MXU_DIM
    hd = pl.program_id(0)
    tile = lax.div(pl.program_id(1), ROW_TILE // tq)
    bsq = [qb_ref[tile, 2 * hd + m] * kb_ref[0, 2 * hd + m] for m in range(2)]
    fast = jnp.maximum(bsq[0], bsq[1]) <= (SHIFT_LIMIT / BOUND_MARGIN) ** 2
    shift = jnp.sqrt(jnp.concatenate(
        [jnp.full((1, tq), jnp.where(fast, b, 0.0), F32) for b in bsq], axis=1)) * BOUND_MARGIN
    qt = qt_ref[...].astype(F32)
    row = lax.broadcasted_iota(jnp.int32, (V_DIM, 2 * tq), 0)
    qz_scr[:V_DIM, :] = jnp.where((row < HEAD_DIM) == (lax.broadcasted_iota(
        jnp.int32, (V_DIM, 2 * tq), 1) < tq), jnp.concatenate([qt, qt], axis=1), 0.0).astype(BF16)
    qz_scr[V_DIM:, :] = jnp.where(row == 0, -shift, 0.0).astype(BF16)
    acc_scr[...] = jnp.zeros_like(acc_scr)

    @pl.when(fast)
    def _():
        span = ATTN_TS * ATTN_QK
        last = n_kv * tk - ATTN_QK

        def shifted_scores(r):
            return jnp.dot(k_ref[pl.ds(pl.multiple_of(r, ATTN_QK), ATTN_QK), :], qz_scr[...],
                           preferred_element_type=F32)

        s0_scr[:ATTN_QK, :] = shifted_scores(0)

        def body(t, carry):
            r0 = t * span
            pv = [None] * n_col
            s_next = None
            for kb in range(ATTN_TS):
                s = s0_scr[:ATTN_QK, :] if kb == 0 else s_next
                s_next = shifted_scores(jnp.minimum(r0 + (kb + 1) * ATTN_QK, last))
                for sub in range(ATTN_QK // MXU_DIM):
                    blk, off = divmod(kb * ATTN_QK + sub * MXU_DIM, tk)
                    rows = slice(sub * MXU_DIM, (sub + 1) * MXU_DIM)
                    for nh in range(n_col):
                        p = jnp.exp2(s[rows, nh * MXU_DIM:(nh + 1) * MXU_DIM]).astype(BF16)
                        d = jnp.dot(vt_ref[t * (span // tk) + blk, :, off:off + MXU_DIM], p,
                                    preferred_element_type=F32)
                        pv[nh] = d if pv[nh] is None else pv[nh] + d
            s0_scr[:ATTN_QK, :] = s_next
            for nh in range(n_col):
                cols = slice(nh * MXU_DIM, (nh + 1) * MXU_DIM)
                acc_scr[:, cols] = acc_scr[:, cols] + pv[nh]
            return carry

        lax.fori_loop(0, n_kv * tk // span, body, 0)

    @pl.when(jnp.logical_not(fast))
    def _():
        n_chunks = n_kv // ATTN_SUB
        ck = ATTN_SUB * tk
        m_scr[...] = jnp.full_like(m_scr, -jnp.inf)

        def scores(j, s_scr, c_scr):
            r = pl.multiple_of(j * ck, ck)
            s = jnp.dot(k_ref[pl.ds(r, ck), :], qz_scr[...],
                        preferred_element_type=F32)
            s_scr[...] = s
            c_scr[...] = jnp.max(s, axis=0, keepdims=True)

        def consume(j, s_scr, c_scr):
            m_prev = m_scr[...]
            m_new = jnp.maximum(m_prev, c_scr[...])
            alpha = jnp.exp2(m_prev - m_new)
            for nh in range(n_col):
                cols = slice(nh * MXU_DIM, (nh + 1) * MXU_DIM)
                pv = None
                for kb in range(ck // MXU_DIM):
                    p = jnp.exp2(s_scr[kb * MXU_DIM:(kb + 1) * MXU_DIM, cols] - m_new[:, cols])
                    blk, off = divmod(kb * MXU_DIM, tk)
                    d = jnp.dot(vt_ref[j * ATTN_SUB + blk, :, off:off + MXU_DIM], p.astype(BF16),
                                preferred_element_type=F32)
                    pv = d if pv is None else pv + d
                acc_scr[:, cols] = acc_scr[:, cols] * alpha[:, cols] + pv
            m_scr[...] = m_new

        scores(0, s0_scr, c0_scr)

        def body(i, carry):
            j = 2 * i
            scores(j + 1, s1_scr, c1_scr)
            consume(j, s0_scr, c0_scr)
            scores(jnp.minimum(j + 2, n_chunks - 1), s0_scr, c0_scr)
            consume(j + 1, s1_scr, c1_scr)
            return carry

        lax.fori_loop(0, n_chunks // 2, body, 0)

    lq = lq_ref[...]
    lam = (jnp.exp(jnp.sum(lq[0:1] * lq[1:2], axis=-1, keepdims=True))
           - jnp.exp(jnp.sum(lq[2:3] * lq[3:4], axis=-1, keepdims=True)) + lambda_init)
    on = acc_scr[:V_DIM, :] / acc_scr[V_DIM:V_DIM + 1, :]
    ot = on[:, :tq] - lam * on[:, tq:]
    ms = jnp.mean(ot * ot, axis=0, keepdims=True)
    o_ref[...] = (ot * lax.rsqrt(ms + SUBLN_EPS) * (g_ref[...] * (1.0 - lambda_init))).astype(BF16)


def _attention(qt, qn, k, kn, vt, lambda_qk, subln_g, lambda_init):
    n_heads, n_tiles, _, _ = qt.shape
    s = k.shape[1]
    n_kv = vt.shape[1]
    kmax = jnp.max(kn, axis=0)
    kb = jnp.stack([kmax[:, 0], kmax[:, HEAD_DIM]], axis=1).reshape(1, 2 * n_heads)
    qb = jnp.stack([qn[:, :, 0], qn[:, :, HEAD_DIM]], axis=2).reshape(n_tiles, 2 * n_heads)
    per_tile = ROW_TILE // ATTN_TQ
    return pl.pallas_call(
        functools.partial(_attn_kernel, lambda_init=lambda_init),
        grid=(n_heads, s // ATTN_TQ),
        in_specs=[
            pl.BlockSpec(memory_space=pltpu.SMEM),
            pl.BlockSpec(memory_space=pltpu.SMEM),
            pl.BlockSpec(lambda_qk.shape, lambda h, i: (0, 0)),
            pl.BlockSpec((V_DIM, 1), lambda h, i: (0, 0)),
            pl.BlockSpec((None, None, V_DIM, ATTN_TQ),
                         lambda h, i: (h, i // per_tile, 0, i % per_tile)),
            pl.BlockSpec((None, s, K_LANES), lambda h, i: (h, 0, 0)),
            pl.BlockSpec((None, n_kv, VT_ROWS, ATTN_TK), lambda h, i: (h, 0, 0, 0)),
        ],
        out_specs=pl.BlockSpec((V_DIM, ATTN_TQ), lambda h, i: (h, i)),
        out_shape=jax.ShapeDtypeStruct((n_heads * V_DIM, s), BF16),
        scratch_shapes=[pltpu.VMEM((K_LANES, 2 * ATTN_TQ), BF16),
                        pltpu.VMEM((ATTN_SUB * ATTN_TK, 2 * ATTN_TQ), F32),
                        pltpu.VMEM((ATTN_SUB * ATTN_TK, 2 * ATTN_TQ), F32),
                        pltpu.VMEM((1, 2 * ATTN_TQ), F32),
                        pltpu.VMEM((1, 2 * ATTN_TQ), F32),
                        pltpu.VMEM((VT_ROWS, 2 * ATTN_TQ), F32),
                        pltpu.VMEM((1, 2 * ATTN_TQ), F32)],
        compiler_params=_cparams(("parallel", "parallel")),
        name="diff_attn",
    )(qb, kb, lambda_qk, subln_g.reshape(V_DIM, 1), qt, k, vt)


def _dft_tables():
    n = FFT_N
    jk = np.outer(np.arange(n), np.arange(n)) % n
    ang = 2.0 * np.pi * jk / n
    c = np.cos(ang) / math.sqrt(n)
    s = np.sin(ang) / math.sqrt(n)
    stage1 = np.concatenate([c, -s], axis=0)
    stage2 = np.block([[c, s], [-s, c]])
    chan = np.concatenate([c, s], axis=0)
    tw_ang = 2.0 * np.pi * np.outer(np.arange(n), np.arange(n)) / (n * n)
    return (_split(jnp.asarray(stage1, F32)), _split(jnp.asarray(stage2, F32)),
            _split(jnp.asarray(chan, F32)),
            jnp.asarray(np.cos(tw_ang), F32), jnp.asarray(np.sin(tw_ang), F32))


def _split(a):
    hi = a.astype(BF16)
    return hi, (a - hi.astype(F32)).astype(BF16)


def _dot3(a, b):
    (a_hi, a_lo), (b_hi, b_lo) = a, b
    return (jnp.dot(a_hi, b_hi, preferred_element_type=F32)
            + jnp.dot(a_hi, b_lo, preferred_element_type=F32)
            + jnp.dot(a_lo, b_hi, preferred_element_type=F32))


def _fft1_kernel(x_ref, fh_ref, fl_ref, twc_ref, tws_ref, o_ref):
    n = FFT_N
    t = _dot3((fh_ref[...], fl_ref[...]), _split(x_ref[...]))
    width = x_ref.shape[1] // FFT_NB
    for b in range(FFT_NB):
        tr = t[:n, b * width:(b + 1) * width]
        ti = t[n:, b * width:(b + 1) * width]
        c = jnp.concatenate([twc_ref[b]] * (width // LANES), axis=1)
        s = jnp.concatenate([tws_ref[b]] * (width // LANES), axis=1)
        o_ref[0, :, b * width:(b + 1) * width] = tr * c + ti * s
        o_ref[1, :, b * width:(b + 1) * width] = ti * c - tr * s


def _fft2_kernel(t_ref, fh_ref, fl_ref, chh_ref, chl_ref, o_ref):
    n = FFT_N
    width = t_ref.shape[3]
    f = (fh_ref[...], fl_ref[...])
    ch = (chh_ref[...], chl_ref[...])
    for b in range(FFT_KB):
        tt = jnp.concatenate([t_ref[0, b], t_ref[1, b]], axis=0)
        z = _dot3(f, _split(tt))
        for g in range(N_GROUPS):
            zz = jnp.concatenate([z[:n, g * GROUP_DIM:(g + 1) * GROUP_DIM],
                                  z[n:, g * GROUP_DIM:(g + 1) * GROUP_DIM]], axis=1)
            y = _dot3(_split(zz), ch)
            o_ref[:, b * width + g * GROUP_DIM:b * width + (g + 1) * GROUP_DIM] = y.astype(o_ref.dtype)


def _fourier_mix(u, tables):
    s, width = u.shape
    n = FFT_N
    stage1, stage2, chan, twc, tws = tables
    twc_b = jnp.broadcast_to(twc[:, :, None], (n, n, LANES))
    tws_b = jnp.broadcast_to(tws[:, :, None], (n, n, LANES))
    t = pl.pallas_call(
        _fft1_kernel,
        grid=(n // FFT_NB,),
        in_specs=[
            pl.BlockSpec((n, FFT_NB * width), lambda j: (0, j)),
            pl.BlockSpec((2 * n, n), lambda j: (0, 0)),
            pl.BlockSpec((2 * n, n), lambda j: (0, 0)),
            pl.BlockSpec((FFT_NB, n, LANES), lambda j: (j, 0, 0)),
            pl.BlockSpec((FFT_NB, n, LANES), lambda j: (j, 0, 0)),
        ],
        out_specs=pl.BlockSpec((2, n, FFT_NB * width), lambda j: (0, 0, j)),
        out_shape=jax.ShapeDtypeStruct((2, n, n * width), F32),
        compiler_params=_cparams(("parallel",)),
        name="fft_stage1",
    )(u.reshape(n, n * width), *stage1, twc_b, tws_b)
    y = pl.pallas_call(
        _fft2_kernel,
        grid=(n // FFT_KB,),
        in_specs=[
            pl.BlockSpec((2, FFT_KB, n, width), lambda i: (0, i, 0, 0)),
            pl.BlockSpec((2 * n, 2 * n), lambda i: (0, 0)),
            pl.BlockSpec((2 * n, 2 * n), lambda i: (0, 0)),
            pl.BlockSpec((2 * n, n), lambda i: (0, 0)),
            pl.BlockSpec((2 * n, n), lambda i: (0, 0)),
        ],
        out_specs=pl.BlockSpec((n, FFT_KB * width), lambda i: (0, i)),
        out_shape=jax.ShapeDtypeStruct((n, n * width), BF16),
        compiler_params=_cparams(("parallel",)),
        name="fft_stage2",
    )(t.reshape(2, n, n, width), *stage2, *chan)
    return y.reshape(s, width)


def _mixout_kernel(x_ref, pg_ref, sh_ref, sc_ref, gt_ref, qg_ref, ao_ref, fy_ref,
                   gwa_ref, gwf_ref, gba_ref, gbf_ref, ap_ref, fp_ref, wo_ref,
                   o_ref, h_scr, acc_scr):
    j = pl.program_id(1)

    @pl.when(j == 0)
    def _():
        xn = _rms(x_ref[...], pg_ref[...], NORM_EPS)
        h_scr[...] = (xn * (1.0 + sc_ref[...]) + sh_ref[...]).astype(BF16)
        acc_scr[...] = jnp.zeros_like(acc_scr)

    h = h_scr[...]
    ga = jax.nn.sigmoid(jnp.dot(h, gwa_ref[...], preferred_element_type=F32) + gba_ref[...])
    gf = jax.nn.sigmoid(jnp.dot(h, gwf_ref[...], preferred_element_type=F32) + gbf_ref[...])
    ya = lax.dot_general(ao_ref[...], ap_ref[...], (((0,), (0,)), ((), ())),
                         preferred_element_type=F32)
    yf = jnp.dot(fy_ref[...], fp_ref[...], preferred_element_type=F32)
    y = (ga * ya + gf * yf).astype(BF16)
    acc_scr[...] += jnp.dot(y, wo_ref[...], preferred_element_type=F32)

    @pl.when(j == pl.num_programs(1) - 1)
    def _():
        yn = _rms(acc_scr[...], qg_ref[...], NORM_EPS)
        o_ref[...] = x_ref[...] + gt_ref[...] * yn


def _mix_out(x, pre_g, shift, scale, gate, post_g, ao, fy, gate_w, gate_b, attn_proj,
             fnet_proj, w_out, layer):
    s, d = x.shape
    nc = d // MIX_TC
    vec = pl.BlockSpec((1, d), lambda i, j: (0, 0))
    return pl.pallas_call(
        _mixout_kernel,
        grid=(s // ROW_TILE, nc),
        in_specs=[
            pl.BlockSpec((ROW_TILE, d), lambda i, j: (i, 0)),
            vec, vec, vec, vec, vec,
            pl.BlockSpec((ATTN_WIDTH, ROW_TILE), lambda i, j: (0, i)),
            pl.BlockSpec((ROW_TILE, FNET_WIDTH), lambda i, j: (i, 0)),
            pl.BlockSpec((None, d, MIX_TC), lambda i, j: (layer, 0, j)),
            pl.BlockSpec((None, d, MIX_TC), lambda i, j: (layer, 0, nc + j)),
            pl.BlockSpec((1, MIX_TC), lambda i, j: (0, j)),
            pl.BlockSpec((1, MIX_TC), lambda i, j: (0, nc + j)),
            pl.BlockSpec((None, ATTN_WIDTH, MIX_TC), lambda i, j: (layer, 0, j)),
            pl.BlockSpec((None, FNET_WIDTH, MIX_TC), lambda i, j: (layer, 0, j)),
            pl.BlockSpec((None, MIX_TC, d), lambda i, j: (layer, j, 0)),
        ],
        out_specs=pl.BlockSpec((ROW_TILE, d), lambda i, j: (i, 0)),
        out_shape=jax.ShapeDtypeStruct((s, d), F32),
        scratch_shapes=[pltpu.VMEM((ROW_TILE, d), BF16), pltpu.VMEM((ROW_TILE, d), F32)],
        compiler_params=_cparams(("parallel", "arbitrary")),
        name="mix_out",
    )(x, pre_g, shift, scale, gate, post_g, ao, fy, gate_w, gate_w, gate_b, gate_b,
      attn_proj, fnet_proj, w_out)


def _rope_tables(seq):
    half = HEAD_DIM // 2
    pos = jnp.arange(seq, dtype=F32)
    inv_freq = ROPE_THETA ** (-jnp.arange(0, HEAD_DIM, 2, dtype=F32) / HEAD_DIM)
    ang = pos[:, None] * inv_freq[None, :]
    cos, sin = jnp.cos(ang), jnp.sin(ang)
    zero = jnp.zeros_like(sin)
    cos_t = jnp.concatenate([cos] * (LANES // half), axis=1)
    sin_n = jnp.concatenate([-sin, zero] * (LANES // HEAD_DIM), axis=1)
    sin_p = jnp.concatenate([zero, sin] * (LANES // HEAD_DIM), axis=1)
    return cos_t, sin_n, sin_p


def kernel(x, c, ada_w, ada_b, pre_norm_g, post_norm_g, ffn1_w_in, ffn1_w_out, mix_w_in,
           lambda_qk, subln_g, attn_proj, fnet_proj, branch_gate_w, branch_gate_b, mix_w_out,
           ffn2_w_in, ffn2_w_out):
    b, s, d = x.shape
    assert b == 1 and s == FFT_N * FFT_N and s % ROW_TILE == 0
    n_layers = ada_w.shape[0]
    cos_t, sin_n, sin_p = _rope_tables(s)
    dft = _dft_tables()
    mod = _ada_mod(c, ada_w, ada_b).reshape(n_layers, 3, 3, 1, d)
    q_scale = HEAD_DIM ** -0.5 * math.log2(math.e)
    xs = x.reshape(s, d)
    ffn1_in, ffn1_out, ffn2_in, ffn2_out, mix_in_w, gate_w, attn_w, fnet_w, mix_out_w = (
        w.astype(BF16) for w in (ffn1_w_in, ffn1_w_out, ffn2_w_in, ffn2_w_out, mix_w_in,
                                 branch_gate_w, attn_proj, fnet_proj, mix_w_out))
    for l in range(n_layers):
        lambda_init = 0.8 - 0.6 * math.exp(-0.3 * l)
        pre = pre_norm_g[l].reshape(3, 1, d)
        post = post_norm_g[l].reshape(3, 1, d)
        xs = _ffn(xs, pre[0], mod[l, 0, 0], mod[l, 0, 1], mod[l, 0, 2], post[0],
                  ffn1_in, ffn1_out, l)
        qt, qn, k, kn, vt, u = _mix_in(xs, pre[1], mod[l, 1, 0], mod[l, 1, 1], mix_in_w, l,
                              cos_t, sin_n, sin_p, q_scale)
        ao = _attention(qt, qn, k, kn, vt, lambda_qk[l], subln_g[l], lambda_init)
        fy = _fourier_mix(u, dft)
        xs = _mix_out(xs, pre[1], mod[l, 1, 0], mod[l, 1, 1], mod[l, 1, 2], post[1], ao, fy,
                      gate_w, branch_gate_b[l].reshape(1, 2 * d), attn_w, fnet_w, mix_out_w, l)
        xs = _ffn(xs, pre[2], mod[l, 2, 0], mod[l, 2, 1], mod[l, 2, 2], post[2],
                  ffn2_in, ffn2_out, l)
    return xs.reshape(b, s, d)
```

```python
import functools
import math

import numpy as np
import jax
import jax.numpy as jnp
from jax import lax
from jax.experimental import pallas as pl
from jax.experimental.pallas import tpu as pltpu

F32 = jnp.float32
BF16 = jnp.bfloat16

N_HEADS = 8
HEAD_DIM = 64
V_DIM = 2 * HEAD_DIM
ACC_ROWS = V_DIM + 8
K_LANES = 2 * V_DIM
ATTN_WIDTH = N_HEADS * V_DIM
N_GROUPS = 8
GROUP_DIM = 128
FNET_WIDTH = N_GROUPS * GROUP_DIM
ROPE_THETA = 10000.0
NORM_EPS = 1e-6
SUBLN_EPS = 1e-5
MACARON_WEIGHT = 0.5

LANES = 128
SUBLANES = 8
MXU_DIM = 256
VMEM_LIMIT_BYTES = 56 * 1024 * 1024

ADA_TN = 1024
ROW_TILE = 512
FFN_TF = 512
MIX_TC = 512
ATTN_TQ = 256
ATTN_TK = ROW_TILE
ATTN_SUB = 2
ATTN_QK = 512
ATTN_TS = 32
SHIFT_LIMIT = 60.0
BOUND_MARGIN = 1.0 + 2.0 ** -7
FFT_N = 128
FFT_NB = 4
FFT_KB = 4


def _cparams(sem):
    return pltpu.CompilerParams(dimension_semantics=sem, vmem_limit_bytes=VMEM_LIMIT_BYTES)


def _rms(x, g, eps):
    ms = jnp.mean(x * x, axis=-1, keepdims=True)
    return x * lax.rsqrt(ms + eps) * g


def _ada_kernel(c_ref, w_ref, b_ref, o_ref):
    d, tn = w_ref.shape

    def body(k, acc):
        r = pl.multiple_of(k * SUBLANES, SUBLANES)
        cb = c_ref[pl.ds(r, SUBLANES), :]
        cb = cb * jax.nn.sigmoid(cb)
        return acc + w_ref[pl.ds(r, SUBLANES), :] * jnp.concatenate([cb] * (tn // LANES), axis=1)

    acc = lax.fori_loop(0, d // SUBLANES, body, jnp.zeros((SUBLANES, tn), F32), unroll=8)
    o_ref[...] = jnp.sum(acc, axis=0, keepdims=True) + b_ref[...]


def _ada_mod(c, ada_w, ada_b):
    n_layers, d, n = ada_w.shape
    c_b = jnp.broadcast_to(c.reshape(d, 1), (d, LANES))
    out = pl.pallas_call(
        _ada_kernel,
        grid=(n_layers, n // ADA_TN),
        in_specs=[
            pl.BlockSpec((d, LANES), lambda l, j: (0, 0)),
            pl.BlockSpec((None, d, ADA_TN), lambda l, j: (l, 0, j)),
            pl.BlockSpec((None, 1, ADA_TN), lambda l, j: (l, 0, j)),
        ],
        out_specs=pl.BlockSpec((None, 1, ADA_TN), lambda l, j: (l, 0, j)),
        out_shape=jax.ShapeDtypeStruct((n_layers, 1, n), F32),
        compiler_params=_cparams(("parallel", "parallel")),
        name="ada_mod",
    )(c_b, ada_w, ada_b.reshape(n_layers, 1, n))
    return out


def _ffn_kernel(x_ref, pg_ref, sh_ref, sc_ref, gt_ref, qg_ref, wg_ref, wu_ref, wo_ref,
                o_ref, h_scr, acc_scr):
    j = pl.program_id(1)

    @pl.when(j == 0)
    def _():
        xn = _rms(x_ref[...], pg_ref[...], NORM_EPS)
        h_scr[...] = (xn * (1.0 + sc_ref[...]) + sh_ref[...]).astype(BF16)
        acc_scr[...] = jnp.zeros_like(acc_scr)

    h = h_scr[...]
    g = jnp.dot(h, wg_ref[...], preferred_element_type=F32)
    u = jnp.dot(h, wu_ref[...], preferred_element_type=F32)
    a = (g * jax.nn.sigmoid(g) * u).astype(BF16)
    acc_scr[...] += jnp.dot(a, wo_ref[...], preferred_element_type=F32)

    @pl.when(j == pl.num_programs(1) - 1)
    def _():
        yn = _rms(acc_scr[...], qg_ref[...], NORM_EPS)
        o_ref[...] = x_ref[...] + (MACARON_WEIGHT * gt_ref[...]) * yn


def _ffn(x, pre_g, shift, scale, gate, post_g, w_in, w_out, layer):
    s, d = x.shape
    f = w_out.shape[1]
    nf = f // FFN_TF
    vec = pl.BlockSpec((1, d), lambda i, j: (0, 0))
    return pl.pallas_call(
        _ffn_kernel,
        grid=(s // ROW_TILE, nf),
        in_specs=[
            pl.BlockSpec((ROW_TILE, d), lambda i, j: (i, 0)),
            vec, vec, vec, vec, vec,
            pl.BlockSpec((None, d, FFN_TF), lambda i, j: (layer, 0, j)),
            pl.BlockSpec((None, d, FFN_TF), lambda i, j: (layer, 0, nf + j)),
            pl.BlockSpec((None, FFN_TF, d), lambda i, j: (layer, j, 0)),
        ],
        out_specs=pl.BlockSpec((ROW_TILE, d), lambda i, j: (i, 0)),
        out_shape=jax.ShapeDtypeStruct((s, d), F32),
        scratch_shapes=[pltpu.VMEM((ROW_TILE, d), BF16), pltpu.VMEM((ROW_TILE, d), F32)],
        compiler_params=_cparams(("parallel", "arbitrary")),
        name="ffn",
    )(x, pre_g, shift, scale, gate, post_g, w_in, w_in, w_out)


def _mixin_kernel(x_ref, pg_ref, sh_ref, sc_ref, w_ref, cos_ref, sinn_ref, sinp_ref,
                  qt_ref, qn_ref, k_ref, kn_ref, vt_ref, u_ref, *, q_scale):
    xn = _rms(x_ref[...], pg_ref[...], NORM_EPS)
    h = (xn * (1.0 + sc_ref[...]) + sh_ref[...]).astype(BF16)
    cos_t, sin_n, sin_p = cos_ref[...], sinn_ref[...], sinp_ref[...]

    def rope(zs):
        return (zs * cos_t + pltpu.roll(zs, LANES - HEAD_DIM // 2, 1) * sin_n
                + pltpu.roll(zs, HEAD_DIM // 2, 1) * sin_p)

    lane = lax.broadcasted_iota(jnp.int32, (x_ref.shape[0], LANES), 1)

    def max_sq_norms(t):
        t2 = t.astype(F32) * t.astype(F32)
        n1 = jnp.max(jnp.sum(jnp.where(lane < HEAD_DIM, t2, 0.0), axis=1, keepdims=True),
                     axis=0, keepdims=True)
        n2 = jnp.max(jnp.sum(jnp.where(lane < HEAD_DIM, 0.0, t2), axis=1, keepdims=True),
                     axis=0, keepdims=True)
        return jnp.where(lane[:1] < HEAD_DIM, n1, n2)

    zq = jnp.dot(h, w_ref[:, 0:ATTN_WIDTH], preferred_element_type=F32)
    for hd in range(N_HEADS):
        qr = (rope(zq[:, hd * V_DIM:(hd + 1) * V_DIM]) * q_scale).astype(BF16)
        qt_ref[hd] = qr.astype(F32).T.astype(BF16)
        qn_ref[hd:hd + 1, :] = max_sq_norms(qr)
    zk = jnp.dot(h, w_ref[:, ATTN_WIDTH:2 * ATTN_WIDTH], preferred_element_type=F32)
    shift_lanes = jnp.where(lane == 0, 1.0, 0.0).astype(BF16)
    for hd in range(N_HEADS):
        kr = rope(zk[:, hd * V_DIM:(hd + 1) * V_DIM]).astype(BF16)
        k_ref[hd, :, :V_DIM] = kr
        k_ref[hd, :, V_DIM:] = shift_lanes
        kn_ref[hd:hd + 1, :] = max_sq_norms(kr)
    zv = jnp.dot(h, w_ref[:, 2 * ATTN_WIDTH:3 * ATTN_WIDTH], preferred_element_type=F32)
    zvt = zv.T
    for hd in range(N_HEADS):
        vt_ref[hd] = zvt[hd * V_DIM:(hd + 1) * V_DIM, :].astype(BF16)
    u_ref[...] = jnp.dot(h, w_ref[:, 3 * ATTN_WIDTH:], preferred_element_type=F32)


def _mix_in(x, pre_g, shift, scale, w, layer, cos_t, sin_n, sin_p, q_scale):
    s, d = x.shape
    n_tiles = s // ROW_TILE
    vec = pl.BlockSpec((1, d), lambda i: (0, 0))
    tab = pl.BlockSpec((ROW_TILE, LANES), lambda i: (i, 0))
    return pl.pallas_call(
        functools.partial(_mixin_kernel, q_scale=q_scale),
        grid=(n_tiles,),
        in_specs=[
            pl.BlockSpec((ROW_TILE, d), lambda i: (i, 0)),
            vec, vec, vec,
            pl.BlockSpec((None,) + w.shape[1:], lambda i: (layer, 0, 0), pipeline_mode=pl.Buffered(1)),
            tab, tab, tab,
        ],
        out_specs=[
            pl.BlockSpec((N_HEADS, None, V_DIM, ROW_TILE), lambda i: (0, i, 0, 0)),
            pl.BlockSpec((None, N_HEADS, LANES), lambda i: (i, 0, 0)),
            pl.BlockSpec((N_HEADS, ROW_TILE, K_LANES), lambda i: (0, i, 0)),
            pl.BlockSpec((None, N_HEADS, LANES), lambda i: (i, 0, 0)),
            pl.BlockSpec((N_HEADS, None, V_DIM, ROW_TILE), lambda i: (0, i, 0, 0)),
            pl.BlockSpec((ROW_TILE, FNET_WIDTH), lambda i: (i, 0)),
        ],
        out_shape=[
            jax.ShapeDtypeStruct((N_HEADS, n_tiles, V_DIM, ROW_TILE), BF16),
            jax.ShapeDtypeStruct((n_tiles, N_HEADS, LANES), F32),
            jax.ShapeDtypeStruct((N_HEADS, s, K_LANES), BF16),
            jax.ShapeDtypeStruct((n_tiles, N_HEADS, LANES), F32),
            jax.ShapeDtypeStruct((N_HEADS, n_tiles, V_DIM, ROW_TILE), BF16),
            jax.ShapeDtypeStruct((s, FNET_WIDTH), F32),
        ],
        compiler_params=_cparams(("parallel",)),
        name="mix_in",
    )(x, pre_g, shift, scale, w, cos_t, sin_n, sin_p)


def _sublane_partial_sum(p):
    return p.reshape(p.shape[0] // SUBLANES, SUBLANES, p.shape[1]).sum(axis=0)


def _attn_kernel(qb_ref, kb_ref, lq_ref, g_ref, qt_ref, k_ref, vt_ref, o_ref,
                 qz_scr, s0_scr, s1_scr, c0_scr, c1_scr, acc_scr, m_scr, *, lambda_init):
    tq = qt_ref.shape[1]
    n_kv, _, tk = vt_ref.shape
    n_col = 2 * tq // MXU_DIM
    hd = pl.program_id(0)
    tile = lax.div(pl.program_id(1), ROW_TILE // tq)
    bsq = [qb_ref[tile, 2 * hd + m] * kb_ref[0, 2 * hd + m] for m in range(2)]
    fast = jnp.maximum(bsq[0], bsq[1]) <= (SHIFT_LIMIT / BOUND_MARGIN) ** 2
    shift = jnp.sqrt(jnp.concatenate(
        [jnp.full((1, tq), jnp.where(fast, b, 0.0), F32) for b in bsq], axis=1)) * BOUND_MARGIN
    qt = qt_ref[...].astype(F32)
    row = lax.broadcasted_iota(jnp.int32, (V_DIM, 2 * tq), 0)
    qz_scr[:V_DIM, :] = jnp.where((row < HEAD_DIM) == (lax.broadcasted_iota(
        jnp.int32, (V_DIM, 2 * tq), 1) < tq), jnp.concatenate([qt, qt], axis=1), 0.0).astype(BF16)
    qz_scr[V_DIM:, :] = jnp.where(row == 0, -shift, 0.0).astype(BF16)
    acc_scr[...] = jnp.zeros_like(acc_scr)

    @pl.when(fast)
    def _():
        span = ATTN_TS * ATTN_QK
        last = n_kv * tk - ATTN_QK

        def shifted_scores(r):
            return jnp.dot(k_ref[pl.ds(pl.multiple_of(r, ATTN_QK), ATTN_QK), :], qz_scr[...],
                           preferred_element_type=F32)

        s0_scr[:ATTN_QK, :] = shifted_scores(0)

        def body(t, carry):
            r0 = t * span
            pv = [None] * n_col
            psum = [jnp.zeros((SUBLANES, MXU_DIM), F32)] * n_col
            s_next = None
            for kb in range(ATTN_TS):
                s = s0_scr[:ATTN_QK, :] if kb == 0 else s_next
                s_next = shifted_scores(jnp.minimum(r0 + (kb + 1) * ATTN_QK, last))
                for sub in range(ATTN_QK // MXU_DIM):
                    blk, off = divmod(kb * ATTN_QK + sub * MXU_DIM, tk)
                    rows = slice(sub * MXU_DIM, (sub + 1) * MXU_DIM)
                    for nh in range(n_col):
                        p = jnp.exp2(s[rows, nh * MXU_DIM:(nh + 1) * MXU_DIM])
                        psum[nh] = psum[nh] + _sublane_partial_sum(p)
                        d = jnp.dot(vt_ref[t * (span // tk) + blk, :, off:off + MXU_DIM],
                                    p.astype(BF16), preferred_element_type=F32)
                        pv[nh] = d if pv[nh] is None else pv[nh] + d
            s0_scr[:ATTN_QK, :] = s_next
            for nh in range(n_col):
                cols = slice(nh * MXU_DIM, (nh + 1) * MXU_DIM)
                acc_scr[:V_DIM, cols] = acc_scr[:V_DIM, cols] + pv[nh]
                acc_scr[V_DIM:, cols] = acc_scr[V_DIM:, cols] + psum[nh]
            return carry

        lax.fori_loop(0, n_kv * tk // span, body, 0)

    @pl.when(jnp.logical_not(fast))
    def _():
        n_chunks = n_kv // ATTN_SUB
        ck = ATTN_SUB * tk
        m_scr[...] = jnp.full_like(m_scr, -jnp.inf)

        def scores(j, s_scr, c_scr):
            r = pl.multiple_of(j * ck, ck)
            s = jnp.dot(k_ref[pl.ds(r, ck), :], qz_scr[...],
                        preferred_element_type=F32)
            s_scr[...] = s
            c_scr[...] = jnp.max(s, axis=0, keepdims=True)

        def consume(j, s_scr, c_scr):
            m_prev = m_scr[...]
            m_new = jnp.maximum(m_prev, c_scr[...])
            alpha = jnp.exp2(m_prev - m_new)
            for nh in range(n_col):
                cols = slice(nh * MXU_DIM, (nh + 1) * MXU_DIM)
                pv = None
                psum = jnp.zeros((SUBLANES, MXU_DIM), F32)
                for kb in range(ck // MXU_DIM):
                    p = jnp.exp2(s_scr[kb * MXU_DIM:(kb + 1) * MXU_DIM, cols] - m_new[:, cols])
                    psum = psum + _sublane_partial_sum(p)
                    blk, off = divmod(kb * MXU_DIM, tk)
                    d = jnp.dot(vt_ref[j * ATTN_SUB + blk, :, off:off + MXU_DIM], p.astype(BF16),
                                preferred_element_type=F32)
                    pv = d if pv is None else pv + d
                acc_scr[:V_DIM, cols] = acc_scr[:V_DIM, cols] * alpha[:, cols] + pv
                acc_scr[V_DIM:, cols] = acc_scr[V_DIM:, cols] * alpha[:, cols] + psum
            m_scr[...] = m_new

        scores(0, s0_scr, c0_scr)

        def body(i, carry):
            j = 2 * i
            scores(j + 1, s1_scr, c1_scr)
            consume(j, s0_scr, c0_scr)
            scores(jnp.minimum(j + 2, n_chunks - 1), s0_scr, c0_scr)
            consume(j + 1, s1_scr, c1_scr)
            return carry

        lax.fori_loop(0, n_chunks // 2, body, 0)

    lq = lq_ref[...]
    lam = (jnp.exp(jnp.sum(lq[0:1] * lq[1:2], axis=-1, keepdims=True))
           - jnp.exp(jnp.sum(lq[2:3] * lq[3:4], axis=-1, keepdims=True)) + lambda_init)
    on = acc_scr[:V_DIM, :] / jnp.sum(acc_scr[V_DIM:, :], axis=0, keepdims=True)
    ot = on[:, :tq] - lam * on[:, tq:]
    ms = jnp.mean(ot * ot, axis=0, keepdims=True)
    o_ref[...] = (ot * lax.rsqrt(ms + SUBLN_EPS) * (g_ref[...] * (1.0 - lambda_init))).astype(BF16)


def _attention(qt, qn, k, kn, vt, lambda_qk, subln_g, lambda_init):
    n_heads, n_tiles, _, _ = qt.shape
    s = k.shape[1]
    n_kv = vt.shape[1]
    kmax = jnp.max(kn, axis=0)
    kb = jnp.stack([kmax[:, 0], kmax[:, HEAD_DIM]], axis=1).reshape(1, 2 * n_heads)
    qb = jnp.stack([qn[:, :, 0], qn[:, :, HEAD_DIM]], axis=2).reshape(n_tiles, 2 * n_heads)
    per_tile = ROW_TILE // ATTN_TQ
    return pl.pallas_call(
        functools.partial(_attn_kernel, lambda_init=lambda_init),
        grid=(n_heads, s // ATTN_TQ),
        in_specs=[
            pl.BlockSpec(memory_space=pltpu.SMEM),
            pl.BlockSpec(memory_space=pltpu.SMEM),
            pl.BlockSpec(lambda_qk.shape, lambda h, i: (0, 0)),
            pl.BlockSpec((V_DIM, 1), lambda h, i: (0, 0)),
            pl.BlockSpec((None, None, V_DIM, ATTN_TQ),
                         lambda h, i: (h, i // per_tile, 0, i % per_tile)),
            pl.BlockSpec((None, s, K_LANES), lambda h, i: (h, 0, 0)),
            pl.BlockSpec((None, n_kv, V_DIM, ATTN_TK), lambda h, i: (h, 0, 0, 0)),
        ],
        out_specs=pl.BlockSpec((V_DIM, ATTN_TQ), lambda h, i: (h, i)),
        out_shape=jax.ShapeDtypeStruct((n_heads * V_DIM, s), BF16),
        scratch_shapes=[pltpu.VMEM((K_LANES, 2 * ATTN_TQ), BF16),
                        pltpu.VMEM((ATTN_SUB * ATTN_TK, 2 * ATTN_TQ), F32),
                        pltpu.VMEM((ATTN_SUB * ATTN_TK, 2 * ATTN_TQ), F32),
                        pltpu.VMEM((1, 2 * ATTN_TQ), F32),
                        pltpu.VMEM((1, 2 * ATTN_TQ), F32),
                        pltpu.VMEM((ACC_ROWS, 2 * ATTN_TQ), F32),
                        pltpu.VMEM((1, 2 * ATTN_TQ), F32)],
        compiler_params=_cparams(("parallel", "parallel")),
        name="diff_attn",
    )(qb, kb, lambda_qk, subln_g.reshape(V_DIM, 1), qt, k, vt)


def _dft_tables():
    n = FFT_N
    jk = np.outer(np.arange(n), np.arange(n)) % n
    ang = 2.0 * np.pi * jk / n
    c = np.cos(ang) / math.sqrt(n)
    s = np.sin(ang) / math.sqrt(n)
    stage1 = np.concatenate([c, -s], axis=0)
    stage2 = np.block([[c, s], [-s, c]])
    chan = np.concatenate([c, s], axis=0)
    tw_ang = 2.0 * np.pi * np.outer(np.arange(n), np.arange(n)) / (n * n)
    return (_split(jnp.asarray(stage1, F32)), _split(jnp.asarray(stage2, F32)),
            _split(jnp.asarray(chan, F32)),
            jnp.asarray(np.cos(tw_ang), F32), jnp.asarray(np.sin(tw_ang), F32))


def _split(a):
    hi = a.astype(BF16)
    return hi, (a - hi.astype(F32)).astype(BF16)


def _dot3(a, b):
    (a_hi, a_lo), (b_hi, b_lo) = a, b
    return (jnp.dot(a_hi, b_hi, preferred_element_type=F32)
            + jnp.dot(a_hi, b_lo, preferred_element_type=F32)
            + jnp.dot(a_lo, b_hi, preferred_element_type=F32))


def _fft1_kernel(x_ref, fh_ref, fl_ref, twc_ref, tws_ref, o_ref):
    n = FFT_N
    t = _dot3((fh_ref[...], fl_ref[...]), _split(x_ref[...]))
    width = x_ref.shape[1] // FFT_NB
    for b in range(FFT_NB):
        tr = t[:n, b * width:(b + 1) * width]
        ti = t[n:, b * width:(b + 1) * width]
        c = jnp.concatenate([twc_ref[b]] * (width // LANES), axis=1)
        s = jnp.concatenate([tws_ref[b]] * (width // LANES), axis=1)
        o_ref[0, :, b * width:(b + 1) * width] = tr * c + ti * s
        o_ref[1, :, b * width:(b + 1) * width] = ti * c - tr * s


def _fft2_kernel(t_ref, fh_ref, fl_ref, chh_ref, chl_ref, o_ref):
    n = FFT_N
    width = t_ref.shape[3]
    f = (fh_ref[...], fl_ref[...])
    ch = (chh_ref[...], chl_ref[...])
    for b in range(FFT_KB):
        tt = jnp.concatenate([t_ref[0, b], t_ref[1, b]], axis=0)
        z = _dot3(f, _split(tt))
        for g in range(N_GROUPS):
            zz = jnp.concatenate([z[:n, g * GROUP_DIM:(g + 1) * GROUP_DIM],
                                  z[n:, g * GROUP_DIM:(g + 1) * GROUP_DIM]], axis=1)
            y = _dot3(_split(zz), ch)
            o_ref[:, b * width + g * GROUP_DIM:b * width + (g + 1) * GROUP_DIM] = y.astype(o_ref.dtype)


def _fourier_mix(u, tables):
    s, width = u.shape
    n = FFT_N
    stage1, stage2, chan, twc, tws = tables
    twc_b = jnp.broadcast_to(twc[:, :, None], (n, n, LANES))
    tws_b = jnp.broadcast_to(tws[:, :, None], (n, n, LANES))
    t = pl.pallas_call(
        _fft1_kernel,
        grid=(n // FFT_NB,),
        in_specs=[
            pl.BlockSpec((n, FFT_NB * width), lambda j: (0, j)),
            pl.BlockSpec((2 * n, n), lambda j: (0, 0)),
            pl.BlockSpec((2 * n, n), lambda j: (0, 0)),
            pl.BlockSpec((FFT_NB, n, LANES), lambda j: (j, 0, 0)),
            pl.BlockSpec((FFT_NB, n, LANES), lambda j: (j, 0, 0)),
        ],
        out_specs=pl.BlockSpec((2, n, FFT_NB * width), lambda j: (0, 0, j)),
        out_shape=jax.ShapeDtypeStruct((2, n, n * width), F32),
        compiler_params=_cparams(("parallel",)),
        name="fft_stage1",
    )(u.reshape(n, n * width), *stage1, twc_b, tws_b)
    y = pl.pallas_call(
        _fft2_kernel,
        grid=(n // FFT_KB,),
        in_specs=[
            pl.BlockSpec((2, FFT_KB, n, width), lambda i: (0, i, 0, 0)),
            pl.BlockSpec((2 * n, 2 * n), lambda i: (0, 0)),
            pl.BlockSpec((2 * n, 2 * n), lambda i: (0, 0)),
            pl.BlockSpec((2 * n, n), lambda i: (0, 0)),
            pl.BlockSpec((2 * n, n), lambda i: (0, 0)),
        ],
        out_specs=pl.BlockSpec((n, FFT_KB * width), lambda i: (0, i)),
        out_shape=jax.ShapeDtypeStruct((n, n * width), BF16),
        compiler_params=_cparams(("parallel",)),
        name="fft_stage2",
    )(t.reshape(2, n, n, width), *stage2, *chan)
    return y.reshape(s, width)


def _mixout_kernel(x_ref, pg_ref, sh_ref, sc_ref, gt_ref, qg_ref, ao_ref, fy_ref,
                   gwa_ref, gwf_ref, gba_ref, gbf_ref, ap_ref, fp_ref, wo_ref,
                   o_ref, h_scr, acc_scr):
    j = pl.program_id(1)

    @pl.when(j == 0)
    def _():
        xn = _rms(x_ref[...], pg_ref[...], NORM_EPS)
        h_scr[...] = (xn * (1.0 + sc_ref[...]) + sh_ref[...]).astype(BF16)
        acc_scr[...] = jnp.zeros_like(acc_scr)

    h = h_scr[...]
    ga = jax.nn.sigmoid(jnp.dot(h, gwa_ref[...], preferred_element_type=F32) + gba_ref[...])
    gf = jax.nn.sigmoid(jnp.dot(h, gwf_ref[...], preferred_element_type=F32) + gbf_ref[...])
    ya = lax.dot_general(ao_ref[...], ap_ref[...], (((0,), (0,)), ((), ())),
                         preferred_element_type=F32)
    yf = jnp.dot(fy_ref[...], fp_ref[...], preferred_element_type=F32)
    y = (ga * ya + gf * yf).astype(BF16)
    acc_scr[...] += jnp.dot(y, wo_ref[...], preferred_element_type=F32)

    @pl.when(j == pl.num_programs(1) - 1)
    def _():
        yn = _rms(acc_scr[...], qg_ref[...], NORM_EPS)
        o_ref[...] = x_ref[...] + gt_ref[...] * yn


def _mix_out(x, pre_g, shift, scale, gate, post_g, ao, fy, gate_w, gate_b, attn_proj,
             fnet_proj, w_out, layer):
    s, d = x.shape
    nc = d // MIX_TC
    vec = pl.BlockSpec((1, d), lambda i, j: (0, 0))
    return pl.pallas_call(
        _mixout_kernel,
        grid=(s // ROW_TILE, nc),
        in_specs=[
            pl.BlockSpec((ROW_TILE, d), lambda i, j: (i, 0)),
            vec, vec, vec, vec, vec,
            pl.BlockSpec((ATTN_WIDTH, ROW_TILE), lambda i, j: (0, i)),
            pl.BlockSpec((ROW_TILE, FNET_WIDTH), lambda i, j: (i, 0)),
            pl.BlockSpec((None, d, MIX_TC), lambda i, j: (layer, 0, j)),
            pl.BlockSpec((None, d, MIX_TC), lambda i, j: (layer, 0, nc + j)),
            pl.BlockSpec((1, MIX_TC), lambda i, j: (0, j)),
            pl.BlockSpec((1, MIX_TC), lambda i, j: (0, nc + j)),
            pl.BlockSpec((None, ATTN_WIDTH, MIX_TC), lambda i, j: (layer, 0, j)),
            pl.BlockSpec((None, FNET_WIDTH, MIX_TC), lambda i, j: (layer, 0, j)),
            pl.BlockSpec((None, MIX_TC, d), lambda i, j: (layer, j, 0)),
        ],
        out_specs=pl.BlockSpec((ROW_TILE, d), lambda i, j: (i, 0)),
        out_shape=jax.ShapeDtypeStruct((s, d), F32),
        scratch_shapes=[pltpu.VMEM((ROW_TILE, d), BF16), pltpu.VMEM((ROW_TILE, d), F32)],
        compiler_params=_cparams(("parallel", "arbitrary")),
        name="mix_out",
    )(x, pre_g, shift, scale, gate, post_g, ao, fy, gate_w, gate_w, gate_b, gate_b,
      attn_proj, fnet_proj, w_out)


def _rope_tables(seq):
    half = HEAD_DIM // 2
    pos = jnp.arange(seq, dtype=F32)
    inv_freq = ROPE_THETA ** (-jnp.arange(0, HEAD_DIM, 2, dtype=F32) / HEAD_DIM)
    ang = pos[:, None] * inv_freq[None, :]
    cos, sin = jnp.cos(ang), jnp.sin(ang)
    zero = jnp.zeros_like(sin)
    cos_t = jnp.concatenate([cos] * (LANES // half), axis=1)
    sin_n = jnp.concatenate([-sin, zero] * (LANES // HEAD_DIM), axis=1)
    sin_p = jnp.concatenate([zero, sin] * (LANES // HEAD_DIM), axis=1)
    return cos_t, sin_n, sin_p


def kernel(x, c, ada_w, ada_b, pre_norm_g, post_norm_g, ffn1_w_in, ffn1_w_out, mix_w_in,
           lambda_qk, subln_g, attn_proj, fnet_proj, branch_gate_w, branch_gate_b, mix_w_out,
           ffn2_w_in, ffn2_w_out):
    b, s, d = x.shape
    assert b == 1 and s == FFT_N * FFT_N and s % ROW_TILE == 0
    n_layers = ada_w.shape[0]
    cos_t, sin_n, sin_p = _rope_tables(s)
    dft = _dft_tables()
    mod = _ada_mod(c, ada_w, ada_b).reshape(n_layers, 3, 3, 1, d)
    q_scale = HEAD_DIM ** -0.5 * math.log2(math.e)
    xs = x.reshape(s, d)
    ffn1_in, ffn1_out, ffn2_in, ffn2_out, mix_in_w, gate_w, attn_w, fnet_w, mix_out_w = (
        w.astype(BF16) for w in (ffn1_w_in, ffn1_w_out, ffn2_w_in, ffn2_w_out, mix_w_in,
                                 branch_gate_w, attn_proj, fnet_proj, mix_w_out))
    for l in range(n_layers):
        lambda_init = 0.8 - 0.6 * math.exp(-0.3 * l)
        pre = pre_norm_g[l].reshape(3, 1, d)
        post = post_norm_g[l].reshape(3, 1, d)
        xs = _ffn(xs, pre[0], mod[l, 0, 0], mod[l, 0, 1], mod[l, 0, 2], post[0],
                  ffn1_in, ffn1_out, l)
        qt, qn, k, kn, vt, u = _mix_in(xs, pre[1], mod[l, 1, 0], mod[l, 1, 1], mix_in_w, l,
                              cos_t, sin_n, sin_p, q_scale)
        ao = _attention(qt, qn, k, kn, vt, lambda_qk[l], subln_g[l], lambda_init)
        fy = _fourier_mix(u, dft)
        xs = _mix_out(xs, pre[1], mod[l, 1, 0], mod[l, 1, 1], mod[l, 1, 2], post[1], ao, fy,
                      gate_w, branch_gate_b[l].reshape(1, 2 * d), attn_w, fnet_w, mix_out_w, l)
        xs = _ffn(xs, pre[2], mod[l, 2, 0], mod[l, 2, 1], mod[l, 2, 2], post[2],
                  ffn2_in, ffn2_out, l)
    return xs.reshape(b, s, d)
```

```python
import functools
import math

import numpy as np
import jax
import jax.numpy as jnp
from jax import lax
from jax.experimental import pallas as pl
from jax.experimental.pallas import tpu as pltpu

F32 = jnp.float32
BF16 = jnp.bfloat16

N_HEADS = 8
HEAD_DIM = 64
V_DIM = 2 * HEAD_DIM
ACC_ROWS = V_DIM + 8
K_LANES = 2 * V_DIM
ATTN_WIDTH = N_HEADS * V_DIM
N_GROUPS = 8
GROUP_DIM = 128
FNET_WIDTH = N_GROUPS * GROUP_DIM
ROPE_THETA = 10000.0
NORM_EPS = 1e-6
SUBLN_EPS = 1e-5
MACARON_WEIGHT = 0.5

LANES = 128
SUBLANES = 8
NORM_ROWS = 16
MXU_DIM = 256
VMEM_LIMIT_BYTES = 56 * 1024 * 1024

ADA_TN = 1024
ROW_TILE = 512
FFN_TF = 512
MIX_TC = 512
ATTN_TQ = 256
ATTN_TK = ROW_TILE
ATTN_SUB = 2
ATTN_QK = 512
ATTN_TS = 32
SHIFT_LIMIT = 60.0
BOUND_MARGIN = 1.0 + 2.0 ** -7
FFT_N = 128
FFT_NB = 4
FFT_KB = 4


def _cparams(sem):
    return pltpu.CompilerParams(dimension_semantics=sem, vmem_limit_bytes=VMEM_LIMIT_BYTES)


def _modulated_norm(x_ref, h_ref, g_ref, scale_ref, shift_ref):
    gs = g_ref[...] * (1.0 + scale_ref[...])
    shift = shift_ref[...]
    for r in range(0, x_ref.shape[0], NORM_ROWS):
        x = x_ref[r:r + NORM_ROWS, :]
        ms = jnp.mean(x * x, axis=-1, keepdims=True)
        h_ref[r:r + NORM_ROWS, :] = (x * lax.rsqrt(ms + NORM_EPS) * gs + shift).astype(h_ref.dtype)


def _gated_norm_residual(x_ref, y_ref, o_ref, g_ref, gate_ref, weight):
    coef = g_ref[...] * (weight * gate_ref[...])
    for r in range(0, x_ref.shape[0], NORM_ROWS):
        y = y_ref[r:r + NORM_ROWS, :]
        ms = jnp.mean(y * y, axis=-1, keepdims=True)
        o_ref[r:r + NORM_ROWS, :] = x_ref[r:r + NORM_ROWS, :] + y * lax.rsqrt(ms + NORM_EPS) * coef


def _ada_kernel(c_ref, w_ref, b_ref, o_ref):
    d, tn = w_ref.shape

    def body(k, acc):
        r = pl.multiple_of(k * SUBLANES, SUBLANES)
        cb = c_ref[pl.ds(r, SUBLANES), :]
        cb = cb * jax.nn.sigmoid(cb)
        return acc + w_ref[pl.ds(r, SUBLANES), :] * jnp.concatenate([cb] * (tn // LANES), axis=1)

    acc = lax.fori_loop(0, d // SUBLANES, body, jnp.zeros((SUBLANES, tn), F32), unroll=8)
    o_ref[...] = jnp.sum(acc, axis=0, keepdims=True) + b_ref[...]


def _ada_mod(c, ada_w, ada_b):
    n_layers, d, n = ada_w.shape
    c_b = jnp.broadcast_to(c.reshape(d, 1), (d, LANES))
    out = pl.pallas_call(
        _ada_kernel,
        grid=(n_layers, n // ADA_TN),
        in_specs=[
            pl.BlockSpec((d, LANES), lambda l, j: (0, 0)),
            pl.BlockSpec((None, d, ADA_TN), lambda l, j: (l, 0, j)),
            pl.BlockSpec((None, 1, ADA_TN), lambda l, j: (l, 0, j)),
        ],
        out_specs=pl.BlockSpec((None, 1, ADA_TN), lambda l, j: (l, 0, j)),
        out_shape=jax.ShapeDtypeStruct((n_layers, 1, n), F32),
        compiler_params=_cparams(("parallel", "parallel")),
        name="ada_mod",
    )(c_b, ada_w, ada_b.reshape(n_layers, 1, n))
    return out


def _ffn_kernel(x_ref, pg_ref, sh_ref, sc_ref, gt_ref, qg_ref, wg_ref, wu_ref, wo_ref,
                o_ref, h_scr, acc_scr):
    j = pl.program_id(1)

    @pl.when(j == 0)
    def _():
        _modulated_norm(x_ref, h_scr, pg_ref, sc_ref, sh_ref)
        acc_scr[...] = jnp.zeros_like(acc_scr)

    h = h_scr[...]
    g = jnp.dot(h, wg_ref[...], preferred_element_type=F32)
    u = jnp.dot(h, wu_ref[...], preferred_element_type=F32)
    a = (g * jax.nn.sigmoid(g) * u).astype(BF16)
    acc_scr[...] += jnp.dot(a, wo_ref[...], preferred_element_type=F32)

    @pl.when(j == pl.num_programs(1) - 1)
    def _():
        _gated_norm_residual(x_ref, acc_scr, o_ref, qg_ref, gt_ref, MACARON_WEIGHT)


def _ffn(x, pre_g, shift, scale, gate, post_g, w_in, w_out, layer):
    s, d = x.shape
    f = w_out.shape[1]
    nf = f // FFN_TF
    vec = pl.BlockSpec((1, d), lambda i, j: (0, 0))
    return pl.pallas_call(
        _ffn_kernel,
        grid=(s // ROW_TILE, nf),
        in_specs=[
            pl.BlockSpec((ROW_TILE, d), lambda i, j: (i, 0)),
            vec, vec, vec, vec, vec,
            pl.BlockSpec((None, d, FFN_TF), lambda i, j: (layer, 0, j)),
            pl.BlockSpec((None, d, FFN_TF), lambda i, j: (layer, 0, nf + j)),
            pl.BlockSpec((None, FFN_TF, d), lambda i, j: (layer, j, 0)),
        ],
        out_specs=pl.BlockSpec((ROW_TILE, d), lambda i, j: (i, 0)),
        out_shape=jax.ShapeDtypeStruct((s, d), F32),
        scratch_shapes=[pltpu.VMEM((ROW_TILE, d), BF16), pltpu.VMEM((ROW_TILE, d), F32)],
        compiler_params=_cparams(("parallel", "arbitrary")),
        name="ffn",
    )(x, pre_g, shift, scale, gate, post_g, w_in, w_in, w_out)


def _mixin_kernel(x_ref, pg_ref, sh_ref, sc_ref, w_ref, cos_ref, sinn_ref, sinp_ref,
                  qt_ref, qn_ref, k_ref, kn_ref, vt_ref, u_ref, h_scr, *, q_scale):
    _modulated_norm(x_ref, h_scr, pg_ref, sc_ref, sh_ref)
    h = h_scr[...]
    cos_t, sin_n, sin_p = cos_ref[...], sinn_ref[...], sinp_ref[...]

    def rope(zs):
        return (zs * cos_t + pltpu.roll(zs, LANES - HEAD_DIM // 2, 1) * sin_n
                + pltpu.roll(zs, HEAD_DIM // 2, 1) * sin_p)

    lane = lax.broadcasted_iota(jnp.int32, (x_ref.shape[0], LANES), 1)

    def max_sq_norms(t):
        t2 = t.astype(F32) * t.astype(F32)
        n1 = jnp.max(jnp.sum(jnp.where(lane < HEAD_DIM, t2, 0.0), axis=1, keepdims=True),
                     axis=0, keepdims=True)
        n2 = jnp.max(jnp.sum(jnp.where(lane < HEAD_DIM, 0.0, t2), axis=1, keepdims=True),
                     axis=0, keepdims=True)
        return jnp.where(lane[:1] < HEAD_DIM, n1, n2)

    zq = jnp.dot(h, w_ref[:, 0:ATTN_WIDTH], preferred_element_type=F32)
    for hd in range(N_HEADS):
        qr = (rope(zq[:, hd * V_DIM:(hd + 1) * V_DIM]) * q_scale).astype(BF16)
        qt_ref[hd] = qr.astype(F32).T.astype(BF16)
        qn_ref[hd:hd + 1, :] = max_sq_norms(qr)
    zk = jnp.dot(h, w_ref[:, ATTN_WIDTH:2 * ATTN_WIDTH], preferred_element_type=F32)
    shift_lanes = jnp.where(lane == 0, 1.0, 0.0).astype(BF16)
    for hd in range(N_HEADS):
        kr = rope(zk[:, hd * V_DIM:(hd + 1) * V_DIM]).astype(BF16)
        k_ref[hd, :, :V_DIM] = kr
        k_ref[hd, :, V_DIM:] = shift_lanes
        kn_ref[hd:hd + 1, :] = max_sq_norms(kr)
    zv = jnp.dot(h, w_ref[:, 2 * ATTN_WIDTH:3 * ATTN_WIDTH], preferred_element_type=F32)
    zvt = zv.T
    for hd in range(N_HEADS):
        vt_ref[hd] = zvt[hd * V_DIM:(hd + 1) * V_DIM, :].astype(BF16)
    u_ref[...] = jnp.dot(h, w_ref[:, 3 * ATTN_WIDTH:], preferred_element_type=F32)


def _mix_in(x, pre_g, shift, scale, w, layer, cos_t, sin_n, sin_p, q_scale):
    s, d = x.shape
    n_tiles = s // ROW_TILE
    vec = pl.BlockSpec((1, d), lambda i: (0, 0))
    tab = pl.BlockSpec((ROW_TILE, LANES), lambda i: (i, 0))
    return pl.pallas_call(
        functools.partial(_mixin_kernel, q_scale=q_scale),
        grid=(n_tiles,),
        in_specs=[
            pl.BlockSpec((ROW_TILE, d), lambda i: (i, 0)),
            vec, vec, vec,
            pl.BlockSpec((None,) + w.shape[1:], lambda i: (layer, 0, 0), pipeline_mode=pl.Buffered(1)),
            tab, tab, tab,
        ],
        out_specs=[
            pl.BlockSpec((N_HEADS, None, V_DIM, ROW_TILE), lambda i: (0, i, 0, 0)),
            pl.BlockSpec((None, N_HEADS, LANES), lambda i: (i, 0, 0)),
            pl.BlockSpec((N_HEADS, ROW_TILE, K_LANES), lambda i: (0, i, 0)),
            pl.BlockSpec((None, N_HEADS, LANES), lambda i: (i, 0, 0)),
            pl.BlockSpec((N_HEADS, None, V_DIM, ROW_TILE), lambda i: (0, i, 0, 0)),
            pl.BlockSpec((ROW_TILE, FNET_WIDTH), lambda i: (i, 0)),
        ],
        out_shape=[
            jax.ShapeDtypeStruct((N_HEADS, n_tiles, V_DIM, ROW_TILE), BF16),
            jax.ShapeDtypeStruct((n_tiles, N_HEADS, LANES), F32),
            jax.ShapeDtypeStruct((N_HEADS, s, K_LANES), BF16),
            jax.ShapeDtypeStruct((n_tiles, N_HEADS, LANES), F32),
            jax.ShapeDtypeStruct((N_HEADS, n_tiles, V_DIM, ROW_TILE), BF16),
            jax.ShapeDtypeStruct((s, FNET_WIDTH), F32),
        ],
        scratch_shapes=[pltpu.VMEM((ROW_TILE, d), BF16)],
        compiler_params=_cparams(("parallel",)),
        name="mix_in",
    )(x, pre_g, shift, scale, w, cos_t, sin_n, sin_p)


def _sublane_partial_sum(p):
    return p.reshape(p.shape[0] // SUBLANES, SUBLANES, p.shape[1]).sum(axis=0)


def _attn_kernel(qb_ref, kb_ref, lq_ref, g_ref, qt_ref, k_ref, vt_ref, o_ref,
                 qz_scr, s0_scr, s1_scr, c0_scr, c1_scr, acc_scr, m_scr, *, lambda_init):
    tq = qt_ref.shape[1]
    n_kv, _, tk = vt_ref.shape
    n_col = 2 * tq // MXU_DIM
    hd = pl.program_id(0)
    tile = lax.div(pl.program_id(1), ROW_TILE // tq)
    bsq = [qb_ref[tile, 2 * hd + m] * kb_ref[0, 2 * hd + m] for m in range(2)]
    fast = jnp.maximum(bsq[0], bsq[1]) <= (SHIFT_LIMIT / BOUND_MARGIN) ** 2
    shift = jnp.sqrt(jnp.concatenate(
        [jnp.full((1, tq), jnp.where(fast, b, 0.0), F32) for b in bsq], axis=1)) * BOUND_MARGIN
    qt = qt_ref[...].astype(F32)
    row = lax.broadcasted_iota(jnp.int32, (V_DIM, 2 * tq), 0)
    qz_scr[:V_DIM, :] = jnp.where((row < HEAD_DIM) == (lax.broadcasted_iota(
        jnp.int32, (V_DIM, 2 * tq), 1) < tq), jnp.concatenate([qt, qt], axis=1), 0.0).astype(BF16)
    qz_scr[V_DIM:, :] = jnp.where(row == 0, -shift, 0.0).astype(BF16)
    acc_scr[...] = jnp.zeros_like(acc_scr)

    @pl.when(fast)
    def _():
        span = ATTN_TS * ATTN_QK
        last = n_kv * tk - ATTN_QK

        def shifted_scores(r):
            return jnp.dot(k_ref[pl.ds(pl.multiple_of(r, ATTN_QK), ATTN_QK), :], qz_scr[...],
                           preferred_element_type=F32)

        s0_scr[:ATTN_QK, :] = shifted_scores(0)

        def body(t, carry):
            r0 = t * span
            pv = [None] * n_col
            psum = [jnp.zeros((SUBLANES, MXU_DIM), F32)] * n_col
            s_next = None
            for kb in range(ATTN_TS):
                s = s0_scr[:ATTN_QK, :] if kb == 0 else s_next
                s_next = shifted_scores(jnp.minimum(r0 + (kb + 1) * ATTN_QK, last))
                for sub in range(ATTN_QK // MXU_DIM):
                    blk, off = divmod(kb * ATTN_QK + sub * MXU_DIM, tk)
                    rows = slice(sub * MXU_DIM, (sub + 1) * MXU_DIM)
                    for nh in range(n_col):
                        p = jnp.exp2(s[rows, nh * MXU_DIM:(nh + 1) * MXU_DIM])
                        psum[nh] = psum[nh] + _sublane_partial_sum(p)
                        d = jnp.dot(vt_ref[t * (span // tk) + blk, :, off:off + MXU_DIM],
                                    p.astype(BF16), preferred_element_type=F32)
                        pv[nh] = d if pv[nh] is None else pv[nh] + d
            s0_scr[:ATTN_QK, :] = s_next
            for nh in range(n_col):
                cols = slice(nh * MXU_DIM, (nh + 1) * MXU_DIM)
                acc_scr[:V_DIM, cols] = acc_scr[:V_DIM, cols] + pv[nh]
                acc_scr[V_DIM:, cols] = acc_scr[V_DIM:, cols] + psum[nh]
            return carry

        lax.fori_loop(0, n_kv * tk // span, body, 0)

    @pl.when(jnp.logical_not(fast))
    def _():
        n_chunks = n_kv // ATTN_SUB
        ck = ATTN_SUB * tk
        m_scr[...] = jnp.full_like(m_scr, -jnp.inf)

        def scores(j, s_scr, c_scr):
            r = pl.multiple_of(j * ck, ck)
            s = jnp.dot(k_ref[pl.ds(r, ck), :], qz_scr[...],
                        preferred_element_type=F32)
            s_scr[...] = s
            c_scr[...] = jnp.max(s, axis=0, keepdims=True)

        def consume(j, s_scr, c_scr):
            m_prev = m_scr[...]
            m_new = jnp.maximum(m_prev, c_scr[...])
            alpha = jnp.exp2(m_prev - m_new)
            for nh in range(n_col):
                cols = slice(nh * MXU_DIM, (nh + 1) * MXU_DIM)
                pv = None
                psum = jnp.zeros((SUBLANES, MXU_DIM), F32)
                for kb in range(ck // MXU_DIM):
                    p = jnp.exp2(s_scr[kb * MXU_DIM:(kb + 1) * MXU_DIM, cols] - m_new[:, cols])
                    psum = psum + _sublane_partial_sum(p)
                    blk, off = divmod(kb * MXU_DIM, tk)
                    d = jnp.dot(vt_ref[j * ATTN_SUB + blk, :, off:off + MXU_DIM], p.astype(BF16),
                                preferred_element_type=F32)
                    pv = d if pv is None else pv + d
                acc_scr[:V_DIM, cols] = acc_scr[:V_DIM, cols] * alpha[:, cols] + pv
                acc_scr[V_DIM:, cols] = acc_scr[V_DIM:, cols] * alpha[:, cols] + psum
            m_scr[...] = m_new

        scores(0, s0_scr, c0_scr)

        def body(i, carry):
            j = 2 * i
            scores(j + 1, s1_scr, c1_scr)
            consume(j, s0_scr, c0_scr)
            scores(jnp.minimum(j + 2, n_chunks - 1), s0_scr, c0_scr)
            consume(j + 1, s1_scr, c1_scr)
            return carry

        lax.fori_loop(0, n_chunks // 2, body, 0)

    lq = lq_ref[...]
    lam = (jnp.exp(jnp.sum(lq[0:1] * lq[1:2], axis=-1, keepdims=True))
           - jnp.exp(jnp.sum(lq[2:3] * lq[3:4], axis=-1, keepdims=True)) + lambda_init)
    on = acc_scr[:V_DIM, :] / jnp.sum(acc_scr[V_DIM:, :], axis=0, keepdims=True)
    ot = on[:, :tq] - lam * on[:, tq:]
    ms = jnp.mean(ot * ot, axis=0, keepdims=True)
    o_ref[...] = (ot * lax.rsqrt(ms + SUBLN_EPS) * (g_ref[...] * (1.0 - lambda_init))).astype(BF16)


def _attention(qt, qn, k, kn, vt, lambda_qk, subln_g, lambda_init):
    n_heads, n_tiles, _, _ = qt.shape
    s = k.shape[1]
    n_kv = vt.shape[1]
    kmax = jnp.max(kn, axis=0)
    kb = jnp.stack([kmax[:, 0], kmax[:, HEAD_DIM]], axis=1).reshape(1, 2 * n_heads)
    qb = jnp.stack([qn[:, :, 0], qn[:, :, HEAD_DIM]], axis=2).reshape(n_tiles, 2 * n_heads)
    per_tile = ROW_TILE // ATTN_TQ
    return pl.pallas_call(
        functools.partial(_attn_kernel, lambda_init=lambda_init),
        grid=(n_heads, s // ATTN_TQ),
        in_specs=[
            pl.BlockSpec(memory_space=pltpu.SMEM),
            pl.BlockSpec(memory_space=pltpu.SMEM),
            pl.BlockSpec(lambda_qk.shape, lambda h, i: (0, 0)),
            pl.BlockSpec((V_DIM, 1), lambda h, i: (0, 0)),
            pl.BlockSpec((None, None, V_DIM, ATTN_TQ),
                         lambda h, i: (h, i // per_tile, 0, i % per_tile)),
            pl.BlockSpec((None, s, K_LANES), lambda h, i: (h, 0, 0)),
            pl.BlockSpec((None, n_kv, V_DIM, ATTN_TK), lambda h, i: (h, 0, 0, 0)),
        ],
        out_specs=pl.BlockSpec((V_DIM, ATTN_TQ), lambda h, i: (h, i)),
        out_shape=jax.ShapeDtypeStruct((n_heads * V_DIM, s), BF16),
        scratch_shapes=[pltpu.VMEM((K_LANES, 2 * ATTN_TQ), BF16),
                        pltpu.VMEM((ATTN_SUB * ATTN_TK, 2 * ATTN_TQ), F32),
                        pltpu.VMEM((ATTN_SUB * ATTN_TK, 2 * ATTN_TQ), F32),
                        pltpu.VMEM((1, 2 * ATTN_TQ), F32),
                        pltpu.VMEM((1, 2 * ATTN_TQ), F32),
                        pltpu.VMEM((ACC_ROWS, 2 * ATTN_TQ), F32),
                        pltpu.VMEM((1, 2 * ATTN_TQ), F32)],
        compiler_params=_cparams(("parallel", "parallel")),
        name="diff_attn",
    )(qb, kb, lambda_qk, subln_g.reshape(V_DIM, 1), qt, k, vt)


def _dft_tables():
    n = FFT_N
    jk = np.outer(np.arange(n), np.arange(n)) % n
    ang = 2.0 * np.pi * jk / n
    c = np.cos(ang) / math.sqrt(n)
    s = np.sin(ang) / math.sqrt(n)
    stage1 = np.concatenate([c, -s], axis=0)
    stage2 = np.block([[c, s], [-s, c]])
    chan = np.concatenate([c, s], axis=0)
    tw_ang = 2.0 * np.pi * np.outer(np.arange(n), np.arange(n)) / (n * n)
    return (_split(jnp.asarray(stage1, F32)), _split(jnp.asarray(stage2, F32)),
            _split(jnp.asarray(chan, F32)),
            jnp.asarray(np.cos(tw_ang), F32), jnp.asarray(np.sin(tw_ang), F32))


def _split(a):
    hi = a.astype(BF16)
    return hi, (a - hi.astype(F32)).astype(BF16)


def _dot3(a, b):
    (a_hi, a_lo), (b_hi, b_lo) = a, b
    return (jnp.dot(a_hi, b_hi, preferred_element_type=F32)
            + jnp.dot(a_hi, b_lo, preferred_element_type=F32)
            + jnp.dot(a_lo, b_hi, preferred_element_type=F32))


def _fft1_kernel(x_ref, fh_ref, fl_ref, twc_ref, tws_ref, o_ref):
    n = FFT_N
    t = _dot3((fh_ref[...], fl_ref[...]), _split(x_ref[...]))
    width = x_ref.shape[1] // FFT_NB
    for b in range(FFT_NB):
        tr = t[:n, b * width:(b + 1) * width]
        ti = t[n:, b * width:(b + 1) * width]
        c = jnp.concatenate([twc_ref[b]] * (width // LANES), axis=1)
        s = jnp.concatenate([tws_ref[b]] * (width // LANES), axis=1)
        o_ref[0, :, b * width:(b + 1) * width] = tr * c + ti * s
        o_ref[1, :, b * width:(b + 1) * width] = ti * c - tr * s


def _fft2_kernel(t_ref, fh_ref, fl_ref, chh_ref, chl_ref, o_ref):
    n = FFT_N
    width = t_ref.shape[3]
    f = (fh_ref[...], fl_ref[...])
    ch = (chh_ref[...], chl_ref[...])
    for b in range(FFT_KB):
        tt = jnp.concatenate([t_ref[0, b], t_ref[1, b]], axis=0)
        z = _dot3(f, _split(tt))
        for g in range(N_GROUPS):
            zz = jnp.concatenate([z[:n, g * GROUP_DIM:(g + 1) * GROUP_DIM],
                                  z[n:, g * GROUP_DIM:(g + 1) * GROUP_DIM]], axis=1)
            y = _dot3(_split(zz), ch)
            o_ref[:, b * width + g * GROUP_DIM:b * width + (g + 1) * GROUP_DIM] = y.astype(o_ref.dtype)


def _fourier_mix(u, tables):
    s, width = u.shape
    n = FFT_N
    stage1, stage2, chan, twc, tws = tables
    twc_b = jnp.broadcast_to(twc[:, :, None], (n, n, LANES))
    tws_b = jnp.broadcast_to(tws[:, :, None], (n, n, LANES))
    t = pl.pallas_call(
        _fft1_kernel,
        grid=(n // FFT_NB,),
        in_specs=[
            pl.BlockSpec((n, FFT_NB * width), lambda j: (0, j)),
            pl.BlockSpec((2 * n, n), lambda j: (0, 0)),
            pl.BlockSpec((2 * n, n), lambda j: (0, 0)),
            pl.BlockSpec((FFT_NB, n, LANES), lambda j: (j, 0, 0)),
            pl.BlockSpec((FFT_NB, n, LANES), lambda j: (j, 0, 0)),
        ],
        out_specs=pl.BlockSpec((2, n, FFT_NB * width), lambda j: (0, 0, j)),
        out_shape=jax.ShapeDtypeStruct((2, n, n * width), F32),
        compiler_params=_cparams(("parallel",)),
        name="fft_stage1",
    )(u.reshape(n, n * width), *stage1, twc_b, tws_b)
    y = pl.pallas_call(
        _fft2_kernel,
        grid=(n // FFT_KB,),
        in_specs=[
            pl.BlockSpec((2, FFT_KB, n, width), lambda i: (0, i, 0, 0)),
            pl.BlockSpec((2 * n, 2 * n), lambda i: (0, 0)),
            pl.BlockSpec((2 * n, 2 * n), lambda i: (0, 0)),
            pl.BlockSpec((2 * n, n), lambda i: (0, 0)),
            pl.BlockSpec((2 * n, n), lambda i: (0, 0)),
        ],
        out_specs=pl.BlockSpec((n, FFT_KB * width), lambda i: (0, i)),
        out_shape=jax.ShapeDtypeStruct((n, n * width), BF16),
        compiler_params=_cparams(("parallel",)),
        name="fft_stage2",
    )(t.reshape(2, n, n, width), *stage2, *chan)
    return y.reshape(s, width)


def _mixout_kernel(x_ref, pg_ref, sh_ref, sc_ref, gt_ref, qg_ref, ao_ref, fy_ref,
                   gwa_ref, gwf_ref, gba_ref, gbf_ref, ap_ref, fp_ref, wo_ref,
                   o_ref, h_scr, acc_scr):
    j = pl.program_id(1)

    @pl.when(j == 0)
    def _():
        _modulated_norm(x_ref, h_scr, pg_ref, sc_ref, sh_ref)
        acc_scr[...] = jnp.zeros_like(acc_scr)

    h = h_scr[...]
    ga =jax.nn.sigmoid(jnp.dot(h, gwa_ref[...], preferred_element_type=F32) + gba_ref[...])
    gf = jax.nn.sigmoid(jnp.dot(h, gwf_ref[...], preferred_element_type=F32) + gbf_ref[...])
    ya = lax.dot_general(ao_ref[...], ap_ref[...], (((0,), (0,)), ((), ())),
                         preferred_element_type=F32)
    yf = jnp.dot(fy_ref[...], fp_ref[...], preferred_element_type=F32)
    y = (ga * ya + gf * yf).astype(BF16)
    acc_scr[...] += jnp.dot(y, wo_ref[...], preferred_element_type=F32)

    @pl.when(j == pl.num_programs(1) - 1)
    def _():
        _gated_norm_residual(x_ref, acc_scr, o_ref, qg_ref, gt_ref, 1.0)


def _mix_out(x, pre_g, shift, scale, gate, post_g, ao, fy, gate_w, gate_b, attn_proj,
             fnet_proj, w_out, layer):
    s, d = x.shape
    nc = d // MIX_TC
    vec = pl.BlockSpec((1, d), lambda i, j: (0, 0))
    return pl.pallas_call(
        _mixout_kernel,
        grid=(s // ROW_TILE, nc),
        in_specs=[
            pl.BlockSpec((ROW_TILE, d), lambda i, j: (i, 0)),
            vec, vec, vec, vec, vec,
            pl.BlockSpec((ATTN_WIDTH, ROW_TILE), lambda i, j: (0, i)),
            pl.BlockSpec((ROW_TILE, FNET_WIDTH), lambda i, j: (i, 0)),
            pl.BlockSpec((None, d, MIX_TC), lambda i, j: (layer, 0, j)),
            pl.BlockSpec((None, d, MIX_TC), lambda i, j: (layer, 0, nc + j)),
            pl.BlockSpec((1, MIX_TC), lambda i, j: (0, j)),
            pl.BlockSpec((1, MIX_TC), lambda i, j: (0, nc + j)),
            pl.BlockSpec((None, ATTN_WIDTH, MIX_TC), lambda i, j: (layer, 0, j)),
            pl.BlockSpec((None, FNET_WIDTH, MIX_TC), lambda i, j: (layer, 0, j)),
            pl.BlockSpec((None, MIX_TC, d), lambda i, j: (layer, j, 0)),
        ],
        out_specs=pl.BlockSpec((ROW_TILE, d), lambda i, j: (i, 0)),
        out_shape=jax.ShapeDtypeStruct((s, d), F32),
        scratch_shapes=[pltpu.VMEM((ROW_TILE, d), BF16), pltpu.VMEM((ROW_TILE, d), F32)],
        compiler_params=_cparams(("parallel", "arbitrary")),
        name="mix_out",
    )(x, pre_g, shift, scale, gate, post_g, ao, fy, gate_w, gate_w, gate_b, gate_b,
      attn_proj, fnet_proj, w_out)


def _rope_tables(seq):
    half = HEAD_DIM // 2
    pos = jnp.arange(seq, dtype=F32)
    inv_freq = ROPE_THETA ** (-jnp.arange(0, HEAD_DIM, 2, dtype=F32) / HEAD_DIM)
    ang = pos[:, None] * inv_freq[None, :]
    cos, sin = jnp.cos(ang), jnp.sin(ang)
    zero = jnp.zeros_like(sin)
    cos_t = jnp.concatenate([cos] * (LANES // half), axis=1)
    sin_n = jnp.concatenate([-sin, zero] * (LANES // HEAD_DIM), axis=1)
    sin_p = jnp.concatenate([zero, sin] * (LANES // HEAD_DIM), axis=1)
    return cos_t, sin_n, sin_p


def kernel(x, c, ada_w, ada_b, pre_norm_g, post_norm_g, ffn1_w_in, ffn1_w_out, mix_w_in,
           lambda_qk, subln_g, attn_proj, fnet_proj, branch_gate_w, branch_gate_b, mix_w_out,
           ffn2_w_in, ffn2_w_out):
    b, s, d = x.shape
    assert b == 1 and s == FFT_N * FFT_N and s % ROW_TILE == 0
    n_layers = ada_w.shape[0]
    cos_t, sin_n, sin_p = _rope_tables(s)
    dft = _dft_tables()
    mod = _ada_mod(c, ada_w, ada_b).reshape(n_layers, 3, 3, 1, d)
    q_scale = HEAD_DIM ** -0.5 * math.log2(math.e)
    xs = x.reshape(s, d)
    ffn1_in, ffn1_out, ffn2_in, ffn2_out, mix_in_w, gate_w, attn_w, fnet_w, mix_out_w = (
        w.astype(BF16) for w in (ffn1_w_in, ffn1_w_out, ffn2_w_in, ffn2_w_out, mix_w_in,
                                 branch_gate_w, attn_proj, fnet_proj, mix_w_out))
    for l in range(n_layers):
        lambda_init = 0.8 - 0.6 * math.exp(-0.3 * l)
        pre = pre_norm_g[l].reshape(3, 1, d)
        post = post_norm_g[l].reshape(3, 1, d)
        xs = _ffn(xs, pre[0], mod[l, 0, 0], mod[l, 0, 1], mod[l, 0, 2], post[0],
                  ffn1_in, ffn1_out, l)
        qt, qn, k, kn, vt, u = _mix_in(xs, pre[1], mod[l, 1, 0], mod[l, 1, 1], mix_in_w, l,
                              cos_t, sin_n, sin_p, q_scale)
        ao = _attention(qt, qn, k, kn, vt, lambda_qk[l], subln_g[l], lambda_init)
        fy = _fourier_mix(u, dft)
        xs = _mix_out(xs, pre[1], mod[l, 1, 0], mod[l, 1, 1], mod[l, 1, 2], post[1], ao, fy,
                      gate_w, branch_gate_b[l].reshape(1, 2 * d), attn_w, fnet_w, mix_out_w, l)
        xs = _ffn(xs, pre[2], mod[l, 2, 0], mod[l, 2, 1], mod[l, 2, 2], post[2],
                  ffn2_in, ffn2_out, l)
    return xs.reshape(b, s, d)
```

```python
import functools
import math

import numpy as np
import jax
import jax.numpy as jnp
from jax import lax
from jax.experimental import pallas as pl
from jax.experimental.pallas import tpu as pltpu

F32 = jnp.float32
BF16 = jnp.bfloat16

N_HEADS = 8
HEAD_DIM = 64
V_DIM = 2 * HEAD_DIM
ACC_ROWS = V_DIM + 8
K_LANES = 2 * V_DIM
ATTN_WIDTH = N_HEADS * V_DIM
N_GROUPS = 8
GROUP_DIM = 128
FNET_WIDTH = N_GROUPS * GROUP_DIM
ROPE_THETA = 10000.0
NORM_EPS = 1e-6
SUBLN_EPS = 1e-5
MACARON_WEIGHT = 0.5

LANES = 128
SUBLANES = 8
NORM_ROWS = 16
MXU_DIM = 256
VMEM_LIMIT_BYTES = 56 * 1024 * 1024

ADA_TN = 1024
ROW_TILE = 512
FFN_ROWS = 1024
FFN_TF = 256
MIX_TC = 512
ATTN_TQ = 256
ATTN_TK = ROW_TILE
ATTN_SUB = 2
ATTN_QK = 512
ATTN_TS = 32
SHIFT_LIMIT = 60.0
BOUND_MARGIN = 1.0 + 2.0 ** -7
FFT_N = 128
FFT_NB = 4
FFT_KB = 4


def _cparams(sem):
    return pltpu.CompilerParams(dimension_semantics=sem, vmem_limit_bytes=VMEM_LIMIT_BYTES)


def _modulated_norm(x_ref, h_ref, g_ref, scale_ref, shift_ref):
    gs = g_ref[...] * (1.0 + scale_ref[...])
    shift = shift_ref[...]
    for r in range(0, x_ref.shape[0], NORM_ROWS):
        x = x_ref[r:r + NORM_ROWS, :]
        ms = jnp.mean(x * x, axis=-1, keepdims=True)
        h_ref[r:r + NORM_ROWS, :] = (x * lax.rsqrt(ms + NORM_EPS) * gs + shift).astype(h_ref.dtype)


def _gated_norm_residual(x_ref, y_ref, o_ref, g_ref, gate_ref, weight):
    coef = g_ref[...] * (weight * gate_ref[...])
    for r in range(0, x_ref.shape[0], NORM_ROWS):
        y = y_ref[r:r + NORM_ROWS, :]
        ms = jnp.mean(y * y, axis=-1, keepdims=True)
        o_ref[r:r + NORM_ROWS, :] = x_ref[r:r + NORM_ROWS, :] + y * lax.rsqrt(ms + NORM_EPS) * coef


def _ada_kernel(c_ref, w_ref, b_ref, o_ref):
    d, tn = w_ref.shape

    def body(k, acc):
        r = pl.multiple_of(k * SUBLANES, SUBLANES)
        cb = c_ref[pl.ds(r, SUBLANES), :]
        cb = cb * jax.nn.sigmoid(cb)
        return acc + w_ref[pl.ds(r, SUBLANES), :] * jnp.concatenate([cb] * (tn // LANES), axis=1)

    acc = lax.fori_loop(0, d // SUBLANES, body, jnp.zeros((SUBLANES, tn), F32), unroll=8)
    o_ref[...] = jnp.sum(acc, axis=0, keepdims=True) + b_ref[...]


def _ada_mod(c, ada_w, ada_b):
    n_layers, d, n = ada_w.shape
    c_b = jnp.broadcast_to(c.reshape(d, 1), (d, LANES))
    out = pl.pallas_call(
        _ada_kernel,
        grid=(n_layers, n // ADA_TN),
        in_specs=[
            pl.BlockSpec((d, LANES), lambda l, j: (0, 0)),
            pl.BlockSpec((None, d, ADA_TN), lambda l, j: (l, 0, j)),
            pl.BlockSpec((None, 1, ADA_TN), lambda l, j: (l, 0, j)),
        ],
        out_specs=pl.BlockSpec((None, 1, ADA_TN), lambda l, j: (l, 0, j)),
        out_shape=jax.ShapeDtypeStruct((n_layers, 1, n), F32),
        compiler_params=_cparams(("parallel", "parallel")),
        name="ada_mod",
    )(c_b, ada_w, ada_b.reshape(n_layers, 1, n))
    return out


def _ffn_kernel(x_ref, pg_ref, sh_ref, sc_ref, gt_ref, qg_ref, wg_ref, wu_ref, wo_ref,
                o_ref, h_scr, acc_scr):
    j = pl.program_id(1)

    @pl.when(j == 0)
    def _():
        _modulated_norm(x_ref, h_scr, pg_ref, sc_ref, sh_ref)
        acc_scr[...] = jnp.zeros_like(acc_scr)

    h = h_scr[...]
    g = jnp.dot(h, wg_ref[...], preferred_element_type=F32)
    u = jnp.dot(h, wu_ref[...], preferred_element_type=F32)
    a = (g * jax.nn.sigmoid(g) * u).astype(BF16)
    acc_scr[...] += jnp.dot(a, wo_ref[...], preferred_element_type=F32)

    @pl.when(j == pl.num_programs(1) - 1)
    def _():
        _gated_norm_residual(x_ref, acc_scr, o_ref, qg_ref, gt_ref, MACARON_WEIGHT)


def _ffn(x, pre_g, shift, scale, gate, post_g, w_in, w_out, layer):
    s, d = x.shape
    f = w_out.shape[1]
    nf = f // FFN_TF
    vec = pl.BlockSpec((1, d), lambda i, j: (0, 0))
    return pl.pallas_call(
        _ffn_kernel,
        grid=(s // FFN_ROWS, nf),
        in_specs=[
            pl.BlockSpec((FFN_ROWS, d), lambda i, j: (i, 0)),
            vec, vec, vec, vec, vec,
            pl.BlockSpec((None, d, FFN_TF), lambda i, j: (layer, 0, j)),
            pl.BlockSpec((None, d, FFN_TF), lambda i, j: (layer, 0, nf + j)),
            pl.BlockSpec((None, FFN_TF, d), lambda i, j: (layer, j, 0)),
        ],
        out_specs=pl.BlockSpec((FFN_ROWS, d), lambda i, j: (i, 0)),
        out_shape=jax.ShapeDtypeStruct((s, d), F32),
        scratch_shapes=[pltpu.VMEM((FFN_ROWS, d), BF16), pltpu.VMEM((FFN_ROWS, d), F32)],
        compiler_params=_cparams(("parallel", "arbitrary")),
        name="ffn",
    )(x, pre_g, shift, scale, gate, post_g, w_in, w_in, w_out)


def _mixin_kernel(x_ref, pg_ref, sh_ref, sc_ref, w_ref, cos_ref, sinn_ref, sinp_ref,
                  qt_ref, qn_ref, k_ref, kn_ref, vt_ref, u_ref, h_scr, *, q_scale):
    _modulated_norm(x_ref, h_scr, pg_ref, sc_ref, sh_ref)
    h = h_scr[...]
    cos_t, sin_n, sin_p = cos_ref[...], sinn_ref[...], sinp_ref[...]

    def rope(zs):
        return (zs * cos_t + pltpu.roll(zs, LANES - HEAD_DIM // 2, 1) * sin_n
                + pltpu.roll(zs, HEAD_DIM // 2, 1) * sin_p)

    lane = lax.broadcasted_iota(jnp.int32, (x_ref.shape[0], LANES), 1)

    def max_sq_norms(t):
        t2 = t.astype(F32) * t.astype(F32)
        n1 = jnp.max(jnp.sum(jnp.where(lane < HEAD_DIM, t2, 0.0), axis=1, keepdims=True),
                     axis=0, keepdims=True)
        n2 = jnp.max(jnp.sum(jnp.where(lane < HEAD_DIM, 0.0, t2), axis=1, keepdims=True),
                     axis=0, keepdims=True)
        return jnp.where(lane[:1] < HEAD_DIM, n1, n2)

    zq = jnp.dot(h, w_ref[:, 0:ATTN_WIDTH], preferred_element_type=F32)
    for hd in range(N_HEADS):
        qr = (rope(zq[:, hd * V_DIM:(hd + 1) * V_DIM]) * q_scale).astype(BF16)
        qt_ref[hd] = qr.astype(F32).T.astype(BF16)
        qn_ref[hd:hd + 1, :] = max_sq_norms(qr)
    zk = jnp.dot(h, w_ref[:, ATTN_WIDTH:2 * ATTN_WIDTH], preferred_element_type=F32)
    shift_lanes = jnp.where(lane == 0, 1.0, 0.0).astype(BF16)
    for hd in range(N_HEADS):
        kr = rope(zk[:, hd * V_DIM:(hd + 1) * V_DIM]).astype(BF16)
        k_ref[hd, :, :V_DIM] = kr
        k_ref[hd, :, V_DIM:] = shift_lanes
        kn_ref[hd:hd + 1, :] = max_sq_norms(kr)
    zv = jnp.dot(h, w_ref[:, 2 * ATTN_WIDTH:3 * ATTN_WIDTH], preferred_element_type=F32)
    zvt = zv.T
    for hd in range(N_HEADS):
        vt_ref[hd] = zvt[hd * V_DIM:(hd + 1) * V_DIM, :].astype(BF16)
    u_ref[...] = jnp.dot(h, w_ref[:, 3 * ATTN_WIDTH:], preferred_element_type=F32)


def _mix_in(x, pre_g, shift, scale, w, layer, cos_t, sin_n, sin_p, q_scale):
    s, d = x.shape
    n_tiles = s // ROW_TILE
    vec = pl.BlockSpec((1, d), lambda i: (0, 0))
    tab = pl.BlockSpec((ROW_TILE, LANES), lambda i: (i, 0))
    return pl.pallas_call(
        functools.partial(_mixin_kernel, q_scale=q_scale),
        grid=(n_tiles,),
        in_specs=[
            pl.BlockSpec((ROW_TILE, d), lambda i: (i, 0)),
            vec, vec, vec,
            pl.BlockSpec((None,) + w.shape[1:], lambda i: (layer, 0, 0), pipeline_mode=pl.Buffered(1)),
            tab, tab, tab,
        ],
        out_specs=[
            pl.BlockSpec((N_HEADS, None, V_DIM, ROW_TILE), lambda i: (0, i, 0, 0)),
            pl.BlockSpec((None, N_HEADS, LANES), lambda i: (i, 0, 0)),
            pl.BlockSpec((N_HEADS, ROW_TILE, K_LANES), lambda i: (0, i, 0)),
            pl.BlockSpec((None, N_HEADS, LANES), lambda i: (i, 0, 0)),
            pl.BlockSpec((N_HEADS, None, V_DIM, ROW_TILE), lambda i: (0, i, 0, 0)),
            pl.BlockSpec((ROW_TILE, FNET_WIDTH), lambda i: (i, 0)),
        ],
        out_shape=[
            jax.ShapeDtypeStruct((N_HEADS, n_tiles, V_DIM, ROW_TILE), BF16),
            jax.ShapeDtypeStruct((n_tiles, N_HEADS, LANES), F32),
            jax.ShapeDtypeStruct((N_HEADS, s, K_LANES), BF16),
            jax.ShapeDtypeStruct((n_tiles, N_HEADS, LANES), F32),
            jax.ShapeDtypeStruct((N_HEADS, n_tiles, V_DIM, ROW_TILE), BF16),
            jax.ShapeDtypeStruct((s, FNET_WIDTH), F32),
        ],
        scratch_shapes=[pltpu.VMEM((ROW_TILE, d), BF16)],
        compiler_params=_cparams(("parallel",)),
        name="mix_in",
    )(x, pre_g, shift, scale, w, cos_t, sin_n, sin_p)


def _sublane_partial_sum(p):
    return p.reshape(p.shape[0] // SUBLANES, SUBLANES, p.shape[1]).sum(axis=0)


def _attn_kernel(qb_ref, kb_ref, lq_ref, g_ref, qt_ref, k_ref, vt_ref, o_ref,
                 qz_scr, s0_scr, s1_scr, c0_scr, c1_scr, acc_scr, m_scr, *, lambda_init):
    tq = qt_ref.shape[1]
    n_kv, _, tk = vt_ref.shape
    n_col = 2 * tq // MXU_DIM
    hd = pl.program_id(0)
    tile = lax.div(pl.program_id(1), ROW_TILE // tq)
    bsq = [qb_ref[tile, 2 * hd + m] * kb_ref[0, 2 * hd + m] for m in range(2)]
    fast = jnp.maximum(bsq[0], bsq[1]) <= (SHIFT_LIMIT / BOUND_MARGIN) ** 2
    shift = jnp.sqrt(jnp.concatenate(
        [jnp.full((1, tq), jnp.where(fast, b, 0.0), F32) for b in bsq], axis=1)) * BOUND_MARGIN
    qt = qt_ref[...].astype(F32)
    row = lax.broadcasted_iota(jnp.int32, (V_DIM, 2 * tq), 0)
    qz_scr[:V_DIM, :] = jnp.where((row < HEAD_DIM) == (lax.broadcasted_iota(
        jnp.int32, (V_DIM, 2 * tq), 1) < tq), jnp.concatenate([qt, qt], axis=1), 0.0).astype(BF16)
    qz_scr[V_DIM:, :] = jnp.where(row == 0, -shift, 0.0).astype(BF16)
    acc_scr[...] = jnp.zeros_like(acc_scr)

    @pl.when(fast)
    def _():
        span = ATTN_TS * ATTN_QK
        last = n_kv * tk - ATTN_QK

        def shifted_scores(r):
            return jnp.dot(k_ref[pl.ds(pl.multiple_of(r, ATTN_QK), ATTN_QK), :], qz_scr[...],
                           preferred_element_type=F32)

        s0_scr[:ATTN_QK, :] = shifted_scores(0)

        def body(t, carry):
            r0 = t * span
            pv = [None] * n_col
            psum = [jnp.zeros((SUBLANES, MXU_DIM), F32)] * n_col
            s_next = None
            for kb in range(ATTN_TS):
                s = s0_scr[:ATTN_QK, :] if kb == 0 else s_next
                s_next = shifted_scores(jnp.minimum(r0 + (kb + 1) * ATTN_QK, last))
                for sub in range(ATTN_QK // MXU_DIM):
                    blk, off = divmod(kb * ATTN_QK + sub * MXU_DIM, tk)
                    rows = slice(sub * MXU_DIM, (sub + 1) * MXU_DIM)
                    for nh in range(n_col):
                        p = jnp.exp2(s[rows, nh * MXU_DIM:(nh + 1) * MXU_DIM])
                        psum[nh] = psum[nh] + _sublane_partial_sum(p)
                        d = jnp.dot(vt_ref[t * (span // tk) + blk, :, off:off + MXU_DIM],
                                    p.astype(BF16), preferred_element_type=F32)
                        pv[nh] = d if pv[nh] is None else pv[nh] + d
            s0_scr[:ATTN_QK, :] = s_next
            for nh in range(n_col):
                cols = slice(nh * MXU_DIM, (nh + 1) * MXU_DIM)
                acc_scr[:V_DIM, cols] = acc_scr[:V_DIM, cols] + pv[nh]
                acc_scr[V_DIM:, cols] = acc_scr[V_DIM:, cols] + psum[nh]
            return carry

        lax.fori_loop(0, n_kv * tk // span, body, 0)

    @pl.when(jnp.logical_not(fast))
    def _():
        n_chunks = n_kv // ATTN_SUB
        ck = ATTN_SUB * tk
        m_scr[...] = jnp.full_like(m_scr, -jnp.inf)

        def scores(j, s_scr, c_scr):
            r = pl.multiple_of(j * ck, ck)
            s = jnp.dot(k_ref[pl.ds(r, ck), :], qz_scr[...],
                        preferred_element_type=F32)
            s_scr[...] = s
            c_scr[...] = jnp.max(s, axis=0, keepdims=True)

        def consume(j, s_scr, c_scr):
            m_prev = m_scr[...]
            m_new = jnp.maximum(m_prev, c_scr[...])
            alpha = jnp.exp2(m_prev - m_new)
            for nh in range(n_col):
                cols = slice(nh * MXU_DIM, (nh + 1) * MXU_DIM)
                pv = None
                psum = jnp.zeros((SUBLANES, MXU_DIM), F32)
                for kb in range(ck // MXU_DIM):
                    p = jnp.exp2(s_scr[kb * MXU_DIM:(kb + 1) * MXU_DIM, cols] - m_new[:, cols])
                    psum = psum + _sublane_partial_sum(p)
                    blk, off = divmod(kb * MXU_DIM, tk)
                    d = jnp.dot(vt_ref[j * ATTN_SUB + blk, :, off:off + MXU_DIM], p.astype(BF16),
                                preferred_element_type=F32)
                    pv = d if pv is None else pv + d
                acc_scr[:V_DIM, cols] = acc_scr[:V_DIM, cols] * alpha[:, cols] + pv
                acc_scr[V_DIM:, cols] = acc_scr[V_DIM:, cols] * alpha[:, cols] + psum
            m_scr[...] = m_new

        scores(0, s0_scr, c0_scr)

        def body(i, carry):
            j = 2 * i
            scores(j + 1, s1_scr, c1_scr)
            consume(j, s0_scr, c0_scr)
            scores(jnp.minimum(j + 2, n_chunks - 1), s0_scr, c0_scr)
            consume(j + 1, s1_scr, c1_scr)
            return carry

        lax.fori_loop(0, n_chunks // 2, body, 0)

    lq = lq_ref[...]
    lam = (jnp.exp(jnp.sum(lq[0:1] * lq[1:2], axis=-1, keepdims=True))
           - jnp.exp(jnp.sum(lq[2:3] * lq[3:4], axis=-1, keepdims=True)) + lambda_init)
    on = acc_scr[:V_DIM, :] / jnp.sum(acc_scr[V_DIM:, :], axis=0, keepdims=True)
    ot = on[:, :tq] - lam * on[:, tq:]
    ms = jnp.mean(ot * ot, axis=0, keepdims=True)
    o_ref[...] = (ot * lax.rsqrt(ms + SUBLN_EPS) * (g_ref[...] * (1.0 - lambda_init))).astype(BF16)


def _attention(qt, qn, k, kn, vt, lambda_qk, subln_g, lambda_init):
    n_heads, n_tiles, _, _ = qt.shape
    s = k.shape[1]
    n_kv = vt.shape[1]
    kmax = jnp.max(kn, axis=0)
    kb = jnp.stack([kmax[:, 0], kmax[:, HEAD_DIM]], axis=1).reshape(1, 2 * n_heads)
    qb = jnp.stack([qn[:, :, 0], qn[:, :, HEAD_DIM]], axis=2).reshape(n_tiles, 2 * n_heads)
    per_tile = ROW_TILE // ATTN_TQ
    return pl.pallas_call(
        functools.partial(_attn_kernel, lambda_init=lambda_init),
        grid=(n_heads, s // ATTN_TQ),
        in_specs=[
            pl.BlockSpec(memory_space=pltpu.SMEM),
            pl.BlockSpec(memory_space=pltpu.SMEM),
            pl.BlockSpec(lambda_qk.shape, lambda h, i: (0, 0)),
            pl.BlockSpec((V_DIM, 1), lambda h, i: (0, 0)),
            pl.BlockSpec((None, None, V_DIM, ATTN_TQ),
                         lambda h, i: (h, i // per_tile, 0, i % per_tile)),
            pl.BlockSpec((None, s, K_LANES), lambda h, i: (h, 0, 0)),
            pl.BlockSpec((None, n_kv, V_DIM, ATTN_TK), lambda h, i: (h, 0, 0, 0)),
        ],
        out_specs=pl.BlockSpec((V_DIM, ATTN_TQ), lambda h, i: (h, i)),
        out_shape=jax.ShapeDtypeStruct((n_heads * V_DIM, s), BF16),
        scratch_shapes=[pltpu.VMEM((K_LANES, 2 * ATTN_TQ), BF16),
                        pltpu.VMEM((ATTN_SUB * ATTN_TK, 2 * ATTN_TQ), F32),
                        pltpu.VMEM((ATTN_SUB * ATTN_TK, 2 * ATTN_TQ), F32),
                        pltpu.VMEM((1, 2 * ATTN_TQ), F32),
                        pltpu.VMEM((1, 2 * ATTN_TQ), F32),
                        pltpu.VMEM((ACC_ROWS, 2 * ATTN_TQ), F32),
                        pltpu.VMEM((1, 2 * ATTN_TQ), F32)],
        compiler_params=_cparams(("parallel", "parallel")),
        name="diff_attn",
    )(qb, kb, lambda_qk, subln_g.reshape(V_DIM, 1), qt, k, vt)


def _dft_tables():
    n = FFT_N
    jk = np.outer(np.arange(n), np.arange(n)) % n
    ang = 2.0 * np.pi * jk / n
    c = np.cos(ang) / math.sqrt(n)
    s = np.sin(ang) / math.sqrt(n)
    stage1 = np.concatenate([c, -s], axis=0)
    stage2 = np.block([[c, s], [-s, c]])
    chan = np.concatenate([c, s], axis=0)
    tw_ang = 2.0 * np.pi * np.outer(np.arange(n), np.arange(n)) / (n * n)
    return (_split(jnp.asarray(stage1, F32)), _split(jnp.asarray(stage2, F32)),
            _split(jnp.asarray(chan, F32)),
            jnp.asarray(np.cos(tw_ang), F32), jnp.asarray(np.sin(tw_ang), F32))


def _split(a):
    hi = a.astype(BF16)
    return hi, (a - hi.astype(F32)).astype(BF16)


def _dot3(a, b):
    (a_hi, a_lo), (b_hi, b_lo) = a, b
    return (jnp.dot(a_hi, b_hi, preferred_element_type=F32)
            + jnp.dot(a_hi, b_lo, preferred_element_type=F32)
            + jnp.dot(a_lo, b_hi, preferred_element_type=F32))


def _fft1_kernel(x_ref, fh_ref, fl_ref, twc_ref, tws_ref, o_ref):
    n = FFT_N
    t = _dot3((fh_ref[...], fl_ref[...]), _split(x_ref[...]))
    width = x_ref.shape[1] // FFT_NB
    for b in range(FFT_NB):
        tr = t[:n, b * width:(b + 1) * width]
        ti = t[n:, b * width:(b + 1) * width]
        c = jnp.concatenate([twc_ref[b]] * (width // LANES), axis=1)
        s = jnp.concatenate([tws_ref[b]] * (width // LANES), axis=1)
        o_ref[0, :, b * width:(b + 1) * width] = tr * c + ti * s
        o_ref[1, :, b * width:(b + 1) * width] = ti * c - tr * s


def _fft2_kernel(t_ref, fh_ref, fl_ref, chh_ref, chl_ref, o_ref):
    n = FFT_N
    width = t_ref.shape[3]
    f = (fh_ref[...], fl_ref[...])
    ch = (chh_ref[...], chl_ref[...])
    for b in range(FFT_KB):
        tt = jnp.concatenate([t_ref[0, b], t_ref[1, b]], axis=0)
        z = _dot3(f, _split(tt))
        for g in range(N_GROUPS):
            zz = jnp.concatenate([z[:n, g * GROUP_DIM:(g + 1) * GROUP_DIM],
                                  z[n:, g * GROUP_DIM:(g + 1) * GROUP_DIM]], axis=1)
            y = _dot3(_split(zz), ch)
            o_ref[:, b * width + g * GROUP_DIM:b * width + (g + 1) * GROUP_DIM] = y.astype(o_ref.dtype)


def _fourier_mix(u, tables):
    s, width = u.shape
    n = FFT_N
    stage1, stage2, chan, twc, tws = tables
    twc_b = jnp.broadcast_to(twc[:, :, None], (n, n, LANES))
    tws_b = jnp.broadcast_to(tws[:, :, None], (n, n, LANES))
    t = pl.pallas_call(
        _fft1_kernel,
        grid=(n // FFT_NB,),
        in_specs=[
            pl.BlockSpec((n, FFT_NB * width), lambda j: (0, j)),
            pl.BlockSpec((2 * n, n), lambda j: (0, 0)),
            pl.BlockSpec((2 * n, n), lambda j: (0, 0)),
            pl.BlockSpec((FFT_NB, n, LANES), lambda j: (j, 0, 0)),
            pl.BlockSpec((FFT_NB, n, LANES), lambda j: (j, 0, 0)),
        ],
        out_specs=pl.BlockSpec((2, n, FFT_NB * width), lambda j: (0, 0, j)),
        out_shape=jax.ShapeDtypeStruct((2, n, n * width), F32),
        compiler_params=_cparams(("parallel",)),
        name="fft_stage1",
    )(u.reshape(n, n * width), *stage1, twc_b, tws_b)
    y = pl.pallas_call(
        _fft2_kernel,
        grid=(n // FFT_KB,),
        in_specs=[
            pl.BlockSpec((2, FFT_KB, n, width), lambda i: (0, i, 0, 0)),
            pl.BlockSpec((2 * n, 2 * n), lambda i: (0, 0)),
            pl.BlockSpec((2 * n, 2 * n), lambda i: (0, 0)),
            pl.BlockSpec((2 * n, n), lambda i: (0, 0)),
            pl.BlockSpec((2 * n, n), lambda i: (0, 0)),
        ],
        out_specs=pl.BlockSpec((n, FFT_KB * width), lambda i: (0, i)),
        out_shape=jax.ShapeDtypeStruct((n, n * width), BF16),
        compiler_params=_cparams(("parallel",)),
        name="fft_stage2",
    )(t.reshape(2, n, n, width), *stage2, *chan)
    return y.reshape(s, width)


def _mixout_kernel(x_ref, pg_ref, sh_ref, sc_ref, gt_ref, qg_ref, ao_ref, fy_ref,
                   gwa_ref, gwf_ref, gba_ref, gbf_ref, ap_ref, fp_ref, wo_ref,
                   o_ref, h_scr, acc_scr):
    j = pl.program_id(1)

    @pl.when(j == 0)
    def _():
        _modulated_norm(x_ref, h_scr, pg_ref, sc_ref, sh_ref)
        acc_scr[...] = jnp.zeros_like(acc_scr)

    h = h_scr[...]
    ga = jax.nn.sigmoid(jnp.dot(h, gwa_ref[...], preferred_element_type=F32) + gba_ref[...])
    gf = jax.nn.sigmoid(jnp.dot(h, gwf_ref[...], preferred_element_type=F32) + gbf_ref[...])
    ya = lax.dot_general(ao_ref[...], ap_ref[...], (((0,), (0,)), ((), ())),
                         preferred_element_type=F32)
    yf = jnp.dot(fy_ref[...], fp_ref[...], preferred_element_type=F32)
    y = (ga * ya + gf * yf).astype(BF16)
    acc_scr[...] += jnp.dot(y, wo_ref[...], preferred_element_type=F32)

    @pl.when(j == pl.num_programs(1) - 1)
    def _():
        _gated_norm_residual(x_ref, acc_scr, o_ref, qg_ref, gt_ref, 1.0)


def _mix_out(x, pre_g, shift, scale, gate, post_g, ao, fy, gate_w, gate_b, attn_proj,
             fnet_proj, w_out, layer):
    s, d = x.shape
    nc = d // MIX_TC
    vec = pl.BlockSpec((1, d), lambda i, j: (0, 0))
    return pl.pallas_call(
        _mixout_kernel,
        grid=(s // ROW_TILE, nc),
        in_specs=[
            pl.BlockSpec((ROW_TILE, d), lambda i, j: (i, 0)),
            vec, vec, vec, vec, vec,
            pl.BlockSpec((ATTN_WIDTH, ROW_TILE), lambda i, j: (0, i)),
            pl.BlockSpec((ROW_TILE, FNET_WIDTH), lambda i, j: (i, 0)),
            pl.BlockSpec((None, d, MIX_TC), lambda i, j: (layer, 0, j)),
            pl.BlockSpec((None, d, MIX_TC), lambda i, j: (layer, 0, nc + j)),
            pl.BlockSpec((1, MIX_TC), lambda i, j: (0, j)),
            pl.BlockSpec((1, MIX_TC), lambda i, j: (0, nc + j)),
            pl.BlockSpec((None, ATTN_WIDTH, MIX_TC), lambda i, j: (layer, 0, j)),
            pl.BlockSpec((None, FNET_WIDTH, MIX_TC), lambda i, j: (layer, 0, j)),
            pl.BlockSpec((None, MIX_TC, d), lambda i, j: (layer, j, 0)),
        ],
        out_specs=pl.BlockSpec((ROW_TILE, d), lambda i, j: (i, 0)),
        out_shape=jax.ShapeDtypeStruct((s, d), F32),
        scratch_shapes=[pltpu.VMEM((ROW_TILE, d), BF16), pltpu.VMEM((ROW_TILE, d), F32)],
        compiler_params=_cparams(("parallel", "arbitrary")),
        name="mix_out",
    )(x, pre_g, shift, scale, gate, post_g, ao, fy, gate_w, gate_w, gate_b, gate_b,
      attn_proj, fnet_proj, w_out)


def _rope_tables(seq):
    half = HEAD_DIM // 2
    pos = jnp.arange(seq, dtype=F32)
    inv_freq = ROPE_THETA ** (-jnp.arange(0, HEAD_DIM, 2, dtype=F32) / HEAD_DIM)
    ang = pos[:, None] * inv_freq[None, :]
    cos, sin = jnp.cos(ang), jnp.sin(ang)
    zero = jnp.zeros_like(sin)
    cos_t = jnp.concatenate([cos] * (LANES // half), axis=1)
    sin_n = jnp.concatenate([-sin, zero] * (LANES // HEAD_DIM), axis=1)
    sin_p = jnp.concatenate([zero, sin] * (LANES // HEAD_DIM), axis=1)
    return cos_t, sin_n, sin_p


def kernel(x, c, ada_w, ada_b, pre_norm_g, post_norm_g, ffn1_w_in, ffn1_w_out, mix_w_in,
           lambda_qk, subln_g, attn_proj, fnet_proj, branch_gate_w, branch_gate_b, mix_w_out,
           ffn2_w_in, ffn2_w_out):
    b, s, d = x.shape
    assert b == 1 and s == FFT_N * FFT_N and s % ROW_TILE == 0
    n_layers = ada_w.shape[0]
    cos_t, sin_n, sin_p = _rope_tables(s)
    dft = _dft_tables()
    mod = _ada_mod(c, ada_w, ada_b).reshape(n_layers, 3, 3, 1, d)
    q_scale = HEAD_DIM ** -0.5 * math.log2(math.e)
    xs = x.reshape(s, d)
    ffn1_in, ffn1_out, ffn2_in, ffn2_out, mix_in_w, gate_w, attn_w, fnet_w, mix_out_w = (
        w.astype(BF16) for w in (ffn1_w_in, ffn1_w_out, ffn2_w_in, ffn2_w_out, mix_w_in,
                                 branch_gate_w, attn_proj, fnet_proj, mix_w_out))
    for l in range(n_layers):
        lambda_init = 0.8 - 0.6 * math.exp(-0.3 * l)
        pre = pre_norm_g[l].reshape(3, 1, d)
        post = post_norm_g[l].reshape(3, 1, d)
        xs = _ffn(xs, pre[0], mod[l, 0, 0], mod[l, 0, 1], mod[l, 0, 2], post[0],
                  ffn1_in, ffn1_out, l)
        qt, qn, k, kn, vt, u = _mix_in(xs, pre[1], mod[l, 1, 0], mod[l, 1, 1], mix_in_w, l,
                              cos_t, sin_n, sin_p, q_scale)
        ao = _attention(qt, qn, k, kn, vt, lambda_qk[l], subln_g[l], lambda_init)
        fy = _fourier_mix(u, dft)
        xs = _mix_out(xs, pre[1], mod[l, 1, 0], mod[l, 1, 1], mod[l, 1, 2], post[1], ao, fy,
                      gate_w, branch_gate_b[l].reshape(1, 2 * d), attn_w, fnet_w, mix_out_w, l)
        xs = _ffn(xs, pre[2], mod[l, 2, 0], mod[l, 2, 1], mod[l, 2, 2], post[2],
                  ffn2_in, ffn2_out, l)
    return xs.reshape(b, s, d)
```

```python
import functools
import math

import numpy as np
import jax
import jax.numpy as jnp
from jax import lax
from jax.experimental import pallas as pl
from jax.experimental.pallas import tpu as pltpu

F32 = jnp.float32
BF16 = jnp.bfloat16

N_HEADS = 8
HEAD_DIM = 64
V_DIM = 2 * HEAD_DIM
ACC_ROWS = V_DIM + 8
K_LANES = 2 * V_DIM
ATTN_WIDTH = N_HEADS * V_DIM
N_GROUPS = 8
GROUP_DIM = 128
FNET_WIDTH = N_GROUPS * GROUP_DIM
ROPE_THETA = 10000.0
NORM_EPS = 1e-6
SUBLN_EPS = 1e-5
MACARON_WEIGHT = 0.5

LANES = 128
SUBLANES = 8
NORM_ROWS = 16
MXU_DIM = 256
VMEM_LIMIT_BYTES = 56 * 1024 * 1024

ADA_TN = 1024
ROW_TILE = 512
FFN_ROWS = 1024
FFN_TF = 512
MIX_TC = 512
ATTN_TQ = 256
ATTN_TK = ROW_TILE
ATTN_SUB = 2
ATTN_QK = 512
ATTN_TS = 32
SHIFT_LIMIT = 60.0
BOUND_MARGIN = 1.0 + 2.0 ** -7
FFT_N = 128
FFT_NB = 4
FFT_KB = 4


def _cparams(sem):
    return pltpu.CompilerParams(dimension_semantics=sem, vmem_limit_bytes=VMEM_LIMIT_BYTES)


def _modulated_norm(x_ref, h_ref, g_ref, scale_ref, shift_ref):
    gs = g_ref[...] * (1.0 + scale_ref[...])
    shift = shift_ref[...]
    for r in range(0, x_ref.shape[0], NORM_ROWS):
        x = x_ref[r:r + NORM_ROWS, :]
        ms = jnp.mean(x * x, axis=-1, keepdims=True)
        h_ref[r:r + NORM_ROWS, :] = (x * lax.rsqrt(ms + NORM_EPS) * gs + shift).astype(h_ref.dtype)


def _gated_norm_residual(x_ref, y_ref, o_ref, g_ref, gate_ref, weight):
    coef = g_ref[...] * (weight * gate_ref[...])
    for r in range(0, x_ref.shape[0], NORM_ROWS):
        y = y_ref[r:r + NORM_ROWS, :]
        ms = jnp.mean(y * y, axis=-1, keepdims=True)
        o_ref[r:r + NORM_ROWS, :] = x_ref[r:r + NORM_ROWS, :] + y * lax.rsqrt(ms + NORM_EPS) * coef


def _ada_kernel(c_ref, w_ref, b_ref, o_ref):
    d, tn = w_ref.shape

    def body(k, acc):
        r = pl.multiple_of(k * SUBLANES, SUBLANES)
        cb = c_ref[pl.ds(r, SUBLANES), :]
        cb = cb * jax.nn.sigmoid(cb)
        return acc + w_ref[pl.ds(r, SUBLANES), :] * jnp.concatenate([cb] * (tn // LANES), axis=1)

    acc = lax.fori_loop(0, d // SUBLANES, body, jnp.zeros((SUBLANES, tn), F32), unroll=8)
    o_ref[...] = jnp.sum(acc, axis=0, keepdims=True) + b_ref[...]


def _ada_mod(c, ada_w, ada_b):
    n_layers, d, n = ada_w.shape
    c_b = jnp.broadcast_to(c.reshape(d, 1), (d, LANES))
    out = pl.pallas_call(
        _ada_kernel,
        grid=(n_layers, n // ADA_TN),
        in_specs=[
            pl.BlockSpec((d, LANES), lambda l, j: (0, 0)),
            pl.BlockSpec((None, d, ADA_TN), lambda l, j: (l, 0, j)),
            pl.BlockSpec((None, 1, ADA_TN), lambda l, j: (l, 0, j)),
        ],
        out_specs=pl.BlockSpec((None, 1, ADA_TN), lambda l, j: (l, 0, j)),
        out_shape=jax.ShapeDtypeStruct((n_layers, 1, n), F32),
        compiler_params=_cparams(("parallel", "parallel")),
        name="ada_mod",
    )(c_b, ada_w, ada_b.reshape(n_layers, 1, n))
    return out


def _ffn_kernel(x_ref, pg_ref, sh_ref, sc_ref, gt_ref, qg_ref, wg_ref, wu_ref, wo_ref,
                o_ref, h_scr):
    j = pl.program_id(1)

    @pl.when(j == 0)
    def _():
        _modulated_norm(x_ref, h_scr, pg_ref, sc_ref, sh_ref)
        o_ref[...] = jnp.zeros_like(o_ref)

    h = h_scr[...]
    g = jnp.dot(h, wg_ref[...], preferred_element_type=F32)
    u = jnp.dot(h, wu_ref[...], preferred_element_type=F32)
    a = (g * jax.nn.sigmoid(g) * u).astype(BF16)
    o_ref[...] += jnp.dot(a, wo_ref[...], preferred_element_type=F32)

    @pl.when(j == pl.num_programs(1) - 1)
    def _():
        _gated_norm_residual(x_ref, o_ref, o_ref, qg_ref, gt_ref, MACARON_WEIGHT)


def _ffn(x, pre_g, shift, scale, gate, post_g, w_in, w_out, layer):
    s, d = x.shape
    f = w_out.shape[1]
    nf = f // FFN_TF
    vec = pl.BlockSpec((1, d), lambda i, j: (0, 0))
    return pl.pallas_call(
        _ffn_kernel,
        grid=(s // FFN_ROWS, nf),
        in_specs=[
            pl.BlockSpec((FFN_ROWS, d), lambda i, j: (i, 0)),
            vec, vec, vec, vec, vec,
            pl.BlockSpec((None, d, FFN_TF), lambda i, j: (layer, 0, j)),
            pl.BlockSpec((None, d, FFN_TF), lambda i, j: (layer, 0, nf + j)),
            pl.BlockSpec((None, FFN_TF, d), lambda i, j: (layer, j, 0)),
        ],
        out_specs=pl.BlockSpec((FFN_ROWS, d), lambda i, j: (i, 0)),
        out_shape=jax.ShapeDtypeStruct((s, d), F32),
        scratch_shapes=[pltpu.VMEM((FFN_ROWS, d), BF16)],
        compiler_params=_cparams(("parallel", "arbitrary")),
        name="ffn",
    )(x, pre_g, shift, scale, gate, post_g, w_in, w_in, w_out)


def _mixin_kernel(x_ref, pg_ref, sh_ref, sc_ref, w_ref, cos_ref, sinn_ref, sinp_ref,
                  qt_ref, qn_ref, k_ref, kn_ref, vt_ref, u_ref, h_scr, *, q_scale):
    _modulated_norm(x_ref, h_scr, pg_ref, sc_ref, sh_ref)
    h = h_scr[...]
    cos_t, sin_n, sin_p = cos_ref[...], sinn_ref[...], sinp_ref[...]

    def rope(zs):
        return (zs * cos_t + pltpu.roll(zs, LANES - HEAD_DIM // 2, 1) * sin_n
                + pltpu.roll(zs, HEAD_DIM // 2, 1) * sin_p)

    lane = lax.broadcasted_iota(jnp.int32, (x_ref.shape[0], LANES), 1)

    def max_sq_norms(t):
        t2 = t.astype(F32) * t.astype(F32)
        n1 = jnp.max(jnp.sum(jnp.where(lane < HEAD_DIM, t2, 0.0), axis=1, keepdims=True),
                     axis=0, keepdims=True)
        n2 = jnp.max(jnp.sum(jnp.where(lane < HEAD_DIM, 0.0, t2), axis=1, keepdims=True),
                     axis=0, keepdims=True)
        return jnp.where(lane[:1] < HEAD_DIM, n1, n2)

    zq = jnp.dot(h, w_ref[:, 0:ATTN_WIDTH], preferred_element_type=F32)
    for hd in range(N_HEADS):
        qr = (rope(zq[:, hd * V_DIM:(hd + 1) * V_DIM]) * q_scale).astype(BF16)
        qt_ref[hd] = qr.astype(F32).T.astype(BF16)
        qn_ref[hd:hd + 1, :] = max_sq_norms(qr)
    zk = jnp.dot(h, w_ref[:, ATTN_WIDTH:2 * ATTN_WIDTH], preferred_element_type=F32)
    shift_lanes = jnp.where(lane == 0, 1.0, 0.0).astype(BF16)
    for hd in range(N_HEADS):
        kr = rope(zk[:, hd * V_DIM:(hd + 1) * V_DIM]).astype(BF16)
        k_ref[hd, :, :V_DIM] = kr
        k_ref[hd, :, V_DIM:] = shift_lanes
        kn_ref[hd:hd + 1, :] = max_sq_norms(kr)
    zv = jnp.dot(h, w_ref[:, 2 * ATTN_WIDTH:3 * ATTN_WIDTH], preferred_element_type=F32)
    zvt = zv.T
    for hd in range(N_HEADS):
        vt_ref[hd] = zvt[hd * V_DIM:(hd + 1) * V_DIM, :].astype(BF16)
    u_ref[...] = jnp.dot(h, w_ref[:, 3 * ATTN_WIDTH:], preferred_element_type=F32)


def _mix_in(x, pre_g, shift, scale, w, layer, cos_t, sin_n, sin_p, q_scale):
    s, d = x.shape
    n_tiles = s // ROW_TILE
    vec = pl.BlockSpec((1, d), lambda i: (0, 0))
    tab = pl.BlockSpec((ROW_TILE, LANES), lambda i: (i, 0))
    return pl.pallas_call(
        functools.partial(_mixin_kernel, q_scale=q_scale),
        grid=(n_tiles,),
        in_specs=[
            pl.BlockSpec((ROW_TILE, d), lambda i: (i, 0)),
            vec, vec, vec,
            pl.BlockSpec((None,) + w.shape[1:], lambda i: (layer, 0, 0), pipeline_mode=pl.Buffered(1)),
            tab, tab, tab,
        ],
        out_specs=[
            pl.BlockSpec((N_HEADS, None, V_DIM, ROW_TILE), lambda i: (0, i, 0, 0)),
            pl.BlockSpec((None, N_HEADS, LANES), lambda i: (i, 0, 0)),
            pl.BlockSpec((N_HEADS, ROW_TILE, K_LANES), lambda i: (0, i, 0)),
            pl.BlockSpec((None, N_HEADS, LANES), lambda i: (i, 0, 0)),
            pl.BlockSpec((N_HEADS, None, V_DIM, ROW_TILE), lambda i: (0, i, 0, 0)),
            pl.BlockSpec((ROW_TILE, FNET_WIDTH), lambda i: (i, 0)),
        ],
        out_shape=[
            jax.ShapeDtypeStruct((N_HEADS, n_tiles, V_DIM, ROW_TILE), BF16),
            jax.ShapeDtypeStruct((n_tiles, N_HEADS, LANES), F32),
            jax.ShapeDtypeStruct((N_HEADS, s, K_LANES), BF16),
            jax.ShapeDtypeStruct((n_tiles, N_HEADS, LANES), F32),
            jax.ShapeDtypeStruct((N_HEADS, n_tiles, V_DIM, ROW_TILE), BF16),
            jax.ShapeDtypeStruct((s, FNET_WIDTH), F32),
        ],
        scratch_shapes=[pltpu.VMEM((ROW_TILE, d), BF16)],
        compiler_params=_cparams(("parallel",)),
        name="mix_in",
    )(x, pre_g, shift, scale, w, cos_t, sin_n, sin_p)


def _sublane_partial_sum(p):
    return p.reshape(p.shape[0] // SUBLANES, SUBLANES, p.shape[1]).sum(axis=0)


def _attn_kernel(qb_ref, kb_ref, lq_ref, g_ref, qt_ref, k_ref, vt_ref, o_ref,
                 qz_scr, s0_scr, s1_scr, c0_scr, c1_scr, acc_scr, m_scr, *, lambda_init):
    tq = qt_ref.shape[1]
    n_kv, _, tk = vt_ref.shape
    n_col = 2 * tq // MXU_DIM
    hd = pl.program_id(0)
    tile = lax.div(pl.program_id(1), ROW_TILE // tq)
    bsq = [qb_ref[tile, 2 * hd + m] * kb_ref[0, 2 * hd + m] for m in range(2)]
    fast = jnp.maximum(bsq[0], bsq[1]) <= (SHIFT_LIMIT / BOUND_MARGIN) ** 2
    shift = jnp.sqrt(jnp.concatenate(
        [jnp.full((1, tq), jnp.where(fast, b, 0.0), F32) for b in bsq], axis=1)) * BOUND_MARGIN
    qt = qt_ref[...].astype(F32)
    row = lax.broadcasted_iota(jnp.int32, (V_DIM, 2 * tq), 0)
    qz_scr[:V_DIM, :] = jnp.where((row < HEAD_DIM) == (lax.broadcasted_iota(
        jnp.int32, (V_DIM, 2 * tq), 1) < tq), jnp.concatenate([qt, qt], axis=1), 0.0).astype(BF16)
    qz_scr[V_DIM:, :] = jnp.where(row == 0, -shift, 0.0).astype(BF16)
    acc_scr[...] = jnp.zeros_like(acc_scr)

    @pl.when(fast)
    def _():
        span = ATTN_TS * ATTN_QK
        last = n_kv * tk - ATTN_QK

        def shifted_scores(r):
            return jnp.dot(k_ref[pl.ds(pl.multiple_of(r, ATTN_QK), ATTN_QK), :], qz_scr[...],
                           preferred_element_type=F32)

        s0_scr[:ATTN_QK, :] = shifted_scores(0)

        def body(t, carry):
            r0 = t * span
            pv = [None] * n_col
            psum = [jnp.zeros((SUBLANES, MXU_DIM), F32)] * n_col
            s_next = None
            for kb in range(ATTN_TS):
                s = s0_scr[:ATTN_QK, :] if kb == 0 else s_next
                s_next = shifted_scores(jnp.minimum(r0 + (kb + 1) * ATTN_QK, last))
                for sub in range(ATTN_QK // MXU_DIM):
                    blk, off = divmod(kb * ATTN_QK + sub * MXU_DIM, tk)
                    rows = slice(sub * MXU_DIM, (sub + 1) * MXU_DIM)
                    for nh in range(n_col):
                        p = jnp.exp2(s[rows, nh * MXU_DIM:(nh + 1) * MXU_DIM])
                        psum[nh] = psum[nh] + _sublane_partial_sum(p)
                        d = jnp.dot(vt_ref[t * (span // tk) + blk, :, off:off + MXU_DIM],
                                    p.astype(BF16), preferred_element_type=F32)
                        pv[nh] = d if pv[nh] is None else pv[nh] + d
            s0_scr[:ATTN_QK, :] = s_next
            for nh in range(n_col):
                cols = slice(nh * MXU_DIM, (nh + 1) * MXU_DIM)
                acc_scr[:V_DIM, cols] = acc_scr[:V_DIM, cols] + pv[nh]
                acc_scr[V_DIM:, cols] = acc_scr[V_DIM:, cols] + psum[nh]
            return carry

        lax.fori_loop(0, n_kv * tk // span, body, 0)

    @pl.when(jnp.logical_not(fast))
    def _():
        n_chunks = n_kv // ATTN_SUB
        ck = ATTN_SUB * tk
        m_scr[...] = jnp.full_like(m_scr, -jnp.inf)

        def scores(j, s_scr, c_scr):
            r = pl.multiple_of(j * ck, ck)
            s = jnp.dot(k_ref[pl.ds(r, ck), :], qz_scr[...],
                        preferred_element_type=F32)
            s_scr[...] = s
            c_scr[...] = jnp.max(s, axis=0, keepdims=True)

        def consume(j, s_scr, c_scr):
            m_prev = m_scr[...]
            m_new = jnp.maximum(m_prev, c_scr[...])
            alpha = jnp.exp2(m_prev - m_new)
            for nh in range(n_col):
                cols = slice(nh * MXU_DIM, (nh + 1) * MXU_DIM)
                pv = None
                psum = jnp.zeros((SUBLANES, MXU_DIM), F32)
                for kb in range(ck // MXU_DIM):
                    p = jnp.exp2(s_scr[kb * MXU_DIM:(kb + 1) * MXU_DIM, cols] - m_new[:, cols])
                    psum = psum + _sublane_partial_sum(p)
                    blk, off = divmod(kb * MXU_DIM, tk)
                    d = jnp.dot(vt_ref[j * ATTN_SUB + blk, :, off:off + MXU_DIM], p.astype(BF16),
                                preferred_element_type=F32)
                    pv = d if pv is None else pv + d
                acc_scr[:V_DIM, cols] = acc_scr[:V_DIM, cols] * alpha[:, cols] + pv
                acc_scr[V_DIM:, cols] = acc_scr[V_DIM:, cols] * alpha[:, cols] + psum
            m_scr[...] = m_new

        scores(0, s0_scr, c0_scr)

        def body(i, carry):
            j = 2 * i
            scores(j + 1, s1_scr, c1_scr)
            consume(j, s0_scr, c0_scr)
            scores(jnp.minimum(j + 2, n_chunks - 1), s0_scr, c0_scr)
            consume(j + 1, s1_scr, c1_scr)
            return carry

        lax.fori_loop(0, n_chunks // 2, body, 0)

    lq = lq_ref[...]
    lam = (jnp.exp(jnp.sum(lq[0:1] * lq[1:2], axis=-1, keepdims=True))
           - jnp.exp(jnp.sum(lq[2:3] * lq[3:4], axis=-1, keepdims=True)) + lambda_init)
    on = acc_scr[:V_DIM, :] / jnp.sum(acc_scr[V_DIM:, :], axis=0, keepdims=True)
    ot = on[:, :tq] - lam * on[:, tq:]
    ms = jnp.mean(ot * ot, axis=0, keepdims=True)
    o_ref[...] = (ot * lax.rsqrt(ms + SUBLN_EPS) * (g_ref[...] * (1.0 - lambda_init))).astype(BF16)


def _attention(qt, qn, k, kn, vt, lambda_qk, subln_g, lambda_init):
    n_heads, n_tiles, _, _ = qt.shape
    s = k.shape[1]
    n_kv = vt.shape[1]
    kmax = jnp.max(kn, axis=0)
    kb = jnp.stack([kmax[:, 0], kmax[:, HEAD_DIM]], axis=1).reshape(1, 2 * n_heads)
    qb = jnp.stack([qn[:, :, 0], qn[:, :, HEAD_DIM]], axis=2).reshape(n_tiles, 2 * n_heads)
    per_tile = ROW_TILE // ATTN_TQ
    return pl.pallas_call(
        functools.partial(_attn_kernel, lambda_init=lambda_init),
        grid=(n_heads, s // ATTN_TQ),
        in_specs=[
            pl.BlockSpec(memory_space=pltpu.SMEM),
            pl.BlockSpec(memory_space=pltpu.SMEM),
            pl.BlockSpec(lambda_qk.shape, lambda h, i: (0, 0)),
            pl.BlockSpec((V_DIM, 1), lambda h, i: (0, 0)),
            pl.BlockSpec((None, None, V_DIM, ATTN_TQ),
                         lambda h, i: (h, i // per_tile, 0, i % per_tile)),
            pl.BlockSpec((None, s, K_LANES), lambda h, i: (h, 0, 0)),
            pl.BlockSpec((None, n_kv, V_DIM, ATTN_TK), lambda h, i: (h, 0, 0, 0)),
        ],
        out_specs=pl.BlockSpec((V_DIM, ATTN_TQ), lambda h, i: (h, i)),
        out_shape=jax.ShapeDtypeStruct((n_heads * V_DIM, s), BF16),
        scratch_shapes=[pltpu.VMEM((K_LANES, 2 * ATTN_TQ), BF16),
                        pltpu.VMEM((ATTN_SUB * ATTN_TK, 2 * ATTN_TQ), F32),
                        pltpu.VMEM((ATTN_SUB * ATTN_TK, 2 * ATTN_TQ), F32),
                        pltpu.VMEM((1, 2 * ATTN_TQ), F32),
                        pltpu.VMEM((1, 2 * ATTN_TQ), F32),
                        pltpu.VMEM((ACC_ROWS, 2 * ATTN_TQ), F32),
                        pltpu.VMEM((1, 2 * ATTN_TQ), F32)],
        compiler_params=_cparams(("parallel", "parallel")),
        name="diff_attn",
    )(qb, kb, lambda_qk, subln_g.reshape(V_DIM, 1), qt, k, vt)


def _dft_tables():
    n = FFT_N
    jk = np.outer(np.arange(n), np.arange(n)) % n
    ang = 2.0 * np.pi * jk / n
    c = np.cos(ang) / math.sqrt(n)
    s = np.sin(ang) / math.sqrt(n)
    stage1 = np.concatenate([c, -s], axis=0)
    stage2 = np.block([[c, s], [-s, c]])
    chan = np.concatenate([c, s], axis=0)
    tw_ang = 2.0 * np.pi * np.outer(np.arange(n), np.arange(n)) / (n * n)
    return (_split(jnp.asarray(stage1, F32)), _split(jnp.asarray(stage2, F32)),
            _split(jnp.asarray(chan, F32)),
            jnp.asarray(np.cos(tw_ang), F32), jnp.asarray(np.sin(tw_ang), F32))


def _split(a):
    hi = a.astype(BF16)
    return hi, (a - hi.astype(F32)).astype(BF16)


def _dot3(a, b):
    (a_hi, a_lo), (b_hi, b_lo) = a, b
    return (jnp.dot(a_hi, b_hi, preferred_element_type=F32)
            + jnp.dot(a_hi, b_lo, preferred_element_type=F32)
            + jnp.dot(a_lo, b_hi, preferred_element_type=F32))


def _fft1_kernel(x_ref, fh_ref, fl_ref, twc_ref, tws_ref, o_ref):
    n = FFT_N
    t = _dot3((fh_ref[...], fl_ref[...]), _split(x_ref[...]))
    width = x_ref.shape[1] // FFT_NB
    for b in range(FFT_NB):
        tr = t[:n, b * width:(b + 1) * width]
        ti = t[n:, b * width:(b + 1) * width]
        c = jnp.concatenate([twc_ref[b]] * (width // LANES), axis=1)
        s = jnp.concatenate([tws_ref[b]] * (width // LANES), axis=1)
        o_ref[0, :, b * width:(b + 1) * width] = tr * c + ti * s
        o_ref[1, :, b * width:(b + 1) * width] = ti * c - tr * s


def _fft2_kernel(t_ref, fh_ref, fl_ref, chh_ref, chl_ref, o_ref):
    n = FFT_N
    width = t_ref.shape[3]
    f = (fh_ref[...], fl_ref[...])
    ch = (chh_ref[...], chl_ref[...])
    for b in range(FFT_KB):
        tt = jnp.concatenate([t_ref[0, b], t_ref[1, b]], axis=0)
        z = _dot3(f, _split(tt))
        for g in range(N_GROUPS):
            zz = jnp.concatenate([z[:n, g * GROUP_DIM:(g + 1) * GROUP_DIM],
                                  z[n:, g * GROUP_DIM:(g + 1) * GROUP_DIM]], axis=1)
            y = _dot3(_split(zz), ch)
            o_ref[:, b * width + g * GROUP_DIM:b * width + (g + 1) * GROUP_DIM] = y.astype(o_ref.dtype)


def _fourier_mix(u, tables):
    s, width = u.shape
    n = FFT_N
    stage1, stage2, chan, twc, tws = tables
    twc_b = jnp.broadcast_to(twc[:, :, None], (n, n, LANES))
    tws_b = jnp.broadcast_to(tws[:, :, None], (n, n, LANES))
    t = pl.pallas_call(
        _fft1_kernel,
        grid=(n // FFT_NB,),
        in_specs=[
            pl.BlockSpec((n, FFT_NB * width), lambda j: (0, j)),
            pl.BlockSpec((2 * n, n), lambda j: (0, 0)),
            pl.BlockSpec((2 * n, n), lambda j: (0, 0)),
            pl.BlockSpec((FFT_NB, n, LANES), lambda j: (j, 0, 0)),
            pl.BlockSpec((FFT_NB, n, LANES), lambda j: (j, 0, 0)),
        ],
        out_specs=pl.BlockSpec((2, n, FFT_NB * width), lambda j: (0, 0, j)),
        out_shape=jax.ShapeDtypeStruct((2, n, n * width), F32),
        compiler_params=_cparams(("parallel",)),
        name="fft_stage1",
    )(u.reshape(n, n * width), *stage1, twc_b, tws_b)
    y = pl.pallas_call(
        _fft2_kernel,
        grid=(n // FFT_KB,),
        in_specs=[
            pl.BlockSpec((2, FFT_KB, n, width), lambda i: (0, i, 0, 0)),
            pl.BlockSpec((2 * n, 2 * n), lambda i: (0, 0)),
            pl.BlockSpec((2 * n, 2 * n), lambda i: (0, 0)),
            pl.BlockSpec((2 * n, n), lambda i: (0, 0)),
            pl.BlockSpec((2 * n, n), lambda i: (0, 0)),
        ],
        out_specs=pl.BlockSpec((n, FFT_KB * width), lambda i: (0, i)),
        out_shape=jax.ShapeDtypeStruct((n, n * width), BF16),
        compiler_params=_cparams(("parallel",)),
        name="fft_stage2",
    )(t.reshape(2, n, n, width), *stage2, *chan)
    return y.reshape(s, width)


def _mixout_kernel(x_ref, pg_ref, sh_ref, sc_ref, gt_ref, qg_ref, ao_ref, fy_ref,
                   gwa_ref, gwf_ref, gba_ref, gbf_ref, ap_ref, fp_ref, wo_ref,
                   o_ref, h_scr, acc_scr):
    j = pl.program_id(1)

    @pl.when(j == 0)
    def _():
        _modulated_norm(x_ref, h_scr, pg_ref, sc_ref, sh_ref)
        acc_scr[...] = jnp.zeros_like(acc_scr)

    h = h_scr[...]
    ga = jax.nn.sigmoid(jnp.dot(h, gwa_ref[...], preferred_element_type=F32) + gba_ref[...])
    gf = jax.nn.sigmoid(jnp.dot(h, gwf_ref[...], preferred_element_type=F32) + gbf_ref[...])
    ya = lax.dot_general(ao_ref[...], ap_ref[...], (((0,), (0,)), ((), ())),
                         preferred_element_type=F32)
    yf = jnp.dot(fy_ref[...], fp_ref[...], preferred_element_type=F32)
    y = (ga * ya + gf * yf).astype(BF16)
    acc_scr[...] += jnp.dot(y, wo_ref[...], preferred_element_type=F32)

    @pl.when(j == pl.num_programs(1) - 1)
    def _():
        _gated_norm_residual(x_ref, acc_scr, o_ref, qg_ref, gt_ref, 1.0)


def _mix_out(x, pre_g, shift, scale, gate, post_g, ao, fy, gate_w, gate_b, attn_proj,
             fnet_proj, w_out, layer):
    s, d = x.shape
    nc = d // MIX_TC
    vec = pl.BlockSpec((1, d), lambda i, j: (0, 0))
    return pl.pallas_call(
        _mixout_kernel,
        grid=(s // ROW_TILE, nc),
        in_specs=[
            pl.BlockSpec((ROW_TILE, d), lambda i, j: (i, 0)),
            vec, vec, vec, vec, vec,
            pl.BlockSpec((ATTN_WIDTH, ROW_TILE), lambda i, j: (0, i)),
            pl.BlockSpec((ROW_TILE, FNET_WIDTH), lambda i, j: (i, 0)),
            pl.BlockSpec((None, d, MIX_TC), lambda i, j: (layer, 0, j)),
            pl.BlockSpec((None, d, MIX_TC), lambda i, j: (layer, 0, nc + j)),
            pl.BlockSpec((1, MIX_TC), lambda i, j: (0, j)),
            pl.BlockSpec((1, MIX_TC), lambda i, j: (0, nc + j)),
            pl.BlockSpec((None, ATTN_WIDTH, MIX_TC), lambda i, j: (layer, 0, j)),
            pl.BlockSpec((None, FNET_WIDTH, MIX_TC), lambda i, j: (layer, 0, j)),
            pl.BlockSpec((None, MIX_TC, d), lambda i, j: (layer, j, 0)),
        ],
        out_specs=pl.BlockSpec((ROW_TILE, d), lambda i, j: (i, 0)),
        out_shape=jax.ShapeDtypeStruct((s, d), F32),
        scratch_shapes=[pltpu.VMEM((ROW_TILE, d), BF16), pltpu.VMEM((ROW_TILE, d), F32)],
        compiler_params=_cparams(("parallel", "arbitrary")),
        name="mix_out",
    )(x, pre_g, shift, scale, gate, post_g, ao, fy, gate_w, gate_w, gate_b, gate_b,
      attn_proj, fnet_proj, w_out)


def _rope_tables(seq):
    half = HEAD_DIM // 2
    pos = jnp.arange(seq, dtype=F32)
    inv_freq = ROPE_THETA ** (-jnp.arange(0, HEAD_DIM, 2, dtype=F32) / HEAD_DIM)
    ang = pos[:, None] * inv_freq[None, :]
    cos, sin = jnp.cos(ang), jnp.sin(ang)
    zero = jnp.zeros_like(sin)
    cos_t = jnp.concatenate([cos] * (LANES // half), axis=1)
    sin_n = jnp.concatenate([-sin, zero] * (LANES // HEAD_DIM), axis=1)
    sin_p = jnp.concatenate([zero, sin] * (LANES // HEAD_DIM), axis=1)
    return cos_t, sin_n, sin_p


def kernel(x, c, ada_w, ada_b, pre_norm_g, post_norm_g, ffn1_w_in, ffn1_w_out, mix_w_in,
           lambda_qk, subln_g, attn_proj, fnet_proj, branch_gate_w, branch_gate_b, mix_w_out,
           ffn2_w_in, ffn2_w_out):
    b, s, d = x.shape
    assert b == 1 and s == FFT_N * FFT_N and s % ROW_TILE == 0
    n_layers = ada_w.shape[0]
    cos_t, sin_n, sin_p = _rope_tables(s)
    dft = _dft_tables()
    mod = _ada_mod(c, ada_w, ada_b).reshape(n_layers, 3, 3, 1, d)
    q_scale = HEAD_DIM ** -0.5 * math.log2(math.e)
    xs = x.reshape(s, d)
    ffn1_in, ffn1_out, ffn2_in, ffn2_out, mix_in_w, gate_w, attn_w, fnet_w, mix_out_w = (
        w.astype(BF16) for w in (ffn1_w_in, ffn1_w_out, ffn2_w_in, ffn2_w_out, mix_w_in,
                                 branch_gate_w, attn_proj, fnet_proj, mix_w_out))
    for l in range(n_layers):
        lambda_init = 0.8 - 0.6 * math.exp(-0.3 * l)
        pre = pre_norm_g[l].reshape(3, 1, d)
        post = post_norm_g[l].reshape(3, 1, d)
        xs = _ffn(xs, pre[0], mod[l, 0, 0], mod[l, 0, 1], mod[l, 0, 2], post[0],
                  ffn1_in, ffn1_out, l)
        qt, qn, k, kn, vt, u = _mix_in(xs, pre[1], mod[l, 1, 0], mod[l, 1, 1], mix_in_w, l,
                              cos_t, sin_n, sin_p, q_scale)
        ao = _attention(qt, qn, k, kn, vt, lambda_qk[l], subln_g[l], lambda_init)
        fy = _fourier_mix(u, dft)
        xs = _mix_out(xs, pre[1], mod[l, 1, 0], mod[l, 1, 1], mod[l, 1, 2], post[1], ao, fy,
                      gate_w, branch_gate_b[l].reshape(1, 2 * d), attn_w, fnet_w, mix_out_w, l)
        xs = _ffn(xs, pre[2], mod[l, 2, 0], mod[l, 2, 1], mod[l, 2, 2], post[2],
                  ffn2_in, ffn2_out, l)
    return xs.reshape(b, s, d)
```

```python
import functools
import math

import numpy as np
import jax
import jax.numpy as jnp
from jax import lax
from jax.experimental import pallas as pl
from jax.experimental.pallas import tpu as pltpu

F32 = jnp.float32
BF16 = jnp.bfloat16

N_HEADS = 8
HEAD_DIM = 64
V_DIM = 2 * HEAD_DIM
ACC_ROWS = V_DIM + 8
K_LANES = 2 * V_DIM
ATTN_WIDTH = N_HEADS * V_DIM
N_GROUPS = 8
GROUP_DIM = 128
FNET_WIDTH = N_GROUPS * GROUP_DIM
ROPE_THETA = 10000.0
NORM_EPS = 1e-6
SUBLN_EPS = 1e-5
MACARON_WEIGHT = 0.5

LANES = 128
SUBLANES = 8
NORM_ROWS = 16
MXU_DIM = 256
VMEM_LIMIT_BYTES = 56 * 1024 * 1024

ADA_TN = 1024
ROW_TILE = 512
FFN_ROWS = 1024
FFN_TF = 512
EDGE_ROWS = 256
MIX_TC = 512
ATTN_TQ = 256
ATTN_TK = ROW_TILE
ATTN_SUB = 2
ATTN_QK = 512
ATTN_TS = 32
SHIFT_LIMIT = 60.0
BOUND_MARGIN = 1.0 + 2.0 ** -7
FFT_N = 128
FFT_NB = 4
FFT_KB = 4


def _cparams(sem):
    return pltpu.CompilerParams(dimension_semantics=sem, vmem_limit_bytes=VMEM_LIMIT_BYTES)


def _modulated_norm(x_ref, h_ref, g_ref, scale_ref, shift_ref):
    gs = g_ref[...] * (1.0 + scale_ref[...])
    shift = shift_ref[...]
    for r in range(0, x_ref.shape[0], NORM_ROWS):
        x = x_ref[r:r + NORM_ROWS, :]
        ms = jnp.mean(x * x, axis=-1, keepdims=True)
        h_ref[r:r + NORM_ROWS, :] = (x * lax.rsqrt(ms + NORM_EPS) * gs + shift).astype(h_ref.dtype)


def _gated_norm_residual(x_ref, y_ref, o_ref, g_ref, gate_ref, weight):
    coef = g_ref[...] * (weight * gate_ref[...])
    for r in range(0, x_ref.shape[0], NORM_ROWS):
        y = y_ref[r:r + NORM_ROWS, :]
        ms = jnp.mean(y * y, axis=-1, keepdims=True)
        o_ref[r:r + NORM_ROWS, :] = x_ref[r:r + NORM_ROWS, :] + y * lax.rsqrt(ms + NORM_EPS) * coef


def _ada_kernel(c_ref, w_ref, b_ref, o_ref):
    d, tn = w_ref.shape

    def body(k, acc):
        r = pl.multiple_of(k * SUBLANES, SUBLANES)
        cb = c_ref[pl.ds(r, SUBLANES), :]
        cb = cb * jax.nn.sigmoid(cb)
        return acc + w_ref[pl.ds(r, SUBLANES), :] * jnp.concatenate([cb] * (tn // LANES), axis=1)

    acc = lax.fori_loop(0, d // SUBLANES, body, jnp.zeros((SUBLANES, tn), F32), unroll=8)
    o_ref[...] = jnp.sum(acc, axis=0, keepdims=True) + b_ref[...]


def _ada_mod(c, ada_w, ada_b):
    n_layers, d, n = ada_w.shape
    c_b = jnp.broadcast_to(c.reshape(d, 1), (d, LANES))
    out = pl.pallas_call(
        _ada_kernel,
        grid=(n_layers, n // ADA_TN),
        in_specs=[
            pl.BlockSpec((d, LANES), lambda l, j: (0, 0)),
            pl.BlockSpec((None, d, ADA_TN), lambda l, j: (l, 0, j)),
            pl.BlockSpec((None, 1, ADA_TN), lambda l, j: (l, 0, j)),
        ],
        out_specs=pl.BlockSpec((None, 1, ADA_TN), lambda l, j: (l, 0, j)),
        out_shape=jax.ShapeDtypeStruct((n_layers, 1, n), F32),
        compiler_params=_cparams(("parallel", "parallel")),
        name="ada_mod",
    )(c_b, ada_w, ada_b.reshape(n_layers, 1, n))
    return out


def _ffn_kernel(x_ref, pg_ref, sh_ref, sc_ref, gt_ref, qg_ref, wg_ref, wu_ref, wo_ref,
                o_ref, h_scr):
    j = pl.program_id(1)
    last = pl.num_programs(1) - 1

    def up_down(rows):
        h = h_scr[rows, :]
        g = jnp.dot(h, wg_ref[...], preferred_element_type=F32)
        u = jnp.dot(h, wu_ref[...], preferred_element_type=F32)
        a = (g * jax.nn.sigmoid(g) * u).astype(BF16)
        return jnp.dot(a, wo_ref[...], preferred_element_type=F32)

    blocks = [slice(r, r + EDGE_ROWS) for r in range(0, x_ref.shape[0], EDGE_ROWS)]

    @pl.when(j == 0)
    def _():
        for rows in blocks:
            _modulated_norm(x_ref.at[rows], h_scr.at[rows], pg_ref, sc_ref, sh_ref)
            o_ref[rows, :] = up_down(rows)

    @pl.when(jnp.logical_and(j > 0, j < last))
    def _():
        o_ref[...] += up_down(slice(None))

    @pl.when(j == last)
    def _():
        for rows in blocks:
            o_ref[rows, :] += up_down(rows)
            _gated_norm_residual(x_ref.at[rows], o_ref.at[rows], o_ref.at[rows], qg_ref, gt_ref,
                                 MACARON_WEIGHT)


def _ffn(x, pre_g, shift, scale, gate, post_g, w_in, w_out, layer):
    s, d = x.shape
    f = w_out.shape[1]
    nf = f // FFN_TF
    vec = pl.BlockSpec((1, d), lambda i, j: (0, 0))
    return pl.pallas_call(
        _ffn_kernel,
        grid=(s // FFN_ROWS, nf),
        in_specs=[
            pl.BlockSpec((FFN_ROWS, d), lambda i, j: (i, 0)),
            vec, vec, vec, vec, vec,
            pl.BlockSpec((None, d, FFN_TF), lambda i, j: (layer, 0, j)),
            pl.BlockSpec((None, d, FFN_TF), lambda i, j: (layer, 0, nf + j)),
            pl.BlockSpec((None, FFN_TF, d), lambda i, j: (layer, j, 0)),
        ],
        out_specs=pl.BlockSpec((FFN_ROWS, d), lambda i, j: (i, 0)),
        out_shape=jax.ShapeDtypeStruct((s, d), F32),
        scratch_shapes=[pltpu.VMEM((FFN_ROWS, d), BF16)],
        compiler_params=_cparams(("parallel", "arbitrary")),
        name="ffn",
    )(x, pre_g, shift, scale, gate, post_g, w_in, w_in, w_out)


def _mixin_kernel(x_ref, pg_ref, sh_ref, sc_ref, w_ref, cos_ref, sinn_ref, sinp_ref,
                  qt_ref, qn_ref, k_ref, kn_ref, vt_ref, u_ref, h_scr, *, q_scale):
    _modulated_norm(x_ref, h_scr, pg_ref, sc_ref, sh_ref)
    h = h_scr[...]
    cos_t, sin_n, sin_p = cos_ref[...], sinn_ref[...], sinp_ref[...]

    def rope(zs):
        return (zs * cos_t + pltpu.roll(zs, LANES - HEAD_DIM // 2, 1) * sin_n
                + pltpu.roll(zs, HEAD_DIM // 2, 1) * sin_p)

    lane = lax.broadcasted_iota(jnp.int32, (x_ref.shape[0], LANES), 1)

    def max_sq_norms(t):
        t2 = t.astype(F32) * t.astype(F32)
        n1 = jnp.max(jnp.sum(jnp.where(lane < HEAD_DIM, t2, 0.0), axis=1, keepdims=True),
                     axis=0, keepdims=True)
        n2 = jnp.max(jnp.sum(jnp.where(lane < HEAD_DIM, 0.0, t2), axis=1, keepdims=True),
                     axis=0, keepdims=True)
        return jnp.where(lane[:1] < HEAD_DIM, n1, n2)

    zq = jnp.dot(h, w_ref[:, 0:ATTN_WIDTH], preferred_element_type=F32)
    for hd in range(N_HEADS):
        qr = (rope(zq[:, hd * V_DIM:(hd + 1) * V_DIM]) * q_scale).astype(BF16)
        qt_ref[hd] = qr.astype(F32).T.astype(BF16)
        qn_ref[hd:hd + 1, :] = max_sq_norms(qr)
    zk = jnp.dot(h, w_ref[:, ATTN_WIDTH:2 * ATTN_WIDTH], preferred_element_type=F32)
    shift_lanes = jnp.where(lane == 0, 1.0, 0.0).astype(BF16)
    for hd in range(N_HEADS):
        kr = rope(zk[:, hd * V_DIM:(hd + 1) * V_DIM]).astype(BF16)
        k_ref[hd, :, :V_DIM] = kr
        k_ref[hd, :, V_DIM:] = shift_lanes
        kn_ref[hd:hd + 1, :] = max_sq_norms(kr)
    zv = jnp.dot(h, w_ref[:, 2 * ATTN_WIDTH:3 * ATTN_WIDTH], preferred_element_type=F32)
    zvt = zv.T
    for hd in range(N_HEADS):
        vt_ref[hd] = zvt[hd * V_DIM:(hd + 1) * V_DIM, :].astype(BF16)
    u_ref[...] = jnp.dot(h, w_ref[:, 3 * ATTN_WIDTH:], preferred_element_type=F32)


def _mix_in(x, pre_g, shift, scale, w, layer, cos_t, sin_n, sin_p, q_scale):
    s, d = x.shape
    n_tiles = s // ROW_TILE
    vec = pl.BlockSpec((1, d), lambda i: (0, 0))
    tab = pl.BlockSpec((ROW_TILE, LANES), lambda i: (i, 0))
    return pl.pallas_call(
        functools.partial(_mixin_kernel, q_scale=q_scale),
        grid=(n_tiles,),
        in_specs=[
            pl.BlockSpec((ROW_TILE, d), lambda i: (i, 0)),
            vec, vec, vec,
            pl.BlockSpec((None,) + w.shape[1:], lambda i: (layer, 0, 0), pipeline_mode=pl.Buffered(1)),
            tab, tab, tab,
        ],
        out_specs=[
            pl.BlockSpec((N_HEADS, None, V_DIM, ROW_TILE), lambda i: (0, i, 0, 0)),
            pl.BlockSpec((None, N_HEADS, LANES), lambda i: (i, 0, 0)),
            pl.BlockSpec((N_HEADS, ROW_TILE, K_LANES), lambda i: (0, i, 0)),
            pl.BlockSpec((None, N_HEADS, LANES), lambda i: (i, 0, 0)),
            pl.BlockSpec((N_HEADS, None, V_DIM, ROW_TILE), lambda i: (0, i, 0, 0)),
            pl.BlockSpec((ROW_TILE, FNET_WIDTH), lambda i: (i, 0)),
        ],
        out_shape=[
            jax.ShapeDtypeStruct((N_HEADS, n_tiles, V_DIM, ROW_TILE), BF16),
            jax.ShapeDtypeStruct((n_tiles, N_HEADS, LANES), F32),
            jax.ShapeDtypeStruct((N_HEADS, s, K_LANES), BF16),
            jax.ShapeDtypeStruct((n_tiles, N_HEADS, LANES), F32),
            jax.ShapeDtypeStruct((N_HEADS, n_tiles, V_DIM, ROW_TILE), BF16),
            jax.ShapeDtypeStruct((s, FNET_WIDTH), F32),
        ],
        scratch_shapes=[pltpu.VMEM((ROW_TILE, d), BF16)],
        compiler_params=_cparams(("parallel",)),
        name="mix_in",
    )(x, pre_g, shift, scale, w, cos_t, sin_n, sin_p)


def _sublane_partial_sum(p):
    return p.reshape(p.shape[0] // SUBLANES, SUBLANES, p.shape[1]).sum(axis=0)


def _attn_kernel(qb_ref, kb_ref, lq_ref, g_ref, qt_ref, k_ref, vt_ref, o_ref,
                 qz_scr, s0_scr, s1_scr, c0_scr, c1_scr, acc_scr, m_scr, *, lambda_init):
    tq = qt_ref.shape[1]
    n_kv, _, tk = vt_ref.shape
    n_col = 2 * tq // MXU_DIM
    hd = pl.program_id(0)
    tile = lax.div(pl.program_id(1), ROW_TILE // tq)
    bsq = [qb_ref[tile, 2 * hd + m] * kb_ref[0, 2 * hd + m] for m in range(2)]
    fast = jnp.maximum(bsq[0], bsq[1]) <= (SHIFT_LIMIT / BOUND_MARGIN) ** 2
    shift = jnp.sqrt(jnp.concatenate(
        [jnp.full((1, tq), jnp.where(fast, b, 0.0), F32) for b in bsq], axis=1)) * BOUND_MARGIN
    qt = qt_ref[...].astype(F32)
    row = lax.broadcasted_iota(jnp.int32, (V_DIM, 2 * tq), 0)
    qz_scr[:V_DIM, :] = jnp.where((row < HEAD_DIM) == (lax.broadcasted_iota(
        jnp.int32, (V_DIM, 2 * tq), 1) < tq), jnp.concatenate([qt, qt], axis=1), 0.0).astype(BF16)
    qz_scr[V_DIM:, :] = jnp.where(row == 0, -shift, 0.0).astype(BF16)
    acc_scr[...] = jnp.zeros_like(acc_scr)

    @pl.when(fast)
    def _():
        span = ATTN_TS * ATTN_QK
        last = n_kv * tk - ATTN_QK

        def shifted_scores(r):
            return jnp.dot(k_ref[pl.ds(pl.multiple_of(r, ATTN_QK), ATTN_QK), :], qz_scr[...],
                           preferred_element_type=F32)

        s0_scr[:ATTN_QK, :] = shifted_scores(0)

        def body(t, carry):
            r0 = t * span
            pv = [None] * n_col
            psum = [jnp.zeros((SUBLANES, MXU_DIM), F32)] * n_col
            s_next = None
            for kb in range(ATTN_TS):
                s = s0_scr[:ATTN_QK, :] if kb == 0 else s_next
                s_next = shifted_scores(jnp.minimum(r0 + (kb + 1) * ATTN_QK, last))
                for sub in range(ATTN_QK // MXU_DIM):
                    blk, off = divmod(kb * ATTN_QK + sub * MXU_DIM, tk)
                    rows = slice(sub * MXU_DIM, (sub + 1) * MXU_DIM)
                    for nh in range(n_col):
                        p = jnp.exp2(s[rows, nh * MXU_DIM:(nh + 1) * MXU_DIM])
                        psum[nh] = psum[nh] + _sublane_partial_sum(p)
                        d = jnp.dot(vt_ref[t * (span // tk) + blk, :, off:off + MXU_DIM],
                                    p.astype(BF16), preferred_element_type=F32)
                        pv[nh] = d if pv[nh] is None else pv[nh] + d
            s0_scr[:ATTN_QK, :] = s_next
            for nh in range(n_col):
                cols = slice(nh * MXU_DIM, (nh + 1) * MXU_DIM)
                acc_scr[:V_DIM, cols] = acc_scr[:V_DIM, cols] + pv[nh]
                acc_scr[V_DIM:, cols] = acc_scr[V_DIM:, cols] + psum[nh]
            return carry

        lax.fori_loop(0, n_kv * tk // span, body, 0)

    @pl.when(jnp.logical_not(fast))
    def _():
        n_chunks = n_kv // ATTN_SUB
        ck = ATTN_SUB * tk
        m_scr[...] = jnp.full_like(m_scr, -jnp.inf)

        def scores(j, s_scr, c_scr):
            r = pl.multiple_of(j * ck, ck)
            s = jnp.dot(k_ref[pl.ds(r, ck), :], qz_scr[...],
                        preferred_element_type=F32)
            s_scr[...] = s
            c_scr[...] = jnp.max(s, axis=0, keepdims=True)

        def consume(j, s_scr, c_scr):
            m_prev = m_scr[...]
            m_new = jnp.maximum(m_prev, c_scr[...])
            alpha = jnp.exp2(m_prev - m_new)
            for nh in range(n_col):
                cols = slice(nh * MXU_DIM, (nh + 1) * MXU_DIM)
                pv = None
                psum = jnp.zeros((SUBLANES, MXU_DIM), F32)
                for kb in range(ck // MXU_DIM):
                    p = jnp.exp2(s_scr[kb * MXU_DIM:(kb + 1) * MXU_DIM, cols] - m_new[:, cols])
                    psum = psum + _sublane_partial_sum(p)
                    blk, off = divmod(kb * MXU_DIM, tk)
                    d = jnp.dot(vt_ref[j * ATTN_SUB + blk, :, off:off + MXU_DIM], p.astype(BF16),
                                preferred_element_type=F32)
                    pv = d if pv is None else pv + d
                acc_scr[:V_DIM, cols] = acc_scr[:V_DIM, cols] * alpha[:, cols] + pv
                acc_scr[V_DIM:, cols] = acc_scr[V_DIM:, cols] * alpha[:, cols] + psum
            m_scr[...] = m_new

        scores(0, s0_scr, c0_scr)

        def body(i, carry):
            j = 2 * i
            scores(j + 1, s1_scr, c1_scr)
            consume(j, s0_scr, c0_scr)
            scores(jnp.minimum(j + 2, n_chunks - 1), s0_scr, c0_scr)
            consume(j + 1, s1_scr, c1_scr)
            return carry

        lax.fori_loop(0, n_chunks // 2, body, 0)

    lq = lq_ref[...]
    lam = (jnp.exp(jnp.sum(lq[0:1] * lq[1:2], axis=-1, keepdims=True))
           - jnp.exp(jnp.sum(lq[2:3] * lq[3:4], axis=-1, keepdims=True)) + lambda_init)
    on = acc_scr[:V_DIM, :] / jnp.sum(acc_scr[V_DIM:, :], axis=0, keepdims=True)
    ot = on[:, :tq] - lam * on[:, tq:]
    ms = jnp.mean(ot * ot, axis=0, keepdims=True)
    o_ref[...] = (ot * lax.rsqrt(ms + SUBLN_EPS) * (g_ref[...] * (1.0 - lambda_init))).astype(BF16)


def _attention(qt, qn, k, kn, vt, lambda_qk, subln_g, lambda_init):
    n_heads, n_tiles, _, _ = qt.shape
    s = k.shape[1]
    n_kv = vt.shape[1]
    kmax = jnp.max(kn, axis=0)
    kb = jnp.stack([kmax[:, 0], kmax[:, HEAD_DIM]], axis=1).reshape(1, 2 * n_heads)
    qb = jnp.stack([qn[:, :, 0], qn[:, :, HEAD_DIM]], axis=2).reshape(n_tiles, 2 * n_heads)
    per_tile = ROW_TILE // ATTN_TQ
    return pl.pallas_call(
        functools.partial(_attn_kernel, lambda_init=lambda_init),
        grid=(n_heads, s // ATTN_TQ),
        in_specs=[
            pl.BlockSpec(memory_space=pltpu.SMEM),
            pl.BlockSpec(memory_space=pltpu.SMEM),
            pl.BlockSpec(lambda_qk.shape, lambda h, i: (0, 0)),
            pl.BlockSpec((V_DIM, 1), lambda h, i: (0, 0)),
            pl.BlockSpec((None, None, V_DIM, ATTN_TQ),
                         lambda h, i: (h, i // per_tile, 0, i % per_tile)),
            pl.BlockSpec((None, s, K_LANES), lambda h, i: (h, 0, 0)),
            pl.BlockSpec((None, n_kv, V_DIM, ATTN_TK), lambda h, i: (h, 0, 0, 0)),
        ],
        out_specs=pl.BlockSpec((V_DIM, ATTN_TQ), lambda h, i: (h, i)),
        out_shape=jax.ShapeDtypeStruct((n_heads * V_DIM, s), BF16),
        scratch_shapes=[pltpu.VMEM((K_LANES, 2 * ATTN_TQ), BF16),
                        pltpu.VMEM((ATTN_SUB * ATTN_TK, 2 * ATTN_TQ), F32),
                        pltpu.VMEM((ATTN_SUB * ATTN_TK, 2 * ATTN_TQ), F32),
                        pltpu.VMEM((1, 2 * ATTN_TQ), F32),
                        pltpu.VMEM((1, 2 * ATTN_TQ), F32),
                        pltpu.VMEM((ACC_ROWS, 2 * ATTN_TQ), F32),
                        pltpu.VMEM((1, 2 * ATTN_TQ), F32)],
        compiler_params=_cparams(("parallel", "parallel")),
        name="diff_attn",
    )(qb, kb, lambda_qk, subln_g.reshape(V_DIM, 1), qt, k, vt)


def _dft_tables():
    n = FFT_N
    jk = np.outer(np.arange(n), np.arange(n)) % n
    ang = 2.0 * np.pi * jk / n
    c = np.cos(ang) / math.sqrt(n)
    s = np.sin(ang) / math.sqrt(n)
    stage1 = np.concatenate([c, -s], axis=0)
    stage2 = np.block([[c, s], [-s, c]])
    chan = np.concatenate([c, s], axis=0)
    tw_ang = 2.0 * np.pi * np.outer(np.arange(n), np.arange(n)) / (n * n)
    return (_split(jnp.asarray(stage1, F32)), _split(jnp.asarray(stage2, F32)),
            _split(jnp.asarray(chan, F32)),
            jnp.asarray(np.cos(tw_ang), F32), jnp.asarray(np.sin(tw_ang), F32))


def _split(a):
    hi = a.astype(BF16)
    return hi, (a - hi.astype(F32)).astype(BF16)


def _dot3(a, b):
    (a_hi, a_lo), (b_hi, b_lo) = a, b
    return (jnp.dot(a_hi, b_hi, preferred_element_type=F32)
            + jnp.dot(a_hi, b_lo, preferred_element_type=F32)
            + jnp.dot(a_lo, b_hi, preferred_element_type=F32))


def _fft1_kernel(x_ref, fh_ref, fl_ref, twc_ref, tws_ref, o_ref):
    n = FFT_N
    t = _dot3((fh_ref[...], fl_ref[...]), _split(x_ref[...]))
    width = x_ref.shape[1] // FFT_NB
    for b in range(FFT_NB):
        tr = t[:n, b * width:(b + 1) * width]
        ti = t[n:, b * width:(b + 1) * width]
        c = jnp.concatenate([twc_ref[b]] * (width // LANES), axis=1)
        s = jnp.concatenate([tws_ref[b]] * (width // LANES), axis=1)
        o_ref[0, :, b * width:(b + 1) * width] = tr * c + ti * s
        o_ref[1, :, b * width:(b + 1) * width] = ti * c - tr * s


def _fft2_kernel(t_ref, fh_ref, fl_ref, chh_ref, chl_ref, o_ref):
    n = FFT_N
    width = t_ref.shape[3]
    f = (fh_ref[...], fl_ref[...])
    ch = (chh_ref[...], chl_ref[...])
    for b in range(FFT_KB):
        tt = jnp.concatenate([t_ref[0, b], t_ref[1, b]], axis=0)
        z = _dot3(f, _split(tt))
        for g in range(N_GROUPS):
            zz = jnp.concatenate([z[:n, g * GROUP_DIM:(g + 1) * GROUP_DIM],
                                  z[n:, g * GROUP_DIM:(g + 1) * GROUP_DIM]], axis=1)
            y = _dot3(_split(zz), ch)
            o_ref[:, b * width + g * GROUP_DIM:b * width + (g + 1) * GROUP_DIM] = y.astype(o_ref.dtype)


def _fourier_mix(u, tables):
    s, width = u.shape
    n = FFT_N
    stage1, stage2, chan, twc, tws = tables
    twc_b = jnp.broadcast_to(twc[:, :, None], (n, n, LANES))
    tws_b = jnp.broadcast_to(tws[:, :, None], (n, n, LANES))
    t = pl.pallas_call(
        _fft1_kernel,
        grid=(n // FFT_NB,),
        in_specs=[
            pl.BlockSpec((n, FFT_NB * width), lambda j: (0, j)),
            pl.BlockSpec((2 * n, n), lambda j: (0, 0)),
            pl.BlockSpec((2 * n, n), lambda j: (0, 0)),
            pl.BlockSpec((FFT_NB, n, LANES), lambda j: (j, 0, 0)),
            pl.BlockSpec((FFT_NB, n, LANES), lambda j: (j, 0, 0)),
        ],
        out_specs=pl.BlockSpec((2, n, FFT_NB * width), lambda j: (0, 0, j)),
        out_shape=jax.ShapeDtypeStruct((2, n, n * width), F32),
        compiler_params=_cparams(("parallel",)),
        name="fft_stage1",
    )(u.reshape(n, n * width), *stage1, twc_b, tws_b)
    y = pl.pallas_call(
        _fft2_kernel,
        grid=(n // FFT_KB,),
        in_specs=[
            pl.BlockSpec((2, FFT_KB, n, width), lambda i: (0, i, 0, 0)),
            pl.BlockSpec((2 * n, 2 * n), lambda i: (0, 0)),
            pl.BlockSpec((2 * n, 2 * n), lambda i: (0, 0)),
            pl.BlockSpec((2 * n, n), lambda i: (0, 0)),
            pl.BlockSpec((2 * n, n), lambda i: (0, 0)),
        ],
        out_specs=pl.BlockSpec((n, FFT_KB * width), lambda i: (0, i)),
        out_shape=jax.ShapeDtypeStruct((n, n * width), BF16),
        compiler_params=_cparams(("parallel",)),
        name="fft_stage2",
    )(t.reshape(2, n, n, width), *stage2, *chan)
    return y.reshape(s, width)


def _mixout_kernel(x_ref, pg_ref, sh_ref, sc_ref, gt_ref, qg_ref, ao_ref, fy_ref,
                   gwa_ref, gwf_ref, gba_ref, gbf_ref, ap_ref, fp_ref, wo_ref,
                   o_ref, h_scr):
    j = pl.program_id(1)
    last = pl.num_programs(1) - 1

    def merged_out(rows):
        h = h_scr[rows, :]
        ga = jax.nn.sigmoid(jnp.dot(h, gwa_ref[...], preferred_element_type=F32) + gba_ref[...])
        gf = jax.nn.sigmoid(jnp.dot(h, gwf_ref[...], preferred_element_type=F32) + gbf_ref[...])
        ya = lax.dot_general(ao_ref[:, rows], ap_ref[...], (((0,), (0,)), ((), ())),
                             preferred_element_type=F32)
        yf = jnp.dot(fy_ref[rows, :], fp_ref[...], preferred_element_type=F32)
        y = (ga * ya + gf * yf).astype(BF16)
        return jnp.dot(y, wo_ref[...], preferred_element_type=F32)

    blocks = [slice(r, r + EDGE_ROWS) for r in range(0, x_ref.shape[0], EDGE_ROWS)]

    @pl.when(j == 0)
    def _():
        for rows in blocks:
            _modulated_norm(x_ref.at[rows], h_scr.at[rows], pg_ref, sc_ref, sh_ref)
            o_ref[rows, :] = merged_out(rows)

    @pl.when(jnp.logical_and(j > 0, j < last))
    def _():
        o_ref[...] += merged_out(slice(None))

    @pl.when(j == last)
    def _():
        for rows in blocks:
            o_ref[rows, :] += merged_out(rows)
            _gated_norm_residual(x_ref.at[rows], o_ref.at[rows], o_ref.at[rows], qg_ref, gt_ref, 1.0)


def _mix_out(x, pre_g, shift, scale, gate, post_g, ao, fy, gate_w, gate_b, attn_proj,
             fnet_proj, w_out, layer):
    s, d = x.shape
    nc = d // MIX_TC
    vec = pl.BlockSpec((1, d), lambda i, j: (0, 0))
    return pl.pallas_call(
        _mixout_kernel,
        grid=(s // ROW_TILE, nc),
        in_specs=[
            pl.BlockSpec((ROW_TILE, d), lambda i, j: (i, 0)),
            vec, vec, vec, vec, vec,
            pl.BlockSpec((ATTN_WIDTH, ROW_TILE), lambda i, j: (0, i)),
            pl.BlockSpec((ROW_TILE, FNET_WIDTH), lambda i, j: (i, 0)),
            pl.BlockSpec((None, d, MIX_TC), lambda i, j: (layer, 0, j)),
            pl.BlockSpec((None, d, MIX_TC), lambda i, j: (layer, 0, nc + j)),
            pl.BlockSpec((1, MIX_TC), lambda i, j: (0, j)),
            pl.BlockSpec((1, MIX_TC), lambda i, j: (0, nc + j)),
            pl.BlockSpec((None, ATTN_WIDTH, MIX_TC), lambda i, j: (layer, 0, j)),
            pl.BlockSpec((None, FNET_WIDTH, MIX_TC), lambda i, j: (layer, 0, j)),
            pl.BlockSpec((None, MIX_TC, d), lambda i, j: (layer, j, 0)),
        ],
        out_specs=pl.BlockSpec((ROW_TILE, d), lambda i, j: (i, 0)),
        out_shape=jax.ShapeDtypeStruct((s, d), F32),
        scratch_shapes=[pltpu.VMEM((ROW_TILE, d), BF16)],
        compiler_params=_cparams(("parallel", "arbitrary")),
        name="mix_out",
    )(x, pre_g, shift, scale, gate, post_g, ao, fy, gate_w, gate_w, gate_b, gate_b,
      attn_proj, fnet_proj, w_out)


def _rope_tables(seq):
    half = HEAD_DIM // 2
    pos = jnp.arange(seq, dtype=F32)
    inv_freq = ROPE_THETA ** (-jnp.arange(0, HEAD_DIM, 2, dtype=F32) / HEAD_DIM)
    ang = pos[:, None] * inv_freq[None, :]
    cos, sin = jnp.cos(ang), jnp.sin(ang)
    zero = jnp.zeros_like(sin)
    cos_t = jnp.concatenate([cos] * (LANES // half), axis=1)
    sin_n = jnp.concatenate([-sin, zero] * (LANES // HEAD_DIM), axis=1)
    sin_p = jnp.concatenate([zero, sin] * (LANES // HEAD_DIM), axis=1)
    return cos_t, sin_n, sin_p


def kernel(x, c, ada_w, ada_b, pre_norm_g, post_norm_g, ffn1_w_in, ffn1_w_out, mix_w_in,
           lambda_qk, subln_g, attn_proj, fnet_proj, branch_gate_w, branch_gate_b, mix_w_out,
           ffn2_w_in, ffn2_w_out):
    b, s, d = x.shape
    assert b == 1 and s == FFT_N * FFT_N and s % FFN_ROWS == 0 and s % (ATTN_TS * ATTN_QK) == 0
    n_layers = ada_w.shape[0]
    cos_t, sin_n, sin_p = _rope_tables(s)
    dft = _dft_tables()
    mod = _ada_mod(c, ada_w, ada_b).reshape(n_layers, 3, 3, 1, d)
    q_scale = HEAD_DIM ** -0.5 * math.log2(math.e)
    xs = x.reshape(s, d)
    ffn1_in, ffn1_out, ffn2_in, ffn2_out, mix_in_w, gate_w, attn_w, fnet_w, mix_out_w = (
        w.astype(BF16) for w in (ffn1_w_in, ffn1_w_out, ffn2_w_in, ffn2_w_out, mix_w_in,
                                 branch_gate_w, attn_proj, fnet_proj, mix_w_out))
    for l in range(n_layers):
        lambda_init = 0.8 - 0.6 * math.exp(-0.3 * l)
        pre = pre_norm_g[l].reshape(3, 1, d)
        post = post_norm_g[l].reshape(3, 1, d)
        xs = _ffn(xs, pre[0], mod[l, 0, 0], mod[l, 0, 1], mod[l, 0, 2], post[0],
                  ffn1_in, ffn1_out, l)
        qt, qn, k, kn, vt, u = _mix_in(xs, pre[1], mod[l, 1, 0], mod[l, 1, 1], mix_in_w, l,
                              cos_t, sin_n, sin_p, q_scale)
        ao = _attention(qt, qn, k, kn, vt, lambda_qk[l], subln_g[l], lambda_init)
        fy = _fourier_mix(u, dft)
        xs = _mix_out(xs, pre[1], mod[l, 1, 0], mod[l, 1, 1], mod[l, 1, 2], post[1], ao, fy,
                      gate_w, branch_gate_b[l].reshape(1, 2 * d), attn_w, fnet_w, mix_out_w, l)
        xs = _ffn(xs, pre[2], mod[l, 2, 0], mod[l, 2, 1], mod[l, 2, 2], post[2],
                  ffn2_in, ffn2_out, l)
    return xs.reshape(b, s, d)
```

```python
import functools
import math

import numpy as np
import jax
import jax.numpy as jnp
from jax import lax
from jax.experimental import pallas as pl
from jax.experimental.pallas import tpu as pltpu

F32 = jnp.float32
BF16 = jnp.bfloat16

N_HEADS = 8
HEAD_DIM = 64
V_DIM = 2 * HEAD_DIM
ACC_ROWS = V_DIM + 8
K_LANES = 2 * V_DIM
ATTN_WIDTH = N_HEADS * V_DIM
N_GROUPS = 8
GROUP_DIM = 128
FNET_WIDTH = N_GROUPS * GROUP_DIM
ROPE_THETA = 10000.0
NORM_EPS = 1e-6
SUBLN_EPS = 1e-5
MACARON_WEIGHT = 0.5

LANES = 128
SUBLANES = 8
NORM_ROWS = 16
MXU_DIM = 256
VMEM_LIMIT_BYTES = 56 * 1024 * 1024

ADA_TN = 1024
ROW_TILE = 512
FFN_ROWS = 1024
FFN_TF = 512
EDGE_ROWS = 256
MIX_TC = 512
ATTN_TQ = 256
ATTN_TK = ROW_TILE
ATTN_SUB = 2
ATTN_QK = 512
SHIFT_LIMIT = 60.0
BOUND_MARGIN = 1.0 + 2.0 ** -7
FFT_N = 128
FFT_NB = 4
FFT_KB = 4


def _cparams(sem):
    return pltpu.CompilerParams(dimension_semantics=sem, vmem_limit_bytes=VMEM_LIMIT_BYTES)


def _modulated_norm(x_ref, h_ref, g_ref, scale_ref, shift_ref):
    gs = g_ref[...] * (1.0 + scale_ref[...])
    shift = shift_ref[...]
    for r in range(0, x_ref.shape[0], NORM_ROWS):
        x = x_ref[r:r + NORM_ROWS, :]
        ms = jnp.mean(x * x, axis=-1, keepdims=True)
        h_ref[r:r + NORM_ROWS, :] = (x * lax.rsqrt(ms + NORM_EPS) * gs + shift).astype(h_ref.dtype)


def _gated_norm_residual(x_ref, y_ref, o_ref, g_ref, gate_ref, weight):
    coef = g_ref[...] * (weight * gate_ref[...])
    for r in range(0, x_ref.shape[0], NORM_ROWS):
        y = y_ref[r:r + NORM_ROWS, :]
        ms = jnp.mean(y * y, axis=-1, keepdims=True)
        o_ref[r:r + NORM_ROWS, :] = x_ref[r:r + NORM_ROWS, :] + y * lax.rsqrt(ms + NORM_EPS) * coef


def _ada_kernel(c_ref, w_ref, b_ref, o_ref):
    d, tn = w_ref.shape

    def body(k, acc):
        r = pl.multiple_of(k * SUBLANES, SUBLANES)
        cb = c_ref[pl.ds(r, SUBLANES), :]
        cb = cb * jax.nn.sigmoid(cb)
        return acc + w_ref[pl.ds(r, SUBLANES), :] * jnp.concatenate([cb] * (tn // LANES), axis=1)

    acc = lax.fori_loop(0, d // SUBLANES, body, jnp.zeros((SUBLANES, tn), F32), unroll=8)
    o_ref[...] = jnp.sum(acc, axis=0, keepdims=True) + b_ref[...]


def _ada_mod(c, ada_w, ada_b):
    n_layers, d, n = ada_w.shape
    c_b = jnp.broadcast_to(c.reshape(d, 1), (d, LANES))
    out = pl.pallas_call(
        _ada_kernel,
        grid=(n_layers, n // ADA_TN),
        in_specs=[
            pl.BlockSpec((d, LANES), lambda l, j: (0, 0)),
            pl.BlockSpec((None, d, ADA_TN), lambda l, j: (l, 0, j)),
            pl.BlockSpec((None, 1, ADA_TN), lambda l, j: (l, 0, j)),
        ],
        out_specs=pl.BlockSpec((None, 1, ADA_TN), lambda l, j: (l, 0, j)),
        out_shape=jax.ShapeDtypeStruct((n_layers, 1, n), F32),
        compiler_params=_cparams(("parallel", "parallel")),
        name="ada_mod",
    )(c_b, ada_w, ada_b.reshape(n_layers, 1, n))
    return out


def _ffn_kernel(x_ref, pg_ref, sh_ref, sc_ref, gt_ref, qg_ref, wg_ref, wu_ref, wo_ref,
                o_ref, h_scr):
    j = pl.program_id(1)
    last = pl.num_programs(1) - 1

    def up_down(rows):
        h = h_scr[rows, :]
        g = jnp.dot(h, wg_ref[...], preferred_element_type=F32)
        u = jnp.dot(h, wu_ref[...], preferred_element_type=F32)
        a = (g * jax.nn.sigmoid(g) * u).astype(BF16)
        return jnp.dot(a, wo_ref[...], preferred_element_type=F32)

    blocks = [slice(r, r + EDGE_ROWS) for r in range(0, x_ref.shape[0], EDGE_ROWS)]

    @pl.when(j == 0)
    def _():
        for rows in blocks:
            _modulated_norm(x_ref.at[rows], h_scr.at[rows], pg_ref, sc_ref, sh_ref)
            o_ref[rows, :] = up_down(rows)

    @pl.when(jnp.logical_and(j > 0, j < last))
    def _():
        o_ref[...] += up_down(slice(None))

    @pl.when(j == last)
    def _():
        for rows in blocks:
            o_ref[rows, :] += up_down(rows)
            _gated_norm_residual(x_ref.at[rows], o_ref.at[rows], o_ref.at[rows], qg_ref, gt_ref,
                                 MACARON_WEIGHT)


def _ffn(x, pre_g, shift, scale, gate, post_g, w_in, w_out, layer):
    s, d = x.shape
    f = w_out.shape[1]
    nf = f // FFN_TF
    vec = pl.BlockSpec((1, d), lambda i, j: (0, 0))
    return pl.pallas_call(
        _ffn_kernel,
        grid=(s // FFN_ROWS, nf),
        in_specs=[
            pl.BlockSpec((FFN_ROWS, d), lambda i, j: (i, 0)),
            vec, vec, vec, vec, vec,
            pl.BlockSpec((None, d, FFN_TF), lambda i, j: (layer, 0, j)),
            pl.BlockSpec((None, d, FFN_TF), lambda i, j: (layer, 0, nf + j)),
            pl.BlockSpec((None, FFN_TF, d), lambda i, j: (layer, j, 0)),
        ],
        out_specs=pl.BlockSpec((FFN_ROWS, d), lambda i, j: (i, 0)),
        out_shape=jax.ShapeDtypeStruct((s, d), F32),
        scratch_shapes=[pltpu.VMEM((FFN_ROWS, d), BF16)],
        compiler_params=_cparams(("parallel", "arbitrary")),
        name="ffn",
    )(x, pre_g, shift, scale, gate, post_g, w_in, w_in, w_out)


def _mixin_kernel(x_ref, pg_ref, sh_ref, sc_ref, w_ref, cos_ref, sinn_ref, sinp_ref,
                  qt_ref, qn_ref, k_ref, kn_ref, vt_ref, u_ref, h_scr, *, q_scale):
    _modulated_norm(x_ref, h_scr, pg_ref, sc_ref, sh_ref)
    h = h_scr[...]
    cos_t, sin_n, sin_p = cos_ref[...], sinn_ref[...], sinp_ref[...]

    def rope(zs):
        return (zs * cos_t + pltpu.roll(zs, LANES - HEAD_DIM // 2, 1) * sin_n
                + pltpu.roll(zs, HEAD_DIM // 2, 1) * sin_p)

    lane = lax.broadcasted_iota(jnp.int32, (x_ref.shape[0], LANES), 1)

    def max_sq_norms(t):
        t2 = t.astype(F32) * t.astype(F32)
        n1 = jnp.max(jnp.sum(jnp.where(lane < HEAD_DIM, t2, 0.0), axis=1, keepdims=True),
                     axis=0, keepdims=True)
        n2 = jnp.max(jnp.sum(jnp.where(lane < HEAD_DIM, 0.0, t2), axis=1, keepdims=True),
                     axis=0, keepdims=True)
        return jnp.where(lane[:1] < HEAD_DIM, n1, n2)

    zq = jnp.dot(h, w_ref[:, 0:ATTN_WIDTH], preferred_element_type=F32)
    for hd in range(N_HEADS):
        qr = (rope(zq[:, hd * V_DIM:(hd + 1) * V_DIM]) * q_scale).astype(BF16)
        qt_ref[hd] = qr.astype(F32).T.astype(BF16)
        qn_ref[hd:hd + 1, :] = max_sq_norms(qr)
    zk = jnp.dot(h, w_ref[:, ATTN_WIDTH:2 * ATTN_WIDTH], preferred_element_type=F32)
    shift_lanes = jnp.where(lane == 0, 1.0, 0.0).astype(BF16)
    for hd in range(N_HEADS):
        kr = rope(zk[:, hd * V_DIM:(hd + 1) * V_DIM]).astype(BF16)
        k_ref[hd, :, :V_DIM] = kr
        k_ref[hd, :, V_DIM:] = shift_lanes
        kn_ref[hd:hd + 1, :] = max_sq_norms(kr)
    zv = jnp.dot(h, w_ref[:, 2 * ATTN_WIDTH:3 * ATTN_WIDTH], preferred_element_type=F32)
    zvt = zv.T
    for hd in range(N_HEADS):
        vt_ref[hd] = zvt[hd * V_DIM:(hd + 1) * V_DIM, :].astype(BF16)
    u_ref[...] = jnp.dot(h, w_ref[:, 3 * ATTN_WIDTH:], preferred_element_type=F32)


def _mix_in(x, pre_g, shift, scale, w, layer, cos_t, sin_n, sin_p, q_scale):
    s, d = x.shape
    n_tiles = s // ROW_TILE
    vec = pl.BlockSpec((1, d), lambda i: (0, 0))
    tab = pl.BlockSpec((ROW_TILE, LANES), lambda i: (i, 0))
    return pl.pallas_call(
        functools.partial(_mixin_kernel, q_scale=q_scale),
        grid=(n_tiles,),
        in_specs=[
            pl.BlockSpec((ROW_TILE, d), lambda i: (i, 0)),
            vec, vec, vec,
            pl.BlockSpec((None,) + w.shape[1:], lambda i: (layer, 0, 0), pipeline_mode=pl.Buffered(1)),
            tab, tab, tab,
        ],
        out_specs=[
            pl.BlockSpec((N_HEADS, None, V_DIM, ROW_TILE), lambda i: (0, i, 0, 0)),
            pl.BlockSpec((None, N_HEADS, LANES), lambda i: (i, 0, 0)),
            pl.BlockSpec((N_HEADS, ROW_TILE, K_LANES), lambda i: (0, i, 0)),
            pl.BlockSpec((None, N_HEADS, LANES), lambda i: (i, 0, 0)),
            pl.BlockSpec((N_HEADS, None, V_DIM, ROW_TILE), lambda i: (0, i, 0, 0)),
            pl.BlockSpec((ROW_TILE, FNET_WIDTH), lambda i: (i, 0)),
        ],
        out_shape=[
            jax.ShapeDtypeStruct((N_HEADS, n_tiles, V_DIM, ROW_TILE), BF16),
            jax.ShapeDtypeStruct((n_tiles, N_HEADS, LANES), F32),
            jax.ShapeDtypeStruct((N_HEADS, s, K_LANES), BF16),
            jax.ShapeDtypeStruct((n_tiles, N_HEADS, LANES), F32),
            jax.ShapeDtypeStruct((N_HEADS, n_tiles, V_DIM, ROW_TILE), BF16),
            jax.ShapeDtypeStruct((s, FNET_WIDTH), F32),
        ],
        scratch_shapes=[pltpu.VMEM((ROW_TILE, d), BF16)],
        compiler_params=_cparams(("parallel",)),
        name="mix_in",
    )(x, pre_g, shift, scale, w, cos_t, sin_n, sin_p)


def _sublane_partial_sum(p):
    return p.reshape(p.shape[0] // SUBLANES, SUBLANES, p.shape[1]).sum(axis=0)


def _attn_kernel(qb_ref, kb_ref, lq_ref, g_ref, qt_ref, k_ref, vt_ref, o_ref,
                 qz_scr, s0_scr, s1_scr, c0_scr, c1_scr, acc_scr, m_scr, *, lambda_init):
    tq = ATTN_TQ
    n_sub = qt_ref.shape[1] // tq
    n_kv, _, tk = vt_ref.shape
    n_keys = n_kv * tk
    n_col = 2 * tq // MXU_DIM
    hd = pl.program_id(0)
    tile = pl.program_id(1)
    bsq = [qb_ref[tile, 2 * hd + m] * kb_ref[0, 2 * hd + m] for m in range(2)]
    fast = jnp.maximum(bsq[0], bsq[1]) <= (SHIFT_LIMIT / BOUND_MARGIN) ** 2
    shift = jnp.sqrt(jnp.concatenate(
        [jnp.full((1, tq), jnp.where(fast, b, 0.0), F32) for b in bsq], axis=1)) * BOUND_MARGIN
    row = lax.broadcasted_iota(jnp.int32, (V_DIM, 2 * tq), 0)
    own_map = (row < HEAD_DIM) == (lax.broadcasted_iota(jnp.int32, (V_DIM, 2 * tq), 1) < tq)
    lq = lq_ref[...]
    lam = (jnp.exp(jnp.sum(lq[0:1] * lq[1:2], axis=-1, keepdims=True))
           - jnp.exp(jnp.sum(lq[2:3] * lq[3:4], axis=-1, keepdims=True)) + lambda_init)

    def load_queries(b):
        qt = qt_ref[:, b * tq:(b + 1) * tq].astype(F32)
        qz_scr[:V_DIM, :] = jnp.where(own_map, jnp.concatenate([qt, qt], axis=1), 0.0).astype(BF16)
        qz_scr[V_DIM:, :] = jnp.where(row == 0, -shift, 0.0).astype(BF16)

    def finish(b):
        acc = acc_scr.at[b]
        on = acc[:V_DIM, :] / jnp.sum(acc[V_DIM:, :], axis=0, keepdims=True)
        ot = on[:, :tq] - lam * on[:, tq:]
        ms = jnp.mean(ot * ot, axis=0, keepdims=True)
        o_ref[:, b * tq:(b + 1) * tq] = (ot * lax.rsqrt(ms + SUBLN_EPS)
                                         * (g_ref[...] * (1.0 - lambda_init))).astype(BF16)

    def bounded_shift_softmax(b):
        def shifted_scores(kb):
            return jnp.dot(k_ref[kb * ATTN_QK:(kb + 1) * ATTN_QK, :], qz_scr[...],
                           preferred_element_type=F32)

        n_blocks = n_keys // ATTN_QK
        pv = [None] * n_col
        psum = [jnp.zeros((SUBLANES, MXU_DIM), F32)] * n_col
        s_next = shifted_scores(0)
        for kb in range(n_blocks):
            s = s_next
            if kb + 1 < n_blocks:
                s_next = shifted_scores(kb + 1)
            for sub in range(ATTN_QK // MXU_DIM):
                blk, off = divmod(kb * ATTN_QK + sub * MXU_DIM, tk)
                rows = slice(sub * MXU_DIM, (sub + 1) * MXU_DIM)
                for nh in range(n_col):
                    p = jnp.exp2(s[rows, nh * MXU_DIM:(nh + 1) * MXU_DIM])
                    psum[nh] = psum[nh] + _sublane_partial_sum(p)
                    d = jnp.dot(vt_ref[blk, :, off:off + MXU_DIM], p.astype(BF16),
                                preferred_element_type=F32)
                    pv[nh] = d if pv[nh] is None else pv[nh] + d
        for nh in range(n_col):
            cols = slice(nh * MXU_DIM, (nh + 1) * MXU_DIM)
            acc_scr[b, :V_DIM, cols] = pv[nh]
            acc_scr[b, V_DIM:, cols] = psum[nh]

    def online_softmax(b):
        n_chunks = n_kv // ATTN_SUB
        ck = ATTN_SUB * tk
        acc = acc_scr.at[b]
        acc[...] = jnp.zeros_like(acc)
        m_scr[...] = jnp.full_like(m_scr, -jnp.inf)

        def scores(j, s_scr, c_scr):
            r = pl.multiple_of(j * ck, ck)
            s = jnp.dot(k_ref[pl.ds(r, ck), :], qz_scr[...],
                        preferred_element_type=F32)
            s_scr[...] = s
            c_scr[...] = jnp.max(s, axis=0, keepdims=True)

        def consume(j, s_scr, c_scr):
            m_prev = m_scr[...]
            m_new = jnp.maximum(m_prev, c_scr[...])
            alpha = jnp.exp2(m_prev - m_new)
            for nh in range(n_col):
                cols = slice(nh * MXU_DIM, (nh + 1) * MXU_DIM)
                pv = None
                psum = jnp.zeros((SUBLANES, MXU_DIM), F32)
                for kb in range(ck // MXU_DIM):
                    p = jnp.exp2(s_scr[kb * MXU_DIM:(kb + 1) * MXU_DIM, cols] - m_new[:, cols])
                    psum = psum + _sublane_partial_sum(p)
                    blk, off = divmod(kb * MXU_DIM, tk)
                    d = jnp.dot(vt_ref[j * ATTN_SUB + blk, :, off:off + MXU_DIM], p.astype(BF16),
                                preferred_element_type=F32)
                    pv = d if pv is None else pv + d
                acc[:V_DIM, cols] = acc[:V_DIM, cols] * alpha[:, cols] + pv
                acc[V_DIM:, cols] = acc[V_DIM:, cols] * alpha[:, cols] + psum
            m_scr[...] = m_new

        scores(0, s0_scr, c0_scr)

        def body(i, carry):
            j = 2 * i
            scores(j + 1, s1_scr, c1_scr)
            consume(j, s0_scr, c0_scr)
            scores(jnp.minimum(j + 2, n_chunks - 1), s0_scr, c0_scr)
            consume(j + 1, s1_scr, c1_scr)
            return carry

        lax.fori_loop(0, n_chunks // 2, body, 0)

    @pl.when(fast)
    def _():
        for b in range(n_sub):
            load_queries(b)
            bounded_shift_softmax(b)
            finish(b)

    @pl.when(jnp.logical_not(fast))
    def _():
        for b in range(n_sub):
            load_queries(b)
            online_softmax(b)
            finish(b)


def _attention(qt, qn, k, kn, vt, lambda_qk, subln_g, lambda_init):
    n_heads, n_tiles, _, _ = qt.shape
    s = k.shape[1]
    n_kv = vt.shape[1]
    kmax = jnp.max(kn, axis=0)
    kb = jnp.stack([kmax[:, 0], kmax[:, HEAD_DIM]], axis=1).reshape(1, 2 * n_heads)
    qb = jnp.stack([qn[:, :, 0], qn[:, :, HEAD_DIM]], axis=2).reshape(n_tiles, 2 * n_heads)
    return pl.pallas_call(
        functools.partial(_attn_kernel, lambda_init=lambda_init),
        grid=(n_heads, n_tiles),
        in_specs=[
            pl.BlockSpec(memory_space=pltpu.SMEM),
            pl.BlockSpec(memory_space=pltpu.SMEM),
            pl.BlockSpec(lambda_qk.shape, lambda h, i: (0, 0)),
            pl.BlockSpec((V_DIM, 1), lambda h, i: (0, 0)),
            pl.BlockSpec((None, None, V_DIM, ROW_TILE), lambda h, i: (h, i, 0, 0)),
            pl.BlockSpec((None, s, K_LANES), lambda h, i: (h, 0, 0)),
            pl.BlockSpec((None, n_kv, V_DIM, ATTN_TK), lambda h, i: (h, 0, 0, 0)),
        ],
        out_specs=pl.BlockSpec((V_DIM, ROW_TILE), lambda h, i: (h, i)),
        out_shape=jax.ShapeDtypeStruct((n_heads * V_DIM, s), BF16),
        scratch_shapes=[pltpu.VMEM((K_LANES, 2 * ATTN_TQ), BF16),
                        pltpu.VMEM((ATTN_SUB * ATTN_TK, 2 * ATTN_TQ), F32),
                        pltpu.VMEM((ATTN_SUB * ATTN_TK, 2 * ATTN_TQ), F32),
                        pltpu.VMEM((1, 2 * ATTN_TQ), F32),
                        pltpu.VMEM((1, 2 * ATTN_TQ), F32),
                        pltpu.VMEM((ROW_TILE // ATTN_TQ, ACC_ROWS, 2 * ATTN_TQ), F32),
                        pltpu.VMEM((1, 2 * ATTN_TQ), F32)],
        compiler_params=_cparams(("parallel", "parallel")),
        name="diff_attn",
    )(qb, kb, lambda_qk, subln_g.reshape(V_DIM, 1), qt, k, vt)


def _dft_tables():
    n = FFT_N
    jk = np.outer(np.arange(n), np.arange(n)) % n
    ang = 2.0 * np.pi * jk / n
    c = np.cos(ang) / math.sqrt(n)
    s = np.sin(ang) / math.sqrt(n)
    stage1 = np.concatenate([c, -s], axis=0)
    stage2 = np.block([[c, s], [-s, c]])
    chan = np.concatenate([c, s], axis=0)
    tw_ang = 2.0 * np.pi * np.outer(np.arange(n), np.arange(n)) / (n * n)
    return (_split(jnp.asarray(stage1, F32)), _split(jnp.asarray(stage2, F32)),
            _split(jnp.asarray(chan, F32)),
            jnp.asarray(np.cos(tw_ang), F32), jnp.asarray(np.sin(tw_ang), F32))


def _split(a):
    hi = a.astype(BF16)
    return hi, (a - hi.astype(F32)).astype(BF16)


def _dot3(a, b):
    (a_hi, a_lo), (b_hi, b_lo) = a, b
    return (jnp.dot(a_hi, b_hi, preferred_element_type=F32)
            + jnp.dot(a_hi, b_lo, preferred_element_type=F32)
            + jnp.dot(a_lo, b_hi, preferred_element_type=F32))


def _fft1_kernel(x_ref, fh_ref, fl_ref, twc_ref, tws_ref, o_ref):
    n = FFT_N
    t = _dot3((fh_ref[...], fl_ref[...]), _split(x_ref[...]))
    width = x_ref.shape[1] // FFT_NB
    for b in range(FFT_NB):
        tr = t[:n, b * width:(b + 1) * width]
        ti = t[n:, b * width:(b + 1) * width]
        c = jnp.concatenate([twc_ref[b]] * (width // LANES), axis=1)
        s = jnp.concatenate([tws_ref[b]] * (width // LANES), axis=1)
        o_ref[0, :, b * width:(b + 1) * width] = tr * c + ti * s
        o_ref[1, :, b * width:(b + 1) * width] = ti * c - tr * s


def _fft2_kernel(t_ref, fh_ref, fl_ref, chh_ref, chl_ref, o_ref):
    n = FFT_N
    width = t_ref.shape[3]
    f = (fh_ref[...], fl_ref[...])
    ch = (chh_ref[...], chl_ref[...])
    for b in range(FFT_KB):
        tt = jnp.concatenate([t_ref[0, b], t_ref[1, b]], axis=0)
        z = _dot3(f, _split(tt))
        for g in range(N_GROUPS):
            zz = jnp.concatenate([z[:n, g * GROUP_DIM:(g + 1) * GROUP_DIM],
                                  z[n:, g * GROUP_DIM:(g + 1) * GROUP_DIM]], axis=1)
            y = _dot3(_split(zz), ch)
            o_ref[:, b * width + g * GROUP_DIM:b * width + (g + 1) * GROUP_DIM] = y.astype(o_ref.dtype)


def _fourier_mix(u, tables):
    s, width = u.shape
    n = FFT_N
    stage1, stage2, chan, twc, tws = tables
    twc_b = jnp.broadcast_to(twc[:, :, None], (n, n, LANES))
    tws_b = jnp.broadcast_to(tws[:, :, None], (n, n, LANES))
    t = pl.pallas_call(
        _fft1_kernel,
        grid=(n // FFT_NB,),
        in_specs=[
            pl.BlockSpec((n, FFT_NB * width), lambda j: (0, j)),
            pl.BlockSpec((2 * n, n), lambda j: (0, 0)),
            pl.BlockSpec((2 * n, n), lambda j: (0, 0)),
            pl.BlockSpec((FFT_NB, n, LANES), lambda j: (j, 0, 0)),
            pl.BlockSpec((FFT_NB, n, LANES), lambda j: (j, 0, 0)),
        ],
        out_specs=pl.BlockSpec((2, n, FFT_NB * width), lambda j: (0, 0, j)),
        out_shape=jax.ShapeDtypeStruct((2, n, n * width), F32),
        compiler_params=_cparams(("parallel",)),
        name="fft_stage1",
    )(u.reshape(n, n * width), *stage1, twc_b, tws_b)
    y = pl.pallas_call(
        _fft2_kernel,
        grid=(n // FFT_KB,),
        in_specs=[
            pl.BlockSpec((2, FFT_KB, n, width), lambda i: (0, i, 0, 0)),
            pl.BlockSpec((2 * n, 2 * n), lambda i: (0, 0)),
            pl.BlockSpec((2 * n, 2 * n), lambda i: (0, 0)),
            pl.BlockSpec((2 * n, n), lambda i: (0, 0)),
            pl.BlockSpec((2 * n, n), lambda i: (0, 0)),
        ],
        out_specs=pl.BlockSpec((n, FFT_KB * width), lambda i: (0, i)),
        out_shape=jax.ShapeDtypeStruct((n, n * width), BF16),
        compiler_params=_cparams(("parallel",)),
        name="fft_stage2",
    )(t.reshape(2, n, n, width), *stage2, *chan)
    return y.reshape(s, width)


def _mixout_kernel(x_ref, pg_ref, sh_ref, sc_ref, gt_ref, qg_ref, ao_ref, fy_ref,
                   gwa_ref, gwf_ref, gba_ref, gbf_ref, ap_ref, fp_ref, wo_ref,
                   o_ref, h_scr):
    j = pl.program_id(1)
    last = pl.num_programs(1) - 1

    def merged_out(rows):
        h = h_scr[rows, :]
        ga = jax.nn.sigmoid(jnp.dot(h, gwa_ref[...], preferred_element_type=F32) + gba_ref[...])
        gf = jax.nn.sigmoid(jnp.dot(h, gwf_ref[...], preferred_element_type=F32) + gbf_ref[...])
        ya = lax.dot_general(ao_ref[:, rows], ap_ref[...], (((0,), (0,)), ((), ())),
                             preferred_element_type=F32)
        yf = jnp.dot(fy_ref[rows, :], fp_ref[...], preferred_element_type=F32)
        y = (ga * ya + gf * yf).astype(BF16)
        return jnp.dot(y, wo_ref[...], preferred_element_type=F32)

    blocks = [slice(r, r + EDGE_ROWS) for r in range(0, x_ref.shape[0], EDGE_ROWS)]

    @pl.when(j == 0)
    def _():
        for rows in blocks:
            _modulated_norm(x_ref.at[rows], h_scr.at[rows], pg_ref, sc_ref, sh_ref)
            o_ref[rows, :] = merged_out(rows)

    @pl.when(jnp.logical_and(j > 0, j < last))
    def _():
        o_ref[...] += merged_out(slice(None))

    @pl.when(j == last)
    def _():
        for rows in blocks:
            o_ref[rows, :] += merged_out(rows)
            _gated_norm_residual(x_ref.at[rows], o_ref.at[rows], o_ref.at[rows], qg_ref, gt_ref, 1.0)


def _mix_out(x, pre_g, shift, scale, gate, post_g, ao, fy, gate_w, gate_b, attn_proj,
             fnet_proj, w_out, layer):
    s, d = x.shape
    nc = d // MIX_TC
    vec = pl.BlockSpec((1, d), lambda i, j: (0, 0))
    return pl.pallas_call(
        _mixout_kernel,
        grid=(s // ROW_TILE, nc),
        in_specs=[
            pl.BlockSpec((ROW_TILE, d), lambda i, j: (i, 0)),
            vec, vec, vec, vec, vec,
            pl.BlockSpec((ATTN_WIDTH, ROW_TILE), lambda i, j: (0, i)),
            pl.BlockSpec((ROW_TILE, FNET_WIDTH), lambda i, j: (i, 0)),
            pl.BlockSpec((None, d, MIX_TC), lambda i, j: (layer, 0, j)),
            pl.BlockSpec((None, d, MIX_TC), lambda i, j: (layer, 0, nc + j)),
            pl.BlockSpec((1, MIX_TC), lambda i, j: (0, j)),
            pl.BlockSpec((1, MIX_TC), lambda i, j: (0, nc + j)),
            pl.BlockSpec((None, ATTN_WIDTH, MIX_TC), lambda i, j: (layer, 0, j)),
            pl.BlockSpec((None, FNET_WIDTH, MIX_TC), lambda i, j: (layer, 0, j)),
            pl.BlockSpec((None, MIX_TC, d), lambda i, j: (layer, j, 0)),
        ],
        out_specs=pl.BlockSpec((ROW_TILE, d), lambda i, j: (i, 0)),
        out_shape=jax.ShapeDtypeStruct((s, d), F32),
        scratch_shapes=[pltpu.VMEM((ROW_TILE, d), BF16)],
        compiler_params=_cparams(("parallel", "arbitrary")),
        name="mix_out",
    )(x, pre_g, shift, scale, gate, post_g, ao, fy, gate_w, gate_w, gate_b, gate_b,
      attn_proj, fnet_proj, w_out)


def _rope_tables(seq):
    half = HEAD_DIM // 2
    pos = jnp.arange(seq, dtype=F32)
    inv_freq = ROPE_THETA ** (-jnp.arange(0, HEAD_DIM, 2, dtype=F32) / HEAD_DIM)
    ang = pos[:, None] * inv_freq[None, :]
    cos, sin = jnp.cos(ang), jnp.sin(ang)
    zero = jnp.zeros_like(sin)
    cos_t = jnp.concatenate([cos] * (LANES // half), axis=1)
    sin_n = jnp.concatenate([-sin, zero] * (LANES // HEAD_DIM), axis=1)
    sin_p = jnp.concatenate([zero, sin] * (LANES // HEAD_DIM), axis=1)
    return cos_t, sin_n, sin_p


def kernel(x, c, ada_w, ada_b, pre_norm_g, post_norm_g, ffn1_w_in, ffn1_w_out, mix_w_in,
           lambda_qk, subln_g, attn_proj, fnet_proj, branch_gate_w, branch_gate_b, mix_w_out,
           ffn2_w_in, ffn2_w_out):
    b, s, d = x.shape
    assert b == 1 and s == FFT_N * FFT_N and s % FFN_ROWS == 0 and s % ATTN_QK == 0
    n_layers = ada_w.shape[0]
    cos_t, sin_n, sin_p = _rope_tables(s)
    dft = _dft_tables()
    mod = _ada_mod(c, ada_w, ada_b).reshape(n_layers, 3, 3, 1, d)
    q_scale = HEAD_DIM ** -0.5 * math.log2(math.e)
    xs = x.reshape(s, d)
    ffn1_in, ffn1_out, ffn2_in, ffn2_out, mix_in_w, gate_w, attn_w, fnet_w, mix_out_w = (
        w.astype(BF16) for w in (ffn1_w_in, ffn1_w_out, ffn2_w_in, ffn2_w_out, mix_w_in,
                                 branch_gate_w, attn_proj, fnet_proj, mix_w_out))
    for l in range(n_layers):
        lambda_init = 0.8 - 0.6 * math.exp(-0.3 * l)
        pre = pre_norm_g[l].reshape(3, 1, d)
        post = post_norm_g[l].reshape(3, 1, d)
        xs = _ffn(xs, pre[0], mod[l, 0, 0], mod[l, 0, 1], mod[l, 0, 2], post[0],
                  ffn1_in, ffn1_out, l)
        qt, qn, k, kn, vt, u = _mix_in(xs, pre[1], mod[l, 1, 0], mod[l, 1, 1], mix_in_w, l,
                              cos_t, sin_n, sin_p, q_scale)
        ao = _attention(qt, qn, k, kn, vt, lambda_qk[l], subln_g[l], lambda_init)
        fy = _fourier_mix(u, dft)
        xs = _mix_out(xs, pre[1], mod[l, 1, 0], mod[l, 1, 1], mod[l, 1, 2], post[1], ao, fy,
                      gate_w, branch_gate_b[l].reshape(1, 2 * d), attn_w, fnet_w, mix_out_w, l)
        xs = _ffn(xs, pre[2], mod[l, 2, 0], mod[l, 2, 1], mod[l, 2, 2], post[2],
                  ffn2_in, ffn2_out, l)
    return xs.reshape(b, s, d)
```

```python
import functools
import math

import numpy as np
import jax
import jax.numpy as jnp
from jax import lax
from jax.experimental import pallas as pl
from jax.experimental.pallas import tpu as pltpu

F32 = jnp.float32
BF16 = jnp.bfloat16

N_HEADS = 8
HEAD_DIM = 64
V_DIM = 2 * HEAD_DIM
ACC_ROWS = V_DIM + 8
K_LANES = 2 * V_DIM
ATTN_WIDTH = N_HEADS * V_DIM
N_GROUPS = 8
GROUP_DIM = 128
FNET_WIDTH = N_GROUPS * GROUP_DIM
ROPE_THETA = 10000.0
NORM_EPS = 1e-6
SUBLN_EPS = 1e-5
MACARON_WEIGHT = 0.5

LANES = 128
SUBLANES = 8
NORM_ROWS = 16
MXU_DIM = 256
VMEM_LIMIT_BYTES = 56 * 1024 * 1024

ADA_TN = 1024
ROW_TILE = 512
FFN_ROWS = 1024
FFN_TF = 512
EDGE_ROWS = 256
MIX_TC = 512
ATTN_TQ = 256
ATTN_TK = ROW_TILE
ATTN_SUB = 2
ATTN_QK = 512
SHIFT_LIMIT = 60.0
BOUND_MARGIN = 1.0 + 2.0 ** -7
FFT_N = 128
FFT_NB = 4
FFT_KB = 4


def _cparams(sem):
    return pltpu.CompilerParams(dimension_semantics=sem, vmem_limit_bytes=VMEM_LIMIT_BYTES)


def _modulated_norm(x_ref, h_ref, g_ref, scale_ref, shift_ref):
    gs = g_ref[...] * (1.0 + scale_ref[...])
    shift = shift_ref[...]
    for r in range(0, x_ref.shape[0], NORM_ROWS):
        x = x_ref[r:r + NORM_ROWS, :]
        ms = jnp.mean(x * x, axis=-1, keepdims=True)
        h_ref[r:r + NORM_ROWS, :] = (x * lax.rsqrt(ms + NORM_EPS) * gs + shift).astype(h_ref.dtype)


def _gated_norm_residual(x_ref, y_ref, o_ref, g_ref, gate_ref, weight):
    coef = g_ref[...] * (weight * gate_ref[...])
    for r in range(0, x_ref.shape[0], NORM_ROWS):
        y = y_ref[r:r + NORM_ROWS, :]
        ms = jnp.mean(y * y, axis=-1, keepdims=True)
        o_ref[r:r + NORM_ROWS, :] = x_ref[r:r + NORM_ROWS, :] + y * lax.rsqrt(ms + NORM_EPS) * coef


def _ada_kernel(c_ref, w_ref, b_ref, o_ref):
    d, tn = w_ref.shape

    def body(k, acc):
        r = pl.multiple_of(k * SUBLANES, SUBLANES)
        cb = c_ref[pl.ds(r, SUBLANES), :]
        cb = cb * jax.nn.sigmoid(cb)
        return acc + w_ref[pl.ds(r, SUBLANES), :] * jnp.concatenate([cb] * (tn // LANES), axis=1)

    acc = lax.fori_loop(0, d // SUBLANES, body, jnp.zeros((SUBLANES, tn), F32), unroll=8)
    o_ref[...] = jnp.sum(acc, axis=0, keepdims=True) + b_ref[...]


def _ada_mod(c, ada_w, ada_b):
    n_layers, d, n = ada_w.shape
    c_b = jnp.broadcast_to(c.reshape(d, 1), (d, LANES))
    out = pl.pallas_call(
        _ada_kernel,
        grid=(n_layers, n // ADA_TN),
        in_specs=[
            pl.BlockSpec((d, LANES), lambda l, j: (0, 0)),
            pl.BlockSpec((None, d, ADA_TN), lambda l, j: (l, 0, j)),
            pl.BlockSpec((None, 1, ADA_TN), lambda l, j: (l, 0, j)),
        ],
        out_specs=pl.BlockSpec((None, 1, ADA_TN), lambda l, j: (l, 0, j)),
        out_shape=jax.ShapeDtypeStruct((n_layers, 1, n), F32),
        compiler_params=_cparams(("parallel", "parallel")),
        name="ada_mod",
    )(c_b, ada_w, ada_b.reshape(n_layers, 1, n))
    return out


def _ffn_kernel(x_ref, pg_ref, sh_ref, sc_ref, gt_ref, qg_ref, wg_ref, wu_ref, wo_ref,
                o_ref, h_scr):
    j = pl.program_id(1)
    last = pl.num_programs(1) - 1

    def up_down(rows):
        h = h_scr[rows, :]
        g = jnp.dot(h, wg_ref[...], preferred_element_type=F32)
        u = jnp.dot(h, wu_ref[...], preferred_element_type=F32)
        a = (g * jax.nn.sigmoid(g) * u).astype(BF16)
        return jnp.dot(a, wo_ref[...], preferred_element_type=F32)

    blocks = [slice(r, r + EDGE_ROWS) for r in range(0, x_ref.shape[0], EDGE_ROWS)]

    @pl.when(j == 0)
    def _():
        for rows in blocks:
            _modulated_norm(x_ref.at[rows], h_scr.at[rows], pg_ref, sc_ref, sh_ref)
            o_ref[rows, :] = up_down(rows)

    @pl.when(jnp.logical_and(j > 0, j < last))
    def _():
        o_ref[...] += up_down(slice(None))

    @pl.when(j == last)
    def _():
        for rows in blocks:
            o_ref[rows, :] += up_down(rows)
            _gated_norm_residual(x_ref.at[rows], o_ref.at[rows], o_ref.at[rows], qg_ref, gt_ref,
                                 MACARON_WEIGHT)


def _ffn(x, pre_g, shift, scale, gate, post_g, w_in, w_out, layer):
    s, d = x.shape
    f = w_out.shape[1]
    nf = f // FFN_TF
    vec = pl.BlockSpec((1, d), lambda i, j: (0, 0))
    return pl.pallas_call(
        _ffn_kernel,
        grid=(s // FFN_ROWS, nf),
        in_specs=[
            pl.BlockSpec((FFN_ROWS, d), lambda i, j: (i, 0)),
            vec, vec, vec, vec, vec,
            pl.BlockSpec((None, d, FFN_TF), lambda i, j: (layer, 0, j)),
            pl.BlockSpec((None, d, FFN_TF), lambda i, j: (layer, 0, nf + j)),
            pl.BlockSpec((None, FFN_TF, d), lambda i, j: (layer, j, 0)),
        ],
        out_specs=pl.BlockSpec((FFN_ROWS, d), lambda i, j: (i, 0)),
        out_shape=jax.ShapeDtypeStruct((s, d), F32),
        scratch_shapes=[pltpu.VMEM((FFN_ROWS, d), BF16)],
        compiler_params=_cparams(("parallel", "arbitrary")),
        name="ffn",
    )(x, pre_g, shift, scale, gate, post_g, w_in, w_in, w_out)


def _mixin_kernel(x_ref, pg_ref, sh_ref, sc_ref, w_ref, freq_ref,
                  qt_ref, qn_ref, k_ref, kn_ref, vt_ref, u_ref, h_scr, *, q_scale):
    _modulated_norm(x_ref, h_scr, pg_ref, sc_ref, sh_ref)
    h = h_scr[...]
    rows = x_ref.shape[0]
    pos = (pl.program_id(0) * rows
           + lax.broadcasted_iota(jnp.int32, (rows, LANES), 0)).astype(F32)
    ang = pos * freq_ref[...]
    first_half = lax.rem(lax.broadcasted_iota(jnp.int32, (rows, LANES), 1), HEAD_DIM) < HEAD_DIM // 2
    cos_t, sin = jnp.cos(ang), jnp.sin(ang)
    sin_n = jnp.where(first_half, -sin, 0.0)
    sin_p = jnp.where(first_half, 0.0, sin)

    def rope(zs):
        return (zs * cos_t + pltpu.roll(zs, LANES - HEAD_DIM // 2, 1) * sin_n
                + pltpu.roll(zs, HEAD_DIM // 2, 1) * sin_p)

    lane = lax.broadcasted_iota(jnp.int32, (x_ref.shape[0], LANES), 1)

    def max_sq_norms(t):
        t2 = t.astype(F32) * t.astype(F32)
        n1 = jnp.max(jnp.sum(jnp.where(lane < HEAD_DIM, t2, 0.0), axis=1, keepdims=True),
                     axis=0, keepdims=True)
        n2 = jnp.max(jnp.sum(jnp.where(lane < HEAD_DIM, 0.0, t2), axis=1, keepdims=True),
                     axis=0, keepdims=True)
        return jnp.where(lane[:1] < HEAD_DIM, n1, n2)

    zq = jnp.dot(h, w_ref[:, 0:ATTN_WIDTH], preferred_element_type=F32)
    for hd in range(N_HEADS):
        qr = (rope(zq[:, hd * V_DIM:(hd + 1) * V_DIM]) * q_scale).astype(BF16)
        qt_ref[hd] = qr.astype(F32).T.astype(BF16)
        qn_ref[hd:hd + 1, :] = max_sq_norms(qr)
    zk = jnp.dot(h, w_ref[:, ATTN_WIDTH:2 * ATTN_WIDTH], preferred_element_type=F32)
    shift_lanes = jnp.where(lane == 0, 1.0, 0.0).astype(BF16)
    for hd in range(N_HEADS):
        kr = rope(zk[:, hd * V_DIM:(hd + 1) * V_DIM]).astype(BF16)
        k_ref[hd, :, :V_DIM] = kr
        k_ref[hd, :, V_DIM:] = shift_lanes
        kn_ref[hd:hd + 1, :] = max_sq_norms(kr)
    zv = jnp.dot(h, w_ref[:, 2 * ATTN_WIDTH:3 * ATTN_WIDTH], preferred_element_type=F32)
    zvt = zv.T
    for hd in range(N_HEADS):
        vt_ref[hd] = zvt[hd * V_DIM:(hd + 1) * V_DIM, :].astype(BF16)
    u_ref[...] = jnp.dot(h, w_ref[:, 3 * ATTN_WIDTH:], preferred_element_type=F32)


def _mix_in(x, pre_g, shift, scale, w, layer, q_scale):
    s, d = x.shape
    n_tiles = s // ROW_TILE
    vec = pl.BlockSpec((1, d), lambda i: (0, 0))
    half = HEAD_DIM // 2
    inv_freq = ROPE_THETA ** (-jnp.arange(0, HEAD_DIM, 2, dtype=F32) / HEAD_DIM)
    freq = jnp.concatenate([inv_freq] * (LANES // half)).reshape(1, LANES)
    return pl.pallas_call(
        functools.partial(_mixin_kernel, q_scale=q_scale),
        grid=(n_tiles,),
        in_specs=[
            pl.BlockSpec((ROW_TILE, d), lambda i: (i, 0)),
            vec, vec, vec,
            pl.BlockSpec((None,) + w.shape[1:], lambda i: (layer, 0, 0), pipeline_mode=pl.Buffered(1)),
            pl.BlockSpec((1, LANES), lambda i: (0, 0)),
        ],
        out_specs=[
            pl.BlockSpec((N_HEADS, None, V_DIM, ROW_TILE), lambda i: (0, i, 0, 0)),
            pl.BlockSpec((None, N_HEADS, LANES), lambda i: (i, 0, 0)),
            pl.BlockSpec((N_HEADS, ROW_TILE, K_LANES), lambda i: (0, i, 0)),
            pl.BlockSpec((None, N_HEADS, LANES), lambda i: (i, 0, 0)),
            pl.BlockSpec((N_HEADS, None, V_DIM, ROW_TILE), lambda i: (0, i, 0, 0)),
            pl.BlockSpec((ROW_TILE, FNET_WIDTH), lambda i: (i, 0)),
        ],
        out_shape=[
            jax.ShapeDtypeStruct((N_HEADS, n_tiles, V_DIM, ROW_TILE), BF16),
            jax.ShapeDtypeStruct((n_tiles, N_HEADS, LANES), F32),
            jax.ShapeDtypeStruct((N_HEADS, s, K_LANES), BF16),
            jax.ShapeDtypeStruct((n_tiles, N_HEADS, LANES), F32),
            jax.ShapeDtypeStruct((N_HEADS, n_tiles, V_DIM, ROW_TILE), BF16),
            jax.ShapeDtypeStruct((s, FNET_WIDTH), F32),
        ],
        scratch_shapes=[pltpu.VMEM((ROW_TILE, d), BF16)],
        compiler_params=_cparams(("parallel",)),
        name="mix_in",
    )(x, pre_g, shift, scale, w, freq)


def _sublane_partial_sum(p):
    return p.reshape(p.shape[0] // SUBLANES, SUBLANES, p.shape[1]).sum(axis=0)


def _attn_kernel(qb_ref, kb_ref, lq_ref, g_ref, qt_ref, k_ref, vt_ref, o_ref,
                 qz_scr, s0_scr, s1_scr, c0_scr, c1_scr, acc_scr, m_scr, *, lambda_init):
    tq = ATTN_TQ
    n_sub = qt_ref.shape[1] // tq
    n_kv, _, tk = vt_ref.shape
    n_keys = n_kv * tk
    n_col = 2 * tq // MXU_DIM
    hd = pl.program_id(0)
    tile = pl.program_id(1)
    bsq = [qb_ref[tile, 2 * hd + m] * kb_ref[0, 2 * hd + m] for m in range(2)]
    fast = jnp.maximum(bsq[0], bsq[1]) <= (SHIFT_LIMIT / BOUND_MARGIN) ** 2
    shift = jnp.sqrt(jnp.concatenate(
        [jnp.full((1, tq), jnp.where(fast, b, 0.0), F32) for b in bsq], axis=1)) * BOUND_MARGIN
    row = lax.broadcasted_iota(jnp.int32, (V_DIM, 2 * tq), 0)
    own_map = (row < HEAD_DIM) == (lax.broadcasted_iota(jnp.int32, (V_DIM, 2 * tq), 1) < tq)
    lq = lq_ref[...]
    lam = (jnp.exp(jnp.sum(lq[0:1] * lq[1:2], axis=-1, keepdims=True))
           - jnp.exp(jnp.sum(lq[2:3] * lq[3:4], axis=-1, keepdims=True)) + lambda_init)

    def load_queries(b):
        qt = qt_ref[:, b * tq:(b + 1) * tq].astype(F32)
        qz_scr[:V_DIM, :] = jnp.where(own_map, jnp.concatenate([qt, qt], axis=1), 0.0).astype(BF16)
        qz_scr[V_DIM:, :] = jnp.where(row == 0, -shift, 0.0).astype(BF16)

    def finish(b):
        acc = acc_scr.at[b]
        on = acc[:V_DIM, :] / jnp.sum(acc[V_DIM:, :], axis=0, keepdims=True)
        ot = on[:, :tq] - lam * on[:, tq:]
        ms = jnp.mean(ot * ot, axis=0, keepdims=True)
        o_ref[:, b * tq:(b + 1) * tq] = (ot * lax.rsqrt(ms + SUBLN_EPS)
                                         * (g_ref[...] * (1.0 - lambda_init))).astype(BF16)

    def bounded_shift_softmax(b):
        def shifted_scores(kb):
            return jnp.dot(k_ref[kb * ATTN_QK:(kb + 1) * ATTN_QK, :], qz_scr[...],
                           preferred_element_type=F32)

        n_blocks = n_keys // ATTN_QK
        pv = [None] * n_col
        psum = [jnp.zeros((SUBLANES, MXU_DIM), F32)] * n_col
        s_next = shifted_scores(0)
        for kb in range(n_blocks):
            s = s_next
            if kb + 1 < n_blocks:
                s_next = shifted_scores(kb + 1)
            for sub in range(ATTN_QK // MXU_DIM):
                blk, off = divmod(kb * ATTN_QK + sub * MXU_DIM, tk)
                rows = slice(sub * MXU_DIM, (sub + 1) * MXU_DIM)
                for nh in range(n_col):
                    p = jnp.exp2(s[rows, nh * MXU_DIM:(nh + 1) * MXU_DIM])
                    psum[nh] = psum[nh] + _sublane_partial_sum(p)
                    d = jnp.dot(vt_ref[blk, :, off:off + MXU_DIM], p.astype(BF16),
                                preferred_element_type=F32)
                    pv[nh] = d if pv[nh] is None else pv[nh] + d
        for nh in range(n_col):
            cols = slice(nh * MXU_DIM, (nh + 1) * MXU_DIM)
            acc_scr[b, :V_DIM, cols] = pv[nh]
            acc_scr[b, V_DIM:, cols] = psum[nh]

    def online_softmax(b):
        n_chunks = n_kv // ATTN_SUB
        ck = ATTN_SUB * tk
        acc = acc_scr.at[b]
        acc[...] = jnp.zeros_like(acc)
        m_scr[...] = jnp.full_like(m_scr, -jnp.inf)

        def scores(j, s_scr, c_scr):
            r = pl.multiple_of(j * ck, ck)
            s = jnp.dot(k_ref[pl.ds(r, ck), :], qz_scr[...],
                        preferred_element_type=F32)
            s_scr[...] = s
            c_scr[...] = jnp.max(s, axis=0, keepdims=True)

        def consume(j, s_scr, c_scr):
            m_prev = m_scr[...]
            m_new = jnp.maximum(m_prev, c_scr[...])
            alpha = jnp.exp2(m_prev - m_new)
            for nh in range(n_col):
                cols = slice(nh * MXU_DIM, (nh + 1) * MXU_DIM)
                pv = None
                psum = jnp.zeros((SUBLANES, MXU_DIM), F32)
                for kb in range(ck // MXU_DIM):
                    p = jnp.exp2(s_scr[kb * MXU_DIM:(kb + 1) * MXU_DIM, cols] - m_new[:, cols])
                    psum = psum + _sublane_partial_sum(p)
                    blk, off = divmod(kb * MXU_DIM, tk)
                    d = jnp.dot(vt_ref[j * ATTN_SUB + blk, :, off:off + MXU_DIM], p.astype(BF16),
                                preferred_element_type=F32)
                    pv = d if pv is None else pv + d
                acc[:V_DIM, cols] = acc[:V_DIM, cols] * alpha[:, cols] + pv
                acc[V_DIM:, cols] = acc[V_DIM:, cols] * alpha[:, cols] + psum
            m_scr[...] = m_new

        scores(0, s0_scr, c0_scr)

        def body(i, carry):
            j = 2 * i
            scores(j + 1, s1_scr, c1_scr)
            consume(j, s0_scr, c0_scr)
            scores(jnp.minimum(j + 2, n_chunks - 1), s0_scr, c0_scr)
            consume(j + 1, s1_scr, c1_scr)
            return carry

        lax.fori_loop(0, n_chunks // 2, body, 0)

    @pl.when(fast)
    def _():
        for b in range(n_sub):
            load_queries(b)
            bounded_shift_softmax(b)
            finish(b)

    @pl.when(jnp.logical_not(fast))
    def _():
        for b in range(n_sub):
            load_queries(b)
            online_softmax(b)
            finish(b)


def _attention(qt, qn, k, kn, vt, lambda_qk, subln_g, lambda_init):
    n_heads, n_tiles, _, _ = qt.shape
    s = k.shape[1]
    n_kv = vt.shape[1]
    kmax = jnp.max(kn, axis=0)
    kb = jnp.stack([kmax[:, 0], kmax[:, HEAD_DIM]], axis=1).reshape(1, 2 * n_heads)
    qb = jnp.stack([qn[:, :, 0], qn[:, :, HEAD_DIM]], axis=2).reshape(n_tiles, 2 * n_heads)
    return pl.pallas_call(
        functools.partial(_attn_kernel, lambda_init=lambda_init),
        grid=(n_heads, n_tiles),
        in_specs=[
            pl.BlockSpec(memory_space=pltpu.SMEM),
            pl.BlockSpec(memory_space=pltpu.SMEM),
            pl.BlockSpec(lambda_qk.shape, lambda h, i: (0, 0)),
            pl.BlockSpec((V_DIM, 1), lambda h, i: (0, 0)),
            pl.BlockSpec((None, None, V_DIM, ROW_TILE), lambda h, i: (h, i, 0, 0)),
            pl.BlockSpec((None, s, K_LANES), lambda h, i: (h, 0, 0)),
            pl.BlockSpec((None, n_kv, V_DIM, ATTN_TK), lambda h, i: (h, 0, 0, 0)),
        ],
        out_specs=pl.BlockSpec((V_DIM, ROW_TILE), lambda h, i: (h, i)),
        out_shape=jax.ShapeDtypeStruct((n_heads * V_DIM, s), BF16),
        scratch_shapes=[pltpu.VMEM((K_LANES, 2 * ATTN_TQ), BF16),
                        pltpu.VMEM((ATTN_SUB * ATTN_TK, 2 * ATTN_TQ), F32),
                        pltpu.VMEM((ATTN_SUB * ATTN_TK, 2 * ATTN_TQ), F32),
                        pltpu.VMEM((1, 2 * ATTN_TQ), F32),
                        pltpu.VMEM((1, 2 * ATTN_TQ), F32),
                        pltpu.VMEM((ROW_TILE // ATTN_TQ, ACC_ROWS, 2 * ATTN_TQ), F32),
                        pltpu.VMEM((1, 2 * ATTN_TQ), F32)],
        compiler_params=_cparams(("parallel", "parallel")),
        name="diff_attn",
    )(qb, kb, lambda_qk, subln_g.reshape(V_DIM, 1), qt, k, vt)


def _dft_tables():
    n = FFT_N
    jk = np.outer(np.arange(n), np.arange(n)) % n
    ang = 2.0 * np.pi * jk / n
    c = np.cos(ang) / math.sqrt(n)
    s = np.sin(ang) / math.sqrt(n)
    stage1 = np.concatenate([c, -s], axis=0)
    stage2 = np.block([[c, s], [-s, c]])
    chan = np.concatenate([c, s], axis=0)
    tw_ang = 2.0 * np.pi * np.outer(np.arange(n), np.arange(n)) / (n * n)
    return (_split(jnp.asarray(stage1, F32)), _split(jnp.asarray(stage2, F32)),
            _split(jnp.asarray(chan, F32)),
            jnp.asarray(np.cos(tw_ang), F32), jnp.asarray(np.sin(tw_ang), F32))


def _split(a):
    hi = a.astype(BF16)
    return hi, (a - hi.astype(F32)).astype(BF16)


def _dot3(a, b):
    (a_hi, a_lo), (b_hi, b_lo) = a, b
    return (jnp.dot(a_hi, b_hi, preferred_element_type=F32)
            + jnp.dot(a_hi, b_lo, preferred_element_type=F32)
            + jnp.dot(a_lo, b_hi, preferred_element_type=F32))


def _fft1_kernel(x_ref, fh_ref, fl_ref, twc_ref, tws_ref, o_ref):
    n = FFT_N
    t = _dot3((fh_ref[...], fl_ref[...]), _split(x_ref[...]))
    width = x_ref.shape[1] // FFT_NB
    for b in range(FFT_NB):
        tr = t[:n, b * width:(b + 1) * width]
        ti = t[n:, b * width:(b + 1) * width]
        c = jnp.concatenate([twc_ref[b]] * (width // LANES), axis=1)
        s = jnp.concatenate([tws_ref[b]] * (width // LANES), axis=1)
        o_ref[0, :, b * width:(b + 1) * width] = tr * c + ti * s
        o_ref[1, :, b * width:(b + 1) * width] = ti * c - tr * s


def _fft2_kernel(t_ref, fh_ref, fl_ref, chh_ref, chl_ref, o_ref):
    n = FFT_N
    width = t_ref.shape[3]
    f = (fh_ref[...], fl_ref[...])
    ch = (chh_ref[...], chl_ref[...])
    for b in range(FFT_KB):
        tt = jnp.concatenate([t_ref[0, b], t_ref[1, b]], axis=0)
        z = _dot3(f, _split(tt))
        for g in range(N_GROUPS):
            zz = jnp.concatenate([z[:n, g * GROUP_DIM:(g + 1) * GROUP_DIM],
                                  z[n:, g * GROUP_DIM:(g + 1) * GROUP_DIM]], axis=1)
            y = _dot3(_split(zz), ch)
            o_ref[:, b * width + g * GROUP_DIM:b * width + (g + 1) * GROUP_DIM] = y.astype(o_ref.dtype)


def _fourier_mix(u, tables):
    s, width = u.shape
    n = FFT_N
    stage1, stage2, chan, twc, tws = tables
    twc_b = jnp.broadcast_to(twc[:, :, None], (n, n, LANES))
    tws_b = jnp.broadcast_to(tws[:, :, None], (n, n, LANES))
    t = pl.pallas_call(
        _fft1_kernel,
        grid=(n // FFT_NB,),
        in_specs=[
            pl.BlockSpec((n, FFT_NB * width), lambda j: (0, j)),
            pl.BlockSpec((2 * n, n), lambda j: (0, 0)),
            pl.BlockSpec((2 * n, n), lambda j: (0, 0)),
            pl.BlockSpec((FFT_NB, n, LANES), lambda j: (j, 0, 0)),
            pl.BlockSpec((FFT_NB, n, LANES), lambda j: (j, 0, 0)),
        ],
        out_specs=pl.BlockSpec((2, n, FFT_NB * width), lambda j: (0, 0, j)),
        out_shape=jax.ShapeDtypeStruct((2, n, n * width), F32),
        compiler_params=_cparams(("parallel",)),
        name="fft_stage1",
    )(u.reshape(n, n * width), *stage1, twc_b, tws_b)
    y = pl.pallas_call(
        _fft2_kernel,
        grid=(n // FFT_KB,),
        in_specs=[
            pl.BlockSpec((2, FFT_KB, n, width), lambda i: (0, i, 0, 0)),
            pl.BlockSpec((2 * n, 2 * n), lambda i: (0, 0)),
            pl.BlockSpec((2 * n, 2 * n), lambda i: (0, 0)),
            pl.BlockSpec((2 * n, n), lambda i: (0, 0)),
            pl.BlockSpec((2 * n, n), lambda i: (0, 0)),
        ],
        out_specs=pl.BlockSpec((n, FFT_KB * width), lambda i: (0, i)),
        out_shape=jax.ShapeDtypeStruct((n, n * width), BF16),
        compiler_params=_cparams(("parallel",)),
        name="fft_stage2",
    )(t.reshape(2, n, n, width), *stage2, *chan)
    return y.reshape(s, width)


def _mixout_kernel(x_ref, pg_ref, sh_ref, sc_ref, gt_ref, qg_ref, ao_ref, fy_ref,
                   gwa_ref, gwf_ref, gba_ref, gbf_ref, ap_ref, fp_ref, wo_ref,
                   o_ref, h_scr):
    j = pl.program_id(1)
    last = pl.num_programs(1) - 1

    def merged_out(rows):
        h = h_scr[rows, :]
        ga = jax.nn.sigmoid(jnp.dot(h, gwa_ref[...], preferred_element_type=F32) + gba_ref[...])
        gf = jax.nn.sigmoid(jnp.dot(h, gwf_ref[...], preferred_element_type=F32) + gbf_ref[...])
        ya = lax.dot_general(ao_ref[:, rows], ap_ref[...], (((0,), (0,)), ((), ())),
                             preferred_element_type=F32)
        yf = jnp.dot(fy_ref[rows, :], fp_ref[...], preferred_element_type=F32)
        y = (ga * ya + gf * yf).astype(BF16)
        return jnp.dot(y, wo_ref[...], preferred_element_type=F32)

    blocks = [slice(r, r + EDGE_ROWS) for r in range(0, x_ref.shape[0], EDGE_ROWS)]

    @pl.when(j == 0)
    def _():
        for rows in blocks:
            _modulated_norm(x_ref.at[rows], h_scr.at[rows], pg_ref, sc_ref, sh_ref)
            o_ref[rows, :] = merged_out(rows)

    @pl.when(jnp.logical_and(j > 0, j < last))
    def _():
        o_ref[...] += merged_out(slice(None))

    @pl.when(j == last)
    def _():
        for rows in blocks:
            o_ref[rows, :] += merged_out(rows)
            _gated_norm_residual(x_ref.at[rows], o_ref.at[rows], o_ref.at[rows], qg_ref, gt_ref, 1.0)


def _mix_out(x, pre_g, shift, scale, gate, post_g, ao, fy, gate_w, gate_b, attn_proj,
             fnet_proj, w_out, layer):
    s, d = x.shape
    nc = d // MIX_TC
    vec = pl.BlockSpec((1, d), lambda i, j: (0, 0))
    return pl.pallas_call(
        _mixout_kernel,
        grid=(s // ROW_TILE, nc),
        in_specs=[
            pl.BlockSpec((ROW_TILE, d), lambda i, j: (i, 0)),
            vec, vec, vec, vec, vec,
            pl.BlockSpec((ATTN_WIDTH, ROW_TILE), lambda i, j: (0, i)),
            pl.BlockSpec((ROW_TILE, FNET_WIDTH), lambda i, j: (i, 0)),
            pl.BlockSpec((None, d, MIX_TC), lambda i, j: (layer, 0, j)),
            pl.BlockSpec((None, d, MIX_TC), lambda i, j: (layer, 0, nc + j)),
            pl.BlockSpec((1, MIX_TC), lambda i, j: (0, j)),
            pl.BlockSpec((1, MIX_TC), lambda i, j: (0, nc + j)),
            pl.BlockSpec((None, ATTN_WIDTH, MIX_TC), lambda i, j: (layer, 0, j)),
            pl.BlockSpec((None, FNET_WIDTH, MIX_TC), lambda i, j: (layer, 0, j)),
            pl.BlockSpec((None, MIX_TC, d), lambda i, j: (layer, j, 0)),
        ],
        out_specs=pl.BlockSpec((ROW_TILE, d), lambda i, j: (i, 0)),
        out_shape=jax.ShapeDtypeStruct((s, d), F32),
        scratch_shapes=[pltpu.VMEM((ROW_TILE, d), BF16)],
        compiler_params=_cparams(("parallel", "arbitrary")),
        name="mix_out",
    )(x, pre_g, shift, scale, gate, post_g, ao, fy, gate_w, gate_w, gate_b, gate_b,
      attn_proj, fnet_proj, w_out)


def kernel(x, c, ada_w, ada_b, pre_norm_g, post_norm_g, ffn1_w_in, ffn1_w_out, mix_w_in,
           lambda_qk, subln_g, attn_proj, fnet_proj, branch_gate_w, branch_gate_b, mix_w_out,
           ffn2_w_in, ffn2_w_out):
    b, s, d = x.shape
    assert b == 1 and s == FFT_N * FFT_N and s % FFN_ROWS == 0 and s % ATTN_QK == 0
    n_layers = ada_w.shape[0]
    dft = _dft_tables()
    mod = _ada_mod(c, ada_w, ada_b).reshape(n_layers, 3, 3, 1, d)
    q_scale = HEAD_DIM ** -0.5 * math.log2(math.e)
    xs = x.reshape(s, d)
    ffn1_in, ffn1_out, ffn2_in, ffn2_out, mix_in_w, gate_w, attn_w, fnet_w, mix_out_w = (
        w.astype(BF16) for w in (ffn1_w_in, ffn1_w_out, ffn2_w_in, ffn2_w_out, mix_w_in,
                                 branch_gate_w, attn_proj, fnet_proj, mix_w_out))
    for l in range(n_layers):
        lambda_init = 0.8 - 0.6 * math.exp(-0.3 * l)
        pre = pre_norm_g[l].reshape(3, 1, d)
        post = post_norm_g[l].reshape(3, 1, d)
        xs = _ffn(xs, pre[0], mod[l, 0, 0], mod[l, 0, 1], mod[l, 0, 2], post[0],
                  ffn1_in, ffn1_out, l)
        qt, qn, k, kn, vt, u = _mix_in(xs, pre[1], mod[l, 1, 0], mod[l, 1, 1], mix_in_w, l, q_scale)
        ao = _attention(qt, qn, k, kn, vt, lambda_qk[l], subln_g[l], lambda_init)
        fy = _fourier_mix(u, dft)
        xs = _mix_out(xs, pre[1], mod[l, 1, 0], mod[l, 1, 1], mod[l, 1, 2], post[1], ao, fy,
                      gate_w, branch_gate_b[l].reshape(1, 2 * d), attn_w, fnet_w, mix_out_w, l)
        xs = _ffn(xs, pre[2], mod[l, 2, 0], mod[l, 2, 1], mod[l, 2, 2], post[2],
                  ffn2_in, ffn2_out, l)
    return xs.reshape(b, s, d)
```

```python
import functools
import math

import numpy as np
import jax
import jax.numpy as jnp
from jax import lax
from jax.experimental import pallas as pl
from jax.experimental.pallas import tpu as pltpu

F32 = jnp.float32
BF16 = jnp.bfloat16

N_HEADS = 8
HEAD_DIM = 64
V_DIM = 2 * HEAD_DIM
ACC_ROWS = V_DIM + 8
K_LANES = 2 * V_DIM
ATTN_WIDTH = N_HEADS * V_DIM
N_GROUPS = 8
GROUP_DIM = 128
FNET_WIDTH = N_GROUPS * GROUP_DIM
ROPE_THETA = 10000.0
NORM_EPS = 1e-6
SUBLN_EPS = 1e-5
MACARON_WEIGHT = 0.5

LANES = 128
SUBLANES = 8
NORM_ROWS = 16
MXU_DIM = 256
VMEM_LIMIT_BYTES = 56 * 1024 * 1024

ADA_TN = 2048
ROW_TILE = 512
FFN_ROWS = 1024
FFN_TF = 512
EDGE_ROWS = 256
MIX_TC = 512
ATTN_TQ = 256
ATTN_TK = ROW_TILE
ATTN_SUB = 2
ATTN_QK = 1024
SHIFT_LIMIT = 60.0
BOUND_MARGIN = 1.0 + 2.0 ** -7
FFT_N = 128
FFT_NB = 8
FFT_KB = 8


def _cparams(sem):
    return pltpu.CompilerParams(dimension_semantics=sem, vmem_limit_bytes=VMEM_LIMIT_BYTES)


def _modulated_norm(x_ref, h_ref, g_ref, scale_ref, shift_ref):
    gs = g_ref[...] * (1.0 + scale_ref[...])
    shift = shift_ref[...]
    for r in range(0, x_ref.shape[0], NORM_ROWS):
        x = x_ref[r:r + NORM_ROWS, :]
        ms = jnp.mean(x * x, axis=-1, keepdims=True)
        h_ref[r:r + NORM_ROWS, :] = (x * lax.rsqrt(ms + NORM_EPS) * gs + shift).astype(h_ref.dtype)


def _gated_norm_residual(x_ref, y_ref, o_ref, g_ref, gate_ref, weight):
    coef = g_ref[...] * (weight * gate_ref[...])
    for r in range(0, x_ref.shape[0], NORM_ROWS):
        y = y_ref[r:r + NORM_ROWS, :]
        ms = jnp.mean(y * y, axis=-1, keepdims=True)
        o_ref[r:r + NORM_ROWS, :] = x_ref[r:r + NORM_ROWS, :] + y * lax.rsqrt(ms + NORM_EPS) * coef


def _ada_kernel(c_ref, w_ref, b_ref, o_ref):
    d, tn = w_ref.shape

    def body(k, acc):
        r = pl.multiple_of(k * SUBLANES, SUBLANES)
        cb = c_ref[pl.ds(r, SUBLANES), :]
        cb = cb * jax.nn.sigmoid(cb)
        return acc + w_ref[pl.ds(r, SUBLANES), :] * jnp.concatenate([cb] * (tn // LANES), axis=1)

    acc = lax.fori_loop(0, d // SUBLANES, body, jnp.zeros((SUBLANES, tn), F32), unroll=8)
    o_ref[...] = jnp.sum(acc, axis=0, keepdims=True) + b_ref[...]


def _ada_mod(c, ada_w, ada_b):
    n_layers, d, n = ada_w.shape
    c_b = jnp.broadcast_to(c.reshape(d, 1), (d, LANES))
    out = pl.pallas_call(
        _ada_kernel,
        grid=(n_layers, n // ADA_TN),
        in_specs=[
            pl.BlockSpec((d, LANES), lambda l, j: (0, 0)),
            pl.BlockSpec((None, d, ADA_TN), lambda l, j: (l, 0, j)),
            pl.BlockSpec((None, 1, ADA_TN), lambda l, j: (l, 0, j)),
        ],
        out_specs=pl.BlockSpec((None, 1, ADA_TN), lambda l, j: (l, 0, j)),
        out_shape=jax.ShapeDtypeStruct((n_layers, 1, n), F32),
        compiler_params=_cparams(("parallel", "parallel")),
        name="ada_mod",
    )(c_b, ada_w, ada_b.reshape(n_layers, 1, n))
    return out


def _ffn_kernel(x_ref, pg_ref, sh_ref, sc_ref, gt_ref, qg_ref, wg_ref, wu_ref, wo_ref,
                o_ref, h_scr):
    j = pl.program_id(1)
    last = pl.num_programs(1) - 1

    def up_down(rows):
        h = h_scr[rows, :]
        g = jnp.dot(h, wg_ref[...], preferred_element_type=F32)
        u = jnp.dot(h, wu_ref[...], preferred_element_type=F32)
        a = (g * jax.nn.sigmoid(g) * u).astype(BF16)
        return jnp.dot(a, wo_ref[...], preferred_element_type=F32)

    blocks = [slice(r, r + EDGE_ROWS) for r in range(0, x_ref.shape[0], EDGE_ROWS)]

    @pl.when(j == 0)
    def _():
        for rows in blocks:
            _modulated_norm(x_ref.at[rows], h_scr.at[rows], pg_ref, sc_ref, sh_ref)
            o_ref[rows, :] = up_down(rows)

    @pl.when(jnp.logical_and(j > 0, j < last))
    def _():
        o_ref[...] += up_down(slice(None))

    @pl.when(j == last)
    def _():
        for rows in blocks:
            o_ref[rows, :] += up_down(rows)
            _gated_norm_residual(x_ref.at[rows], o_ref.at[rows], o_ref.at[rows], qg_ref, gt_ref,
                                 MACARON_WEIGHT)


def _ffn(x, pre_g, shift, scale, gate, post_g, w_in, w_out, layer):
    s, d = x.shape
    f = w_out.shape[1]
    nf = f // FFN_TF
    vec = pl.BlockSpec((1, d), lambda i, j: (0, 0))
    return pl.pallas_call(
        _ffn_kernel,
        grid=(s // FFN_ROWS, nf),
        in_specs=[
            pl.BlockSpec((FFN_ROWS, d), lambda i, j: (i, 0)),
            vec, vec, vec, vec, vec,
            pl.BlockSpec((None, d, FFN_TF), lambda i, j: (layer, 0, j)),
            pl.BlockSpec((None, d, FFN_TF), lambda i, j: (layer, 0, nf + j)),
            pl.BlockSpec((None, FFN_TF, d), lambda i, j: (layer, j, 0)),
        ],
        out_specs=pl.BlockSpec((FFN_ROWS, d), lambda i, j: (i, 0)),
        out_shape=jax.ShapeDtypeStruct((s, d), F32),
        scratch_shapes=[pltpu.VMEM((FFN_ROWS, d), BF16)],
        compiler_params=_cparams(("parallel", "arbitrary")),
        name="ffn",
    )(x, pre_g, shift, scale, gate, post_g, w_in, w_in, w_out)


def _mixin_kernel(x_ref, pg_ref, sh_ref, sc_ref, w_ref, freq_ref,
                  qt_ref, qn_ref, k_ref, kn_ref, vt_ref, u_ref, h_scr, *, q_scale):
    n_rows = EDGE_ROWS
    lane = lax.broadcasted_iota(jnp.int32, (n_rows, LANES), 1)
    first_half = lax.rem(lane, HEAD_DIM) < HEAD_DIM // 2
    shift_lanes = jnp.where(lane == 0, 1.0, 0.0).astype(BF16)

    def max_sq_norms(t):
        t2 = t.astype(F32) * t.astype(F32)
        n1 = jnp.max(jnp.sum(jnp.where(lane < HEAD_DIM, t2, 0.0), axis=1, keepdims=True),
                     axis=0, keepdims=True)
        n2 = jnp.max(jnp.sum(jnp.where(lane < HEAD_DIM, 0.0, t2), axis=1, keepdims=True),
                     axis=0, keepdims=True)
        return jnp.where(lane[:1] < HEAD_DIM, n1, n2)

    qn = [None] * N_HEADS
    kn = [None] * N_HEADS
    for r0 in range(0, x_ref.shape[0], n_rows):
        rows = slice(r0, r0 + n_rows)
        _modulated_norm(x_ref.at[rows], h_scr.at[rows], pg_ref, sc_ref, sh_ref)
        h = h_scr[rows, :]
        pos = (pl.program_id(0) * x_ref.shape[0] + r0
               + lax.broadcasted_iota(jnp.int32, (n_rows, LANES), 0)).astype(F32)
        ang = pos * freq_ref[...]
        cos_t, sin = jnp.cos(ang), jnp.sin(ang)
        sin_n = jnp.where(first_half, -sin, 0.0)
        sin_p = jnp.where(first_half, 0.0, sin)

        def rope(zs):
            return (zs * cos_t + pltpu.roll(zs, LANES - HEAD_DIM // 2, 1) * sin_n
                    + pltpu.roll(zs, HEAD_DIM // 2, 1) * sin_p)

        zq = jnp.dot(h, w_ref[:, 0:ATTN_WIDTH], preferred_element_type=F32)
        for hd in range(N_HEADS):
            qr = (rope(zq[:, hd * V_DIM:(hd + 1) * V_DIM]) * q_scale).astype(BF16)
            qt_ref[hd, :, rows] = qr.astype(F32).T.astype(BF16)
            n = max_sq_norms(qr)
            qn[hd] = n if qn[hd] is None else jnp.maximum(qn[hd], n)
        zk = jnp.dot(h, w_ref[:, ATTN_WIDTH:2 * ATTN_WIDTH], preferred_element_type=F32)
        for hd in range(N_HEADS):
            kr = rope(zk[:, hd * V_DIM:(hd + 1) * V_DIM]).astype(BF16)
            k_ref[hd, rows, :V_DIM] = kr
            k_ref[hd, rows, V_DIM:] = shift_lanes
            n = max_sq_norms(kr)
            kn[hd] = n if kn[hd] is None else jnp.maximum(kn[hd], n)
        zvt = jnp.dot(h, w_ref[:, 2 * ATTN_WIDTH:3 * ATTN_WIDTH], preferred_element_type=F32).T
        for hd in range(N_HEADS):
            vt_ref[hd, :, rows] = zvt[hd * V_DIM:(hd + 1) * V_DIM, :].astype(BF16)
        u_ref[rows, :] = jnp.dot(h, w_ref[:, 3 * ATTN_WIDTH:], preferred_element_type=F32)
    for hd in range(N_HEADS):
        qn_ref[hd:hd + 1, :] = qn[hd]
        kn_ref[hd:hd + 1, :] = kn[hd]


def _mix_in(x, pre_g, shift, scale, w, layer, q_scale):
    s, d = x.shape
    n_tiles = s // ROW_TILE
    vec = pl.BlockSpec((1, d), lambda i: (0, 0))
    half = HEAD_DIM // 2
    inv_freq = ROPE_THETA ** (-jnp.arange(0, HEAD_DIM, 2, dtype=F32) / HEAD_DIM)
    freq = jnp.concatenate([inv_freq] * (LANES // half)).reshape(1, LANES)
    return pl.pallas_call(
        functools.partial(_mixin_kernel, q_scale=q_scale),
        grid=(n_tiles,),
        in_specs=[
            pl.BlockSpec((ROW_TILE, d), lambda i: (i, 0)),
            vec, vec, vec,
            pl.BlockSpec((None,) + w.shape[1:], lambda i: (layer, 0, 0), pipeline_mode=pl.Buffered(1)),
            pl.BlockSpec((1, LANES), lambda i: (0, 0)),
        ],
        out_specs=[
            pl.BlockSpec((N_HEADS, None, V_DIM, ROW_TILE), lambda i: (0, i, 0, 0)),
            pl.BlockSpec((None, N_HEADS, LANES), lambda i: (i, 0, 0)),
            pl.BlockSpec((N_HEADS, ROW_TILE, K_LANES), lambda i: (0, i, 0)),
            pl.BlockSpec((None, N_HEADS, LANES), lambda i: (i, 0, 0)),
            pl.BlockSpec((N_HEADS, None, V_DIM, ROW_TILE), lambda i: (0, i, 0, 0)),
            pl.BlockSpec((ROW_TILE, FNET_WIDTH), lambda i: (i, 0)),
        ],
        out_shape=[
            jax.ShapeDtypeStruct((N_HEADS, n_tiles, V_DIM, ROW_TILE), BF16),
            jax.ShapeDtypeStruct((n_tiles, N_HEADS, LANES), F32),
            jax.ShapeDtypeStruct((N_HEADS, s, K_LANES), BF16),
            jax.ShapeDtypeStruct((n_tiles, N_HEADS, LANES), F32),
            jax.ShapeDtypeStruct((N_HEADS, n_tiles, V_DIM, ROW_TILE), BF16),
            jax.ShapeDtypeStruct((s, FNET_WIDTH), F32),
        ],
        scratch_shapes=[pltpu.VMEM((ROW_TILE, d), BF16)],
        compiler_params=_cparams(("parallel",)),
        name="mix_in",
    )(x, pre_g, shift, scale, w, freq)


def _sublane_partial_sum(p):
    return p.reshape(p.shape[0] // SUBLANES, SUBLANES, p.shape[1]).sum(axis=0)


def _attn_kernel(qb_ref, kb_ref, lq_ref, g_ref, qt_ref, k_ref, vt_ref, o_ref,
                 qz_scr, s0_scr, s1_scr, c0_scr, c1_scr, acc_scr, m_scr, *, lambda_init):
    tq = ATTN_TQ
    n_sub = qt_ref.shape[1] // tq
    n_kv, _, tk = vt_ref.shape
    n_keys = n_kv * tk
    n_col = 2 * tq // MXU_DIM
    hd = pl.program_id(0)
    tile = pl.program_id(1)
    bsq = [qb_ref[tile, 2 * hd + m] * kb_ref[0, 2 * hd + m] for m in range(2)]
    fast = jnp.maximum(bsq[0], bsq[1]) <= (SHIFT_LIMIT / BOUND_MARGIN) ** 2
    shift = jnp.sqrt(jnp.concatenate(
        [jnp.full((1, tq), jnp.where(fast, b, 0.0), F32) for b in bsq], axis=1)) * BOUND_MARGIN
    row = lax.broadcasted_iota(jnp.int32, (V_DIM, 2 * tq), 0)
    own_map = (row < HEAD_DIM) == (lax.broadcasted_iota(jnp.int32, (V_DIM, 2 * tq), 1) < tq)
    lq = lq_ref[...]
    lam = (jnp.exp(jnp.sum(lq[0:1] * lq[1:2], axis=-1, keepdims=True))
           - jnp.exp(jnp.sum(lq[2:3] * lq[3:4], axis=-1, keepdims=True)) + lambda_init)

    def load_queries(b):
        qt = qt_ref[:, b * tq:(b + 1) * tq].astype(F32)
        qz_scr[:V_DIM, :] = jnp.where(own_map, jnp.concatenate([qt, qt], axis=1), 0.0).astype(BF16)
        qz_scr[V_DIM:, :] = jnp.where(row == 0, -shift, 0.0).astype(BF16)

    def finish(b):
        acc = acc_scr.at[b]
        on = acc[:V_DIM, :] / jnp.sum(acc[V_DIM:, :], axis=0, keepdims=True)
        ot = on[:, :tq] - lam * on[:, tq:]
        ms = jnp.mean(ot * ot, axis=0, keepdims=True)
        o_ref[:, b * tq:(b + 1) * tq] = (ot * lax.rsqrt(ms + SUBLN_EPS)
                                         * (g_ref[...] * (1.0 - lambda_init))).astype(BF16)

    def bounded_shift_softmax(b):
        def shifted_scores(kb):
            return jnp.dot(k_ref[kb * ATTN_QK:(kb + 1) * ATTN_QK, :], qz_scr[...],
                           preferred_element_type=F32)

        n_blocks = n_keys // ATTN_QK
        pv = [None] * n_col
        psum = [jnp.zeros((SUBLANES, MXU_DIM), F32)] * n_col
        s_next = shifted_scores(0)
        for kb in range(n_blocks):
            s = s_next
            if kb + 1 < n_blocks:
                s_next = shifted_scores(kb + 1)
            for sub in range(ATTN_QK // MXU_DIM):
                blk, off = divmod(kb * ATTN_QK + sub * MXU_DIM, tk)
                rows = slice(sub * MXU_DIM, (sub + 1) * MXU_DIM)
                for nh in range(n_col):
                    p = jnp.exp2(s[rows, nh * MXU_DIM:(nh + 1) * MXU_DIM])
                    psum[nh] = psum[nh] + _sublane_partial_sum(p)
                    d = jnp.dot(vt_ref[blk, :, off:off + MXU_DIM], p.astype(BF16),
                                preferred_element_type=F32)
                    pv[nh] = d if pv[nh] is None else pv[nh] + d
        for nh in range(n_col):
            cols = slice(nh * MXU_DIM, (nh + 1) * MXU_DIM)
            acc_scr[b, :V_DIM, cols] = pv[nh]
            acc_scr[b, V_DIM:, cols] = psum[nh]

    def online_softmax(b):
        n_chunks = n_kv // ATTN_SUB
        ck = ATTN_SUB * tk
        acc = acc_scr.at[b]
        acc[...] = jnp.zeros_like(acc)
        m_scr[...] = jnp.full_like(m_scr, -jnp.inf)

        def scores(j, s_scr, c_scr):
            r = pl.multiple_of(j * ck, ck)
            s = jnp.dot(k_ref[pl.ds(r, ck), :], qz_scr[...],
                        preferred_element_type=F32)
            s_scr[...] = s
            c_scr[...] = jnp.max(s, axis=0, keepdims=True)

        def consume(j, s_scr, c_scr):
            m_prev = m_scr[...]
            m_new = jnp.maximum(m_prev, c_scr[...])
            alpha = jnp.exp2(m_prev - m_new)
            for nh in range(n_col):
                cols = slice(nh * MXU_DIM, (nh + 1) * MXU_DIM)
                pv = None
                psum = jnp.zeros((SUBLANES, MXU_DIM), F32)
                for kb in range(ck // MXU_DIM):
                    p = jnp.exp2(s_scr[kb * MXU_DIM:(kb + 1) * MXU_DIM, cols] - m_new[:, cols])
                    psum = psum + _sublane_partial_sum(p)
                    blk, off = divmod(kb * MXU_DIM, tk)
                    d = jnp.dot(vt_ref[j * ATTN_SUB + blk, :, off:off + MXU_DIM], p.astype(BF16),
                                preferred_element_type=F32)
                    pv = d if pv is None else pv + d
                acc[:V_DIM, cols] = acc[:V_DIM, cols] * alpha[:, cols] + pv
                acc[V_DIM:, cols] = acc[V_DIM:, cols] * alpha[:, cols] + psum
            m_scr[...] = m_new

        scores(0, s0_scr, c0_scr)

        def body(i, carry):
            j = 2 * i
            scores(j + 1, s1_scr, c1_scr)
            consume(j, s0_scr, c0_scr)
            scores(jnp.minimum(j + 2, n_chunks - 1), s0_scr, c0_scr)
            consume(j + 1, s1_scr, c1_scr)
            return carry

        lax.fori_loop(0, n_chunks // 2, body, 0)

    @pl.when(fast)
    def _():
        for b in range(n_sub):
            load_queries(b)
            bounded_shift_softmax(b)
            finish(b)

    @pl.when(jnp.logical_not(fast))
    def _():
        for b in range(n_sub):
            load_queries(b)
            online_softmax(b)
            finish(b)


def _attention(qt, qn, k, kn, vt, lambda_qk, subln_g, lambda_init):
    n_heads, n_tiles, _, _ = qt.shape
    s = k.shape[1]
    n_kv = vt.shape[1]
    kmax = jnp.max(kn, axis=0)
    kb = jnp.stack([kmax[:, 0], kmax[:, HEAD_DIM]], axis=1).reshape(1, 2 * n_heads)
    qb = jnp.stack([qn[:, :, 0], qn[:, :, HEAD_DIM]], axis=2).reshape(n_tiles, 2 * n_heads)
    return pl.pallas_call(
        functools.partial(_attn_kernel, lambda_init=lambda_init),
        grid=(n_heads, n_tiles),
        in_specs=[
            pl.BlockSpec(memory_space=pltpu.SMEM),
            pl.BlockSpec(memory_space=pltpu.SMEM),
            pl.BlockSpec(lambda_qk.shape, lambda h, i: (0, 0)),
            pl.BlockSpec((V_DIM, 1), lambda h, i: (0, 0)),
            pl.BlockSpec((None, None, V_DIM, ROW_TILE), lambda h, i: (h, i, 0, 0)),
            pl.BlockSpec((None, s, K_LANES), lambda h, i: (h, 0, 0)),
            pl.BlockSpec((None, n_kv, V_DIM, ATTN_TK), lambda h, i: (h, 0, 0, 0)),
        ],
        out_specs=pl.BlockSpec((V_DIM, ROW_TILE), lambda h, i: (h, i)),
        out_shape=jax.ShapeDtypeStruct((n_heads * V_DIM, s), BF16),
        scratch_shapes=[pltpu.VMEM((K_LANES, 2 * ATTN_TQ), BF16),
                        pltpu.VMEM((ATTN_SUB * ATTN_TK, 2 * ATTN_TQ), F32),
                        pltpu.VMEM((ATTN_SUB * ATTN_TK, 2 * ATTN_TQ), F32),
                        pltpu.VMEM((1, 2 * ATTN_TQ), F32),
                        pltpu.VMEM((1, 2 * ATTN_TQ), F32),
                        pltpu.VMEM((ROW_TILE // ATTN_TQ, ACC_ROWS, 2 * ATTN_TQ), F32),
                        pltpu.VMEM((1, 2 * ATTN_TQ), F32)],
        compiler_params=_cparams(("parallel", "parallel")),
        name="diff_attn",
    )(qb, kb, lambda_qk, subln_g.reshape(V_DIM, 1), qt, k, vt)


def _dft_tables():
    n = FFT_N
    jk = np.outer(np.arange(n), np.arange(n)) % n
    ang = 2.0 * np.pi * jk / n
    c = np.cos(ang) / math.sqrt(n)
    s = np.sin(ang) / math.sqrt(n)
    stage1 = np.concatenate([c, -s], axis=0)
    stage2 = np.block([[c, s], [-s, c]])
    chan = np.concatenate([c, s], axis=0)
    tw_ang = 2.0 * np.pi * np.outer(np.arange(n), np.arange(n)) / (n * n)
    return (_split(jnp.asarray(stage1, F32)), _split(jnp.asarray(stage2, F32)),
            _split(jnp.asarray(chan, F32)),
            jnp.asarray(np.cos(tw_ang), F32), jnp.asarray(np.sin(tw_ang), F32))


def _split(a):
    hi = a.astype(BF16)
    return hi, (a - hi.astype(F32)).astype(BF16)


def _dot3(a, b):
    (a_hi, a_lo), (b_hi, b_lo) = a, b
    return (jnp.dot(a_hi, b_hi, preferred_element_type=F32)
            + jnp.dot(a_hi, b_lo, preferred_element_type=F32)
            + jnp.dot(a_lo, b_hi, preferred_element_type=F32))


def _fft1_kernel(x_ref, fh_ref, fl_ref, twc_ref, tws_ref, o_ref):
    n = FFT_N
    t = _dot3((fh_ref[...], fl_ref[...]), _split(x_ref[...]))
    width = x_ref.shape[1] // FFT_NB
    for b in range(FFT_NB):
        tr = t[:n, b * width:(b + 1) * width]
        ti = t[n:, b * width:(b + 1) * width]
        c = jnp.concatenate([twc_ref[b]] * (width // LANES), axis=1)
        s = jnp.concatenate([tws_ref[b]] * (width // LANES), axis=1)
        o_ref[0, :, b * width:(b + 1) * width] = tr * c + ti * s
        o_ref[1, :, b * width:(b + 1) * width] = ti * c - tr * s


def _fft2_kernel(t_ref, fh_ref, fl_ref, chh_ref, chl_ref, o_ref):
    n = FFT_N
    width = t_ref.shape[3]
    f = (fh_ref[...], fl_ref[...])
    ch = (chh_ref[...], chl_ref[...])
    for b in range(FFT_KB):
        tt = jnp.concatenate([t_ref[0, b], t_ref[1, b]], axis=0)
        z = _dot3(f, _split(tt))
        for g in range(N_GROUPS):
            zz = jnp.concatenate([z[:n, g * GROUP_DIM:(g + 1) * GROUP_DIM],
                                  z[n:, g * GROUP_DIM:(g + 1) * GROUP_DIM]], axis=1)
            y = _dot3(_split(zz), ch)
            o_ref[:, b * width + g * GROUP_DIM:b * width + (g + 1) * GROUP_DIM] = y.astype(o_ref.dtype)


def _fourier_mix(u, tables):
    s, width = u.shape
    n = FFT_N
    stage1, stage2, chan, twc, tws = tables
    twc_b = jnp.broadcast_to(twc[:, :, None], (n, n, LANES))
    tws_b = jnp.broadcast_to(tws[:, :, None], (n, n, LANES))
    t = pl.pallas_call(
        _fft1_kernel,
        grid=(n // FFT_NB,),
        in_specs=[
            pl.BlockSpec((n, FFT_NB * width), lambda j: (0, j)),
            pl.BlockSpec((2 * n, n), lambda j: (0, 0)),
            pl.BlockSpec((2 * n, n), lambda j: (0, 0)),
            pl.BlockSpec((FFT_NB, n, LANES), lambda j: (j, 0, 0)),
            pl.BlockSpec((FFT_NB, n, LANES), lambda j: (j, 0, 0)),
        ],
        out_specs=pl.BlockSpec((2, n, FFT_NB * width), lambda j: (0, 0, j)),
        out_shape=jax.ShapeDtypeStruct((2, n, n * width), F32),
        compiler_params=_cparams(("parallel",)),
        name="fft_stage1",
    )(u.reshape(n, n * width), *stage1, twc_b, tws_b)
    y = pl.pallas_call(
        _fft2_kernel,
        grid=(n // FFT_KB,),
        in_specs=[
            pl.BlockSpec((2, FFT_KB, n, width), lambda i: (0, i, 0, 0)),
            pl.BlockSpec((2 * n, 2 * n), lambda i: (0, 0)),
            pl.BlockSpec((2 * n, 2 * n), lambda i: (0, 0)),
            pl.BlockSpec((2 * n, n), lambda i: (0, 0)),
            pl.BlockSpec((2 * n, n), lambda i: (0, 0)),
        ],
        out_specs=pl.BlockSpec((n, FFT_KB * width), lambda i: (0, i)),
        out_shape=jax.ShapeDtypeStruct((n, n * width), BF16),
        compiler_params=_cparams(("parallel",)),
        name="fft_stage2",
    )(t.reshape(2, n, n, width), *stage2, *chan)
    return y.reshape(s, width)


def _mixout_kernel(x_ref, pg_ref, sh_ref, sc_ref, gt_ref, qg_ref, ao_ref, fy_ref,
                   gwa_ref, gwf_ref, gba_ref, gbf_ref, ap_ref, fp_ref, wo_ref,
                   o_ref, h_scr):
    j = pl.program_id(1)
    last = pl.num_programs(1) - 1

    def merged_out(rows):
        h = h_scr[rows, :]
        ga = jax.nn.sigmoid(jnp.dot(h, gwa_ref[...], preferred_element_type=F32) + gba_ref[...])
        gf = jax.nn.sigmoid(jnp.dot(h, gwf_ref[...], preferred_element_type=F32) + gbf_ref[...])
        ya = lax.dot_general(ao_ref[:, rows], ap_ref[...], (((0,), (0,)), ((), ())),
                             preferred_element_type=F32)
        yf = jnp.dot(fy_ref[rows, :], fp_ref[...], preferred_element_type=F32)
        y = (ga * ya + gf * yf).astype(BF16)
        return jnp.dot(y, wo_ref[...], preferred_element_type=F32)

    blocks = [slice(r, r + EDGE_ROWS) for r in range(0, x_ref.shape[0], EDGE_ROWS)]

    @pl.when(j == 0)
    def _():
        for rows in blocks:
            _modulated_norm(x_ref.at[rows], h_scr.at[rows], pg_ref, sc_ref, sh_ref)
            o_ref[rows, :] = merged_out(rows)

    @pl.when(jnp.logical_and(j > 0, j < last))
    def _():
        o_ref[...] += merged_out(slice(None))

    @pl.when(j == last)
    def _():
        for rows in blocks:
            o_ref[rows, :] += merged_out(rows)
            _gated_norm_residual(x_ref.at[rows], o_ref.at[rows], o_ref.at[rows], qg_ref, gt_ref, 1.0)


def _mix_out(x, pre_g, shift, scale, gate, post_g, ao, fy, gate_w, gate_b, attn_proj,
             fnet_proj, w_out, layer):
    s, d = x.shape
    nc = d // MIX_TC
    vec = pl.BlockSpec((1, d), lambda i, j: (0, 0))
    return pl.pallas_call(
        _mixout_kernel,
        grid=(s // ROW_TILE, nc),
        in_specs=[
            pl.BlockSpec((ROW_TILE, d), lambda i, j: (i, 0)),
            vec, vec, vec, vec, vec,
            pl.BlockSpec((ATTN_WIDTH, ROW_TILE), lambda i, j: (0, i)),
            pl.BlockSpec((ROW_TILE, FNET_WIDTH), lambda i, j: (i, 0)),
            pl.BlockSpec((None, d, MIX_TC), lambda i, j: (layer, 0, j)),
            pl.BlockSpec((None, d, MIX_TC), lambda i, j: (layer, 0, nc + j)),
            pl.BlockSpec((1, MIX_TC), lambda i, j: (0, j)),
            pl.BlockSpec((1, MIX_TC), lambda i, j: (0, nc + j)),
            pl.BlockSpec((None, ATTN_WIDTH, MIX_TC), lambda i, j: (layer, 0, j)),
            pl.BlockSpec((None, FNET_WIDTH, MIX_TC), lambda i, j: (layer, 0, j)),
            pl.BlockSpec((None, MIX_TC, d), lambda i, j: (layer, j, 0)),
        ],
        out_specs=pl.BlockSpec((ROW_TILE, d), lambda i, j: (i, 0)),
        out_shape=jax.ShapeDtypeStruct((s, d), F32),
        scratch_shapes=[pltpu.VMEM((ROW_TILE, d), BF16)],
        compiler_params=_cparams(("parallel", "arbitrary")),
        name="mix_out",
    )(x, pre_g, shift, scale, gate, post_g, ao, fy, gate_w, gate_w, gate_b, gate_b,
      attn_proj, fnet_proj, w_out)


def kernel(x, c, ada_w, ada_b, pre_norm_g, post_norm_g, ffn1_w_in, ffn1_w_out, mix_w_in,
           lambda_qk, subln_g, attn_proj, fnet_proj, branch_gate_w, branch_gate_b, mix_w_out,
           ffn2_w_in, ffn2_w_out):
    b, s, d = x.shape
    assert b == 1 and s == FFT_N * FFT_N and s % FFN_ROWS == 0 and s % ATTN_QK == 0
    n_layers = ada_w.shape[0]
    dft = _dft_tables()
    mod = _ada_mod(c, ada_w, ada_b).reshape(n_layers, 3, 3, 1, d)
    q_scale = HEAD_DIM ** -0.5 * math.log2(math.e)
    xs = x.reshape(s, d)
    ffn1_in, ffn1_out, ffn2_in, ffn2_out, mix_in_w, gate_w, attn_w, fnet_w, mix_out_w = (
        w.astype(BF16) for w in (ffn1_w_in, ffn1_w_out, ffn2_w_in, ffn2_w_out, mix_w_in,
                                 branch_gate_w, attn_proj, fnet_proj, mix_w_out))
    for l in range(n_layers):
        lambda_init = 0.8 - 0.6 * math.exp(-0.3 * l)
        pre = pre_norm_g[l].reshape(3, 1, d)
        post = post_norm_g[l].reshape(3, 1, d)
        xs = _ffn(xs, pre[0], mod[l, 0, 0], mod[l, 0, 1], mod[l, 0, 2], post[0],
                  ffn1_in, ffn1_out, l)
        qt, qn, k, kn, vt, u = _mix_in(xs, pre[1], mod[l, 1, 0], mod[l, 1, 1], mix_in_w, l, q_scale)
        ao = _attention(qt, qn, k, kn, vt, lambda_qk[l], subln_g[l], lambda_init)
        fy = _fourier_mix(u, dft)
        xs = _mix_out(xs, pre[1], mod[l, 1, 0], mod[l, 1, 1], mod[l, 1, 2], post[1], ao, fy,
                      gate_w, branch_gate_b[l].reshape(1, 2 * d), attn_w, fnet_w, mix_out_w, l)
        xs = _ffn(xs, pre[2], mod[l, 2, 0], mod[l, 2, 1], mod[l, 2, 2], post[2],
                  ffn2_in, ffn2_out, l)
    return xs.reshape(b, s, d)
```

```python
import functools
import math

import numpy as np
import jax
import jax.numpy as jnp
from jax import lax
from jax.experimental import pallas as pl
from jax.experimental.pallas import tpu as pltpu

F32 = jnp.float32
BF16 = jnp.bfloat16

N_HEADS = 8
HEAD_DIM = 64
V_DIM = 2 * HEAD_DIM
ACC_ROWS = V_DIM + 8
K_LANES = 2 * V_DIM
ATTN_WIDTH = N_HEADS * V_DIM
N_GROUPS = 8
GROUP_DIM = 128
FNET_WIDTH = N_GROUPS * GROUP_DIM
ROPE_THETA = 10000.0
NORM_EPS = 1e-6
SUBLN_EPS = 1e-5
MACARON_WEIGHT = 0.5

LANES = 128
SUBLANES = 8
NORM_ROWS = 16
MXU_DIM = 256
VMEM_LIMIT_BYTES = 56 * 1024 * 1024

ADA_TN = 2048
ROW_TILE = 512
FFN_ROWS = 1024
FFN_TF = 512
EDGE_ROWS = 256
MIX_TC = 512
ATTN_TQ = 256
ATTN_TK = ROW_TILE
ATTN_TILES = 2
ATTN_SUB = 2
ATTN_QK = 512
SHIFT_LIMIT = 60.0
BOUND_MARGIN = 1.0 + 2.0 ** -7
FFT_N = 128
FFT_NB = 8
FFT_KB = 8


def _cparams(sem):
    return pltpu.CompilerParams(dimension_semantics=sem, vmem_limit_bytes=VMEM_LIMIT_BYTES)


def _modulated_norm(x_ref, h_ref, g_ref, scale_ref, shift_ref):
    gs = g_ref[...] * (1.0 + scale_ref[...])
    shift = shift_ref[...]
    for r in range(0, x_ref.shape[0], NORM_ROWS):
        x = x_ref[r:r + NORM_ROWS, :]
        ms = jnp.mean(x * x, axis=-1, keepdims=True)
        h_ref[r:r + NORM_ROWS, :] = (x * lax.rsqrt(ms + NORM_EPS) * gs + shift).astype(h_ref.dtype)


def _gated_norm_residual(x_ref, y_ref, o_ref, g_ref, gate_ref, weight):
    coef = g_ref[...] * (weight * gate_ref[...])
    for r in range(0, x_ref.shape[0], NORM_ROWS):
        y = y_ref[r:r + NORM_ROWS, :]
        ms = jnp.mean(y * y, axis=-1, keepdims=True)
        o_ref[r:r + NORM_ROWS, :] = x_ref[r:r + NORM_ROWS, :] + y * lax.rsqrt(ms + NORM_EPS) * coef


def _ada_kernel(c_ref, w_ref, b_ref, o_ref):
    d, tn = w_ref.shape

    def body(k, acc):
        r = pl.multiple_of(k * SUBLANES, SUBLANES)
        cb = c_ref[pl.ds(r, SUBLANES), :]
        cb = cb * jax.nn.sigmoid(cb)
        return acc + w_ref[pl.ds(r, SUBLANES), :] * jnp.concatenate([cb] * (tn // LANES), axis=1)

    acc = lax.fori_loop(0, d // SUBLANES, body, jnp.zeros((SUBLANES, tn), F32), unroll=8)
    o_ref[...] = jnp.sum(acc, axis=0, keepdims=True) + b_ref[...]


def _ada_mod(c, ada_w, ada_b):
    n_layers, d, n = ada_w.shape
    c_b = jnp.broadcast_to(c.reshape(d, 1), (d, LANES))
    out = pl.pallas_call(
        _ada_kernel,
        grid=(n_layers, n // ADA_TN),
        in_specs=[
            pl.BlockSpec((d, LANES), lambda l, j: (0, 0)),
            pl.BlockSpec((None, d, ADA_TN), lambda l, j: (l, 0, j)),
            pl.BlockSpec((None, 1, ADA_TN), lambda l, j: (l, 0, j)),
        ],
        out_specs=pl.BlockSpec((None, 1, ADA_TN), lambda l, j: (l, 0, j)),
        out_shape=jax.ShapeDtypeStruct((n_layers, 1, n), F32),
        compiler_params=_cparams(("parallel", "parallel")),
        name="ada_mod",
    )(c_b, ada_w, ada_b.reshape(n_layers, 1, n))
    return out


def _ffn_kernel(x_ref, pg_ref, sh_ref, sc_ref, gt_ref, qg_ref, wg_ref, wu_ref, wo_ref,
                o_ref, h_scr):
    j = pl.program_id(1)
    last = pl.num_programs(1) - 1

    def up_down(rows):
        h = h_scr[rows, :]
        g = jnp.dot(h, wg_ref[...], preferred_element_type=F32)
        u = jnp.dot(h, wu_ref[...], preferred_element_type=F32)
        a = (g * jax.nn.sigmoid(g) * u).astype(BF16)
        return jnp.dot(a, wo_ref[...], preferred_element_type=F32)

    blocks = [slice(r, r + EDGE_ROWS) for r in range(0, x_ref.shape[0], EDGE_ROWS)]

    @pl.when(j == 0)
    def _():
        for rows in blocks:
            _modulated_norm(x_ref.at[rows], h_scr.at[rows], pg_ref, sc_ref, sh_ref)
            o_ref[rows, :] = up_down(rows)

    @pl.when(jnp.logical_and(j > 0, j < last))
    def _():
        o_ref[...] += up_down(slice(None))

    @pl.when(j == last)
    def _():
        for rows in blocks:
            o_ref[rows, :] += up_down(rows)
            _gated_norm_residual(x_ref.at[rows], o_ref.at[rows], o_ref.at[rows], qg_ref, gt_ref,
                                 MACARON_WEIGHT)


def _ffn(x, pre_g, shift, scale, gate, post_g, w_in, w_out, layer):
    s, d = x.shape
    f = w_out.shape[1]
    nf = f // FFN_TF
    vec = pl.BlockSpec((1, d), lambda i, j: (0, 0))
    return pl.pallas_call(
        _ffn_kernel,
        grid=(s // FFN_ROWS, nf),
        in_specs=[
            pl.BlockSpec((FFN_ROWS, d), lambda i, j: (i, 0)),
            vec, vec, vec, vec, vec,
            pl.BlockSpec((None, d, FFN_TF), lambda i, j: (layer, 0, j)),
            pl.BlockSpec((None, d, FFN_TF), lambda i, j: (layer, 0, nf + j)),
            pl.BlockSpec((None, FFN_TF, d), lambda i, j: (layer, j, 0)),
        ],
        out_specs=pl.BlockSpec((FFN_ROWS, d), lambda i, j: (i, 0)),
        out_shape=jax.ShapeDtypeStruct((s, d), F32),
        scratch_shapes=[pltpu.VMEM((FFN_ROWS, d), BF16)],
        compiler_params=pltpu.CompilerParams(
            dimension_semantics=("parallel", "arbitrary"), vmem_limit_bytes=VMEM_LIMIT_BYTES,
            allow_input_fusion=[False] * 6 + [True] * 3),
        name="ffn",
    )(x, pre_g, shift, scale, gate, post_g, w_in, w_in, w_out)


def _mixin_kernel(x_ref, pg_ref, sh_ref, sc_ref, w_ref, freq_ref,
                  qt_ref, qn_ref, k_ref, kn_ref, vt_ref, u_ref, h_scr, *, q_scale):
    n_rows = EDGE_ROWS
    lane = lax.broadcasted_iota(jnp.int32, (n_rows, LANES), 1)
    first_half = lax.rem(lane, HEAD_DIM) < HEAD_DIM // 2
    shift_lanes = jnp.where(lane == 0, 1.0, 0.0).astype(BF16)

    def max_sq_norms(t):
        t2 = t.astype(F32) * t.astype(F32)
        n1 = jnp.max(jnp.sum(jnp.where(lane < HEAD_DIM, t2, 0.0), axis=1, keepdims=True),
                     axis=0, keepdims=True)
        n2 = jnp.max(jnp.sum(jnp.where(lane < HEAD_DIM, 0.0, t2), axis=1, keepdims=True),
                     axis=0, keepdims=True)
        return jnp.where(lane[:1] < HEAD_DIM, n1, n2)

    qn = [None] * N_HEADS
    kn = [None] * N_HEADS
    for r0 in range(0, x_ref.shape[0], n_rows):
        rows = slice(r0, r0 + n_rows)
        _modulated_norm(x_ref.at[rows], h_scr.at[rows], pg_ref, sc_ref, sh_ref)
        h = h_scr[rows, :]
        pos = (pl.program_id(0) * x_ref.shape[0] + r0
               + lax.broadcasted_iota(jnp.int32, (n_rows, LANES), 0)).astype(F32)
        ang = pos * freq_ref[...]
        cos_t, sin = jnp.cos(ang), jnp.sin(ang)
        sin_n = jnp.where(first_half, -sin, 0.0)
        sin_p = jnp.where(first_half, 0.0, sin)

        def rope(zs):
            return (zs * cos_t + pltpu.roll(zs, LANES - HEAD_DIM // 2, 1) * sin_n
                    + pltpu.roll(zs, HEAD_DIM // 2, 1) * sin_p)

        zq = jnp.dot(h, w_ref[:, 0:ATTN_WIDTH], preferred_element_type=F32)
        for hd in range(N_HEADS):
            qr = (rope(zq[:, hd * V_DIM:(hd + 1) * V_DIM]) * q_scale).astype(BF16)
            qt_ref[hd, :, rows] = qr.astype(F32).T.astype(BF16)
            n = max_sq_norms(qr)
            qn[hd] = n if qn[hd] is None else jnp.maximum(qn[hd], n)
        zk = jnp.dot(h, w_ref[:, ATTN_WIDTH:2 * ATTN_WIDTH], preferred_element_type=F32)
        for hd in range(N_HEADS):
            kr = rope(zk[:, hd * V_DIM:(hd + 1) * V_DIM]).astype(BF16)
            k_ref[hd, rows, :V_DIM] = kr
            k_ref[hd, rows, V_DIM:] = shift_lanes
            n = max_sq_norms(kr)
            kn[hd] = n if kn[hd] is None else jnp.maximum(kn[hd], n)
        zvt = jnp.dot(h, w_ref[:, 2 * ATTN_WIDTH:3 * ATTN_WIDTH], preferred_element_type=F32).T
        for hd in range(N_HEADS):
            vt_ref[hd, :, rows] = zvt[hd * V_DIM:(hd + 1) * V_DIM, :].astype(BF16)
        u_ref[rows, :] = jnp.dot(h, w_ref[:, 3 * ATTN_WIDTH:], preferred_element_type=F32)
    for hd in range(N_HEADS):
        qn_ref[hd:hd + 1, :] = qn[hd]
        kn_ref[hd:hd + 1, :] = kn[hd]


def _mix_in(x, pre_g, shift, scale, w, layer, q_scale):
    s, d = x.shape
    n_tiles = s // ROW_TILE
    vec = pl.BlockSpec((1, d), lambda i: (0, 0))
    half = HEAD_DIM // 2
    inv_freq = ROPE_THETA ** (-jnp.arange(0, HEAD_DIM, 2, dtype=F32) / HEAD_DIM)
    freq = jnp.concatenate([inv_freq] * (LANES // half)).reshape(1, LANES)
    return pl.pallas_call(
        functools.partial(_mixin_kernel, q_scale=q_scale),
        grid=(n_tiles,),
        in_specs=[
            pl.BlockSpec((ROW_TILE, d), lambda i: (i, 0)),
            vec, vec, vec,
            pl.BlockSpec((None,) + w.shape[1:], lambda i: (layer, 0, 0), pipeline_mode=pl.Buffered(1)),
            pl.BlockSpec((1, LANES), lambda i: (0, 0)),
        ],
        out_specs=[
            pl.BlockSpec((N_HEADS, None, V_DIM, ROW_TILE), lambda i: (0, i, 0, 0)),
            pl.BlockSpec((None, N_HEADS, LANES), lambda i: (i, 0, 0)),
            pl.BlockSpec((N_HEADS, ROW_TILE, K_LANES), lambda i: (0, i, 0)),
            pl.BlockSpec((None, N_HEADS, LANES), lambda i: (i, 0, 0)),
            pl.BlockSpec((N_HEADS, None, V_DIM, ROW_TILE), lambda i: (0, i, 0, 0)),
            pl.BlockSpec((ROW_TILE, FNET_WIDTH), lambda i: (i, 0)),
        ],
        out_shape=[
            jax.ShapeDtypeStruct((N_HEADS, n_tiles, V_DIM, ROW_TILE), BF16),
            jax.ShapeDtypeStruct((n_tiles, N_HEADS, LANES), F32),
            jax.ShapeDtypeStruct((N_HEADS, s, K_LANES), BF16),
            jax.ShapeDtypeStruct((n_tiles, N_HEADS, LANES), F32),
            jax.ShapeDtypeStruct((N_HEADS, n_tiles, V_DIM, ROW_TILE), BF16),
            jax.ShapeDtypeStruct((s, FNET_WIDTH), F32),
        ],
        scratch_shapes=[pltpu.VMEM((ROW_TILE, d), BF16)],
        compiler_params=_cparams(("parallel",)),
        name="mix_in",
    )(x, pre_g, shift, scale, w, freq)


def _sublane_partial_sum(p):
    return p.reshape(p.shape[0] // SUBLANES, SUBLANES, p.shape[1]).sum(axis=0)


def _attn_kernel(qb_ref, kb_ref, lq_ref, g_ref, qt_ref, k_ref, vt_ref, o_ref,
                 qz_scr, s0_scr, s1_scr, c0_scr, c1_scr, acc_scr, m_scr, *, lambda_init):
    tq = ATTN_TQ
    n_tiles_step, _, row_tile = qt_ref.shape
    per_tile = row_tile // tq
    n_sub = n_tiles_step * per_tile
    n_kv, _, tk = vt_ref.shape
    n_keys = n_kv * tk
    n_col = 2 * tq // MXU_DIM
    hd = pl.program_id(0)
    tile0 = pl.program_id(1) * n_tiles_step
    bsq = []
    for m in range(2):
        qmax = qb_ref[tile0, 2 * hd + m]
        for t in range(1, n_tiles_step):
            qmax = jnp.maximum(qmax, qb_ref[tile0 + t, 2 * hd + m])
        bsq.append(qmax * kb_ref[0, 2 * hd + m])
    fast = jnp.maximum(bsq[0], bsq[1]) <= (SHIFT_LIMIT / BOUND_MARGIN) ** 2
    shift = jnp.sqrt(jnp.concatenate(
        [jnp.full((1, tq), jnp.where(fast, b, 0.0), F32) for b in bsq], axis=1)) * BOUND_MARGIN
    row = lax.broadcasted_iota(jnp.int32, (V_DIM, 2 * tq), 0)
    own_map = (row < HEAD_DIM) == (lax.broadcasted_iota(jnp.int32, (V_DIM, 2 * tq), 1) < tq)
    lq = lq_ref[...]
    lam = (jnp.exp(jnp.sum(lq[0:1] * lq[1:2], axis=-1, keepdims=True))
           - jnp.exp(jnp.sum(lq[2:3] * lq[3:4], axis=-1, keepdims=True)) + lambda_init)

    def load_queries(b):
        t, c = divmod(b, per_tile)
        qt = qt_ref[t, :, c * tq:(c + 1) * tq].astype(F32)
        qz_scr[b, :V_DIM, :] = jnp.where(own_map, jnp.concatenate([qt, qt], axis=1), 0.0).astype(BF16)
        qz_scr[b, V_DIM:, :] = jnp.where(row == 0, -shift, 0.0).astype(BF16)

    def finish(b):
        acc = acc_scr.at[b]
        on = acc[:V_DIM, :] / jnp.sum(acc[V_DIM:, :], axis=0, keepdims=True)
        ot = on[:, :tq] - lam * on[:, tq:]
        ms = jnp.mean(ot * ot, axis=0, keepdims=True)
        o_ref[:, b * tq:(b + 1) * tq] = (ot * lax.rsqrt(ms + SUBLN_EPS)
                                         * (g_ref[...] * (1.0 - lambda_init))).astype(BF16)

    def bounded_shift_softmax(b):
        def shifted_scores(kb):
            return jnp.dot(k_ref[kb * ATTN_QK:(kb + 1) * ATTN_QK, :], qz_scr[b],
                           preferred_element_type=F32)

        n_blocks = n_keys // ATTN_QK
        pv = [None] * n_col
        psum = [jnp.zeros((SUBLANES, MXU_DIM), F32)] * n_col
        s_next = shifted_scores(0)
        for kb in range(n_blocks):
            s = s_next
            if kb + 1 < n_blocks:
                s_next = shifted_scores(kb + 1)
            for sub in range(ATTN_QK // MXU_DIM):
                blk, off = divmod(kb * ATTN_QK + sub * MXU_DIM, tk)
                rows = slice(sub * MXU_DIM, (sub + 1) * MXU_DIM)
                for nh in range(n_col):
                    p = jnp.exp2(s[rows, nh * MXU_DIM:(nh + 1) * MXU_DIM])
                    psum[nh] = psum[nh] + _sublane_partial_sum(p)
                    d = jnp.dot(vt_ref[blk, :, off:off + MXU_DIM], p.astype(BF16),
                                preferred_element_type=F32)
                    pv[nh] = d if pv[nh] is None else pv[nh] + d
        for nh in range(n_col):
            cols = slice(nh * MXU_DIM, (nh + 1) * MXU_DIM)
            acc_scr[b, :V_DIM, cols] = pv[nh]
            acc_scr[b, V_DIM:, cols] = psum[nh]

    def online_softmax(b):
        n_chunks = n_kv // ATTN_SUB
        ck = ATTN_SUB * tk
        acc = acc_scr.at[b]
        acc[...] = jnp.zeros_like(acc)
        m_scr[...] = jnp.full_like(m_scr, -jnp.inf)

        def scores(j, s_scr, c_scr):
            r = pl.multiple_of(j * ck, ck)
            s = jnp.dot(k_ref[pl.ds(r, ck), :], qz_scr[b],
                        preferred_element_type=F32)
            s_scr[...] = s
            c_scr[...] = jnp.max(s, axis=0, keepdims=True)

        def consume(j, s_scr, c_scr):
            m_prev = m_scr[...]
            m_new = jnp.maximum(m_prev, c_scr[...])
            alpha = jnp.exp2(m_prev - m_new)
            for nh in range(n_col):
                cols = slice(nh * MXU_DIM, (nh + 1) * MXU_DIM)
                pv = None
                psum = jnp.zeros((SUBLANES, MXU_DIM), F32)
                for kb in range(ck // MXU_DIM):
                    p = jnp.exp2(s_scr[kb * MXU_DIM:(kb + 1) * MXU_DIM, cols] - m_new[:, cols])
                    psum = psum + _sublane_partial_sum(p)
                    blk, off = divmod(kb * MXU_DIM, tk)
                    d = jnp.dot(vt_ref[j * ATTN_SUB + blk, :, off:off + MXU_DIM], p.astype(BF16),
                                preferred_element_type=F32)
                    pv = d if pv is None else pv + d
                acc[:V_DIM, cols] = acc[:V_DIM, cols] * alpha[:, cols] + pv
                acc[V_DIM:, cols] = acc[V_DIM:, cols] * alpha[:, cols] + psum
            m_scr[...] = m_new

        scores(0, s0_scr, c0_scr)

        def body(i, carry):
            j = 2 * i
            scores(j + 1, s1_scr, c1_scr)
            consume(j, s0_scr, c0_scr)
            scores(jnp.minimum(j + 2, n_chunks - 1), s0_scr, c0_scr)
            consume(j + 1, s1_scr, c1_scr)
            return carry

        lax.fori_loop(0, n_chunks // 2, body, 0)

    @pl.when(fast)
    def _():
        for b in range(n_sub):
            load_queries(b)
            bounded_shift_softmax(b)
            finish(b)

    @pl.when(jnp.logical_not(fast))
    def _():
        for b in range(n_sub):
            load_queries(b)
            online_softmax(b)
            finish(b)


def _attention(qt, qn, k, kn, vt, lambda_qk, subln_g, lambda_init):
    n_heads, n_tiles, _, _ = qt.shape
    s = k.shape[1]
    n_kv = vt.shape[1]
    kmax = jnp.max(kn, axis=0)
    kb = jnp.stack([kmax[:, 0], kmax[:, HEAD_DIM]], axis=1).reshape(1, 2 * n_heads)
    qb = jnp.stack([qn[:, :, 0], qn[:, :, HEAD_DIM]], axis=2).reshape(n_tiles, 2 * n_heads)
    return pl.pallas_call(
        functools.partial(_attn_kernel, lambda_init=lambda_init),
        grid=(n_heads, n_tiles // ATTN_TILES),
        in_specs=[
            pl.BlockSpec(memory_space=pltpu.SMEM),
            pl.BlockSpec(memory_space=pltpu.SMEM),
            pl.BlockSpec(lambda_qk.shape, lambda h, i: (0, 0)),
            pl.BlockSpec((V_DIM, 1), lambda h, i: (0, 0)),
            pl.BlockSpec((None, ATTN_TILES, V_DIM, ROW_TILE), lambda h, i: (h, i, 0, 0)),
            pl.BlockSpec((None, s, K_LANES), lambda h, i: (h, 0, 0)),
            pl.BlockSpec((None, n_kv, V_DIM, ATTN_TK), lambda h, i: (h, 0, 0, 0)),
        ],
        out_specs=pl.BlockSpec((V_DIM, ATTN_TILES * ROW_TILE), lambda h, i: (h, i)),
        out_shape=jax.ShapeDtypeStruct((n_heads * V_DIM, s), BF16),
        scratch_shapes=[pltpu.VMEM((ATTN_TILES * ROW_TILE // ATTN_TQ, K_LANES, 2 * ATTN_TQ), BF16),
                        pltpu.VMEM((ATTN_SUB * ATTN_TK, 2 * ATTN_TQ), F32),
                        pltpu.VMEM((ATTN_SUB * ATTN_TK, 2 * ATTN_TQ), F32),
                        pltpu.VMEM((1, 2 * ATTN_TQ), F32),
                        pltpu.VMEM((1, 2 * ATTN_TQ), F32),
                        pltpu.VMEM((ATTN_TILES * ROW_TILE // ATTN_TQ, ACC_ROWS, 2 * ATTN_TQ), F32),
                        pltpu.VMEM((1, 2 * ATTN_TQ), F32)],
        compiler_params=_cparams(("parallel", "parallel")),
        name="diff_attn",
    )(qb, kb, lambda_qk, subln_g.reshape(V_DIM, 1), qt, k, vt)


def _dft_tables():
    n = FFT_N
    jk = np.outer(np.arange(n), np.arange(n)) % n
    ang = 2.0 * np.pi * jk / n
    c = np.cos(ang) / math.sqrt(n)
    s = np.sin(ang) / math.sqrt(n)
    stage1 = np.concatenate([c, -s], axis=0)
    stage2 = np.block([[c, s], [-s, c]])
    chan = np.concatenate([c, s], axis=0)
    tw_ang = 2.0 * np.pi * np.outer(np.arange(n), np.arange(n)) / (n * n)
    return (_split(jnp.asarray(stage1, F32)), _split(jnp.asarray(stage2, F32)),
            _split(jnp.asarray(chan, F32)),
            jnp.asarray(np.cos(tw_ang), F32), jnp.asarray(np.sin(tw_ang), F32))


def _split(a):
    hi = a.astype(BF16)
    return hi, (a - hi.astype(F32)).astype(BF16)


def _dot3(a, b):
    (a_hi, a_lo), (b_hi, b_lo) = a, b
    return (jnp.dot(a_hi, b_hi, preferred_element_type=F32)
            + jnp.dot(a_hi, b_lo, preferred_element_type=F32)
            + jnp.dot(a_lo, b_hi, preferred_element_type=F32))


def _fft1_kernel(x_ref, fh_ref, fl_ref, twc_ref, tws_ref, o_ref):
    n = FFT_N
    t = _dot3((fh_ref[...], fl_ref[...]), _split(x_ref[...]))
    width = x_ref.shape[1] // FFT_NB
    for b in range(FFT_NB):
        tr = t[:n, b * width:(b + 1) * width]
        ti = t[n:, b * width:(b + 1) * width]
        c = jnp.concatenate([twc_ref[b]] * (width // LANES), axis=1)
        s = jnp.concatenate([tws_ref[b]] * (width // LANES), axis=1)
        o_ref[0, :, b * width:(b + 1) * width] = tr * c + ti * s
        o_ref[1, :, b * width:(b + 1) * width] = ti * c - tr * s


def _fft2_kernel(t_ref, fh_ref, fl_ref, chh_ref, chl_ref, o_ref):
    n = FFT_N
    width = t_ref.shape[3]
    f = (fh_ref[...], fl_ref[...])
    ch = (chh_ref[...], chl_ref[...])
    for b in range(FFT_KB):
        tt = jnp.concatenate([t_ref[0, b], t_ref[1, b]], axis=0)
        z = _dot3(f, _split(tt))
        for g in range(N_GROUPS):
            zz = jnp.concatenate([z[:n, g * GROUP_DIM:(g + 1) * GROUP_DIM],
                                  z[n:, g * GROUP_DIM:(g + 1) * GROUP_DIM]], axis=1)
            y = _dot3(_split(zz), ch)
            o_ref[:, b * width + g * GROUP_DIM:b * width + (g + 1) * GROUP_DIM] = y.astype(o_ref.dtype)


def _fourier_mix(u, tables):
    s, width = u.shape
    n = FFT_N
    stage1, stage2, chan, twc, tws = tables
    twc_b = jnp.broadcast_to(twc[:, :, None], (n, n, LANES))
    tws_b = jnp.broadcast_to(tws[:, :, None], (n, n, LANES))
    t = pl.pallas_call(
        _fft1_kernel,
        grid=(n // FFT_NB,),
        in_specs=[
            pl.BlockSpec((n, FFT_NB * width), lambda j: (0, j)),
            pl.BlockSpec((2 * n, n), lambda j: (0, 0)),
            pl.BlockSpec((2 * n, n), lambda j: (0, 0)),
            pl.BlockSpec((FFT_NB, n, LANES), lambda j: (j, 0, 0)),
            pl.BlockSpec((FFT_NB, n, LANES), lambda j: (j, 0, 0)),
        ],
        out_specs=pl.BlockSpec((2, n, FFT_NB * width), lambda j: (0, 0, j)),
        out_shape=jax.ShapeDtypeStruct((2, n, n * width), F32),
        compiler_params=_cparams(("parallel",)),
        name="fft_stage1",
    )(u.reshape(n, n * width), *stage1, twc_b, tws_b)
    y = pl.pallas_call(
        _fft2_kernel,
        grid=(n // FFT_KB,),
        in_specs=[
            pl.BlockSpec((2, FFT_KB, n, width), lambda i: (0, i, 0, 0)),
            pl.BlockSpec((2 * n, 2 * n), lambda i: (0, 0)),
            pl.BlockSpec((2 * n, 2 * n), lambda i: (0, 0)),
            pl.BlockSpec((2 * n, n), lambda i: (0, 0)),
            pl.BlockSpec((2 * n, n), lambda i: (0, 0)),
        ],
        out_specs=pl.BlockSpec((n, FFT_KB * width), lambda i: (0, i)),
        out_shape=jax.ShapeDtypeStruct((n, n * width), BF16),
        compiler_params=_cparams(("parallel",)),
        name="fft_stage2",
    )(t.reshape(2, n, n, width), *stage2, *chan)
    return y.reshape(s, width)


def _mixout_kernel(x_ref, pg_ref, sh_ref, sc_ref, gt_ref, qg_ref, ao_ref, fy_ref,
                   gwa_ref, gwf_ref, gba_ref, gbf_ref, ap_ref, fp_ref, wo_ref,
                   o_ref, h_scr):
    j = pl.program_id(1)
    last = pl.num_programs(1) - 1

    def merged_out(rows):
        h = h_scr[rows, :]
        ga = jax.nn.sigmoid(jnp.dot(h, gwa_ref[...], preferred_element_type=F32) + gba_ref[...])
        gf = jax.nn.sigmoid(jnp.dot(h, gwf_ref[...], preferred_element_type=F32) + gbf_ref[...])
        ya = lax.dot_general(ao_ref[:, rows], ap_ref[...], (((0,), (0,)), ((), ())),
                             preferred_element_type=F32)
        yf = jnp.dot(fy_ref[rows, :], fp_ref[...], preferred_element_type=F32)
        y = (ga * ya + gf * yf).astype(BF16)
        return jnp.dot(y, wo_ref[...], preferred_element_type=F32)

    blocks = [slice(r, r + EDGE_ROWS) for r in range(0, x_ref.shape[0], EDGE_ROWS)]

    @pl.when(j == 0)
    def _():
        for rows in blocks:
            _modulated_norm(x_ref.at[rows], h_scr.at[rows], pg_ref, sc_ref, sh_ref)
            o_ref[rows, :] = merged_out(rows)

    @pl.when(jnp.logical_and(j > 0, j < last))
    def _():
        o_ref[...] += merged_out(slice(None))

    @pl.when(j == last)
    def _():
        for rows in blocks:
            o_ref[rows, :] += merged_out(rows)
            _gated_norm_residual(x_ref.at[rows], o_ref.at[rows], o_ref.at[rows], qg_ref, gt_ref, 1.0)


def _mix_out(x, pre_g, shift, scale, gate, post_g, ao, fy, gate_w, gate_b, attn_proj,
             fnet_proj, w_out, layer):
    s, d = x.shape
    nc = d // MIX_TC
    vec = pl.BlockSpec((1, d), lambda i, j: (0, 0))
    return pl.pallas_call(
        _mixout_kernel,
        grid=(s // ROW_TILE, nc),
        in_specs=[
            pl.BlockSpec((ROW_TILE, d), lambda i, j: (i, 0)),
            vec, vec, vec, vec, vec,
            pl.BlockSpec((ATTN_WIDTH, ROW_TILE), lambda i, j: (0, i)),
            pl.BlockSpec((ROW_TILE, FNET_WIDTH), lambda i, j: (i, 0)),
            pl.BlockSpec((None, d, MIX_TC), lambda i, j: (layer, 0, j)),
            pl.BlockSpec((None, d, MIX_TC), lambda i, j: (layer, 0, nc + j)),
            pl.BlockSpec((1, MIX_TC), lambda i, j: (0, j)),
            pl.BlockSpec((1, MIX_TC), lambda i, j: (0, nc + j)),
            pl.BlockSpec((None, ATTN_WIDTH, MIX_TC), lambda i, j: (layer, 0, j)),
            pl.BlockSpec((None, FNET_WIDTH, MIX_TC), lambda i, j: (layer, 0, j)),
            pl.BlockSpec((None, MIX_TC, d), lambda i, j: (layer, j, 0)),
        ],
        out_specs=pl.BlockSpec((ROW_TILE, d), lambda i, j: (i, 0)),
        out_shape=jax.ShapeDtypeStruct((s, d), F32),
        scratch_shapes=[pltpu.VMEM((ROW_TILE, d), BF16)],
        compiler_params=_cparams(("parallel", "arbitrary")),
        name="mix_out",
    )(x, pre_g, shift, scale, gate, post_g, ao, fy, gate_w, gate_w, gate_b, gate_b,
      attn_proj, fnet_proj, w_out)


def kernel(x, c, ada_w, ada_b, pre_norm_g, post_norm_g, ffn1_w_in, ffn1_w_out, mix_w_in,
           lambda_qk, subln_g, attn_proj, fnet_proj, branch_gate_w, branch_gate_b, mix_w_out,
           ffn2_w_in, ffn2_w_out):
    b, s, d = x.shape
    assert b == 1 and s == FFT_N * FFT_N and s % FFN_ROWS == 0 and s % ATTN_QK == 0
    n_layers = ada_w.shape[0]
    dft = _dft_tables()
    mod = _ada_mod(c, ada_w, ada_b).reshape(n_layers, 3, 3, 1, d)
    q_scale = HEAD_DIM ** -0.5 * math.log2(math.e)
    xs = x.reshape(s, d)
    ffn1_in, ffn1_out, ffn2_in, ffn2_out, mix_in_w, gate_w, attn_w, fnet_w, mix_out_w = (
        w.astype(BF16) for w in (ffn1_w_in, ffn1_w_out, ffn2_w_in, ffn2_w_out, mix_w_in,
                                 branch_gate_w, attn_proj, fnet_proj, mix_w_out))
    for l in range(n_layers):
        lambda_init = 0.8 - 0.6 * math.exp(-0.3 * l)
        pre = pre_norm_g[l].reshape(3, 1, d)
        post = post_norm_g[l].reshape(3, 1, d)
        xs = _ffn(xs, pre[0], mod[l, 0, 0], mod[l, 0, 1], mod[l, 0, 2], post[0],
                  ffn1_in, ffn1_out, l)
        qt, qn, k, kn, vt, u = _mix_in(xs, pre[1], mod[l, 1, 0], mod[l, 1, 1], mix_in_w, l, q_scale)
        ao = _attention(qt, qn, k, kn, vt, lambda_qk[l], subln_g[l], lambda_init)
        fy = _fourier_mix(u, dft)
        xs = _mix_out(xs, pre[1], mod[l, 1, 0], mod[l, 1, 1], mod[l, 1, 2], post[1], ao, fy,
                      gate_w, branch_gate_b[l].reshape(1, 2 * d), attn_w, fnet_w, mix_out_w, l)
        xs = _ffn(xs, pre[2], mod[l, 2, 0], mod[l, 2, 1], mod[l, 2, 2], post[2],
                  ffn2_in, ffn2_out, l)
    return xs.reshape(b, s, d)
```
